```python
import jax, jax.numpy as jnp
from jax import lax
import numpy as np

D_MODEL = 1024
BATCH = 16
SEQ = 256
DEPTH = 2
DEC_BATCH = 4
DEC_SEQ = 2048
PAST_LEN = 512

GRID_W = 64
ROPE_THETA = 10000.0
EPS = 1e-6
Q_BLOCK = 128
N_MOD = 6

HEADS_A = 8
KV_HEADS_A = 2
HD_A = 64
GROUP_A = HEADS_A // KV_HEADS_A
HEADS_B = 8
Q_LORA = 384
KV_LORA = 256
NOPE_B = 64
ROPE_B = 32
VD_B = 64
HEADS_C = 8
DK_C = 64
DV_C = 64
CONV_K = 3
CHUNK = 64
CONV_CH = 2 * HEADS_C * DK_C + HEADS_C * DV_C
D_FF = 256 * ((8 * D_MODEL + 3 * 256 - 1) // (3 * 256))

SPLIT_SIZES = (
    HEADS_A * HD_A, KV_HEADS_A * HD_A, KV_HEADS_A * HD_A,
    Q_LORA, KV_LORA, ROPE_B,
    CONV_CH, HEADS_C * DV_C, 2 * HEADS_C, 2 * HEADS_C,
    3 * D_MODEL,
)
IN_COLS = sum(SPLIT_SIZES)

kernel_name = 'hybrid_prefix_diffusion_trunk'


def split_cols(x, sizes):
    offs = np.cumsum(sizes)[:-1].tolist()
    return jnp.split(x, offs, axis=-1)


def rms_norm(x, g):
    xf = x.astype(jnp.float32)
    y = xf * lax.rsqrt(jnp.mean(xf * xf, axis=-1, keepdims=True) + EPS)
    return (y * g.astype(jnp.float32)).astype(x.dtype)


def l2_norm(x):
    xf = x.astype(jnp.float32)
    return xf * lax.rsqrt(jnp.sum(xf * xf, axis=-1, keepdims=True) + EPS)


def axial_angles(n, rot_dim):
    rows = n // GRID_W
    row = jnp.repeat(jnp.arange(rows, dtype=jnp.float32), GRID_W)
    col = jnp.tile(jnp.arange(GRID_W, dtype=jnp.float32), rows)
    nf = rot_dim // 4
    inv_freq = ROPE_THETA ** (-jnp.arange(nf, dtype=jnp.float32) / nf)
    return row[:, None] * inv_freq, col[:, None] * inv_freq


def rotate(x, ang):
    x1, x2 = jnp.split(x, 2, axis=-1)
    cos = jnp.cos(ang)[None, :, None, :].astype(x.dtype)
    sin = jnp.sin(ang)[None, :, None, :].astype(x.dtype)
    return jnp.concatenate([x1 * cos - x2 * sin, x2 * cos + x1 * sin], axis=-1)


def axial_rope(x, angs):
    xr, xc = jnp.split(x, 2, axis=-1)
    return jnp.concatenate([rotate(xr, angs[0]), rotate(xc, angs[1])], axis=-1)


def block_attention(q, k, v, scale):
    b, s = q.shape[:2]
    nblk = s // Q_BLOCK
    qb = jnp.swapaxes(q.reshape((b, nblk, Q_BLOCK) + q.shape[2:]), 0, 1)

    def one_block(qi):
        sc = jnp.einsum('bqkgd,btkd->bkgqt', qi, k).astype(jnp.float32) * scale
        p = jax.nn.softmax(sc, axis=-1).astype(v.dtype)
        return jnp.einsum('bkgqt,btkd->bqkgd', p, v)

    out = lax.map(one_block, qb)
    return jnp.swapaxes(out, 0, 1).reshape((b, s) + out.shape[3:])


def attn_gqa(qa, ka, va, p, ctx, angs):
    b, n = qa.shape[:2]
    q = rms_norm(qa.reshape(b, n, HEADS_A, HD_A), p['a_qnorm'])
    k = rms_norm(ka.reshape(b, n, KV_HEADS_A, HD_A), p['a_knorm'])
    v = va.reshape(b, n, KV_HEADS_A, HD_A)
    if ctx is None:
        kk, vv, new = k, v, (k, v)
    else:
        q = axial_rope(q, angs)
        k = axial_rope(k, angs)
        kk = jnp.concatenate([ctx[0].astype(k.dtype), k], axis=1)
        vv = jnp.concatenate([ctx[1].astype(v.dtype), v], axis=1)
        new = None
    o = block_attention(q.reshape(b, n, KV_HEADS_A, GROUP_A, HD_A), kk, vv, HD_A ** -0.5)
    return o.reshape(b, n, HEADS_A * HD_A), new


def attn_mla(cq, ckv, kpe, p, ctx, angs):
    b, n = cq.shape[:2]
    q = (rms_norm(cq, p['b_qnorm']) @ p['w_uq']).reshape(b, n, HEADS_B, NOPE_B + ROPE_B)
    q_nope, q_pe = q[..., :NOPE_B], q[..., NOPE_B:]
    ckv = rms_norm(ckv, p['b_kvnorm'])
    if ctx is None:
        ckv_all, kpe_all, new = ckv, kpe, (ckv, kpe)
    else:
        q_pe = axial_rope(q_pe, angs)
        kpe = axial_rope(kpe[:, :, None, :], angs)[:, :, 0, :]
        ckv_all = jnp.concatenate([ctx[0].astype(ckv.dtype), ckv], axis=1)
        kpe_all = jnp.concatenate([ctx[1].astype(kpe.dtype), kpe], axis=1)
        new = None
    t = ckv_all.shape[1]
    kv = (ckv_all @ p['w_ukv']).reshape(b, t, HEADS_B, NOPE_B + VD_B)
    k = jnp.concatenate([kv[..., :NOPE_B],
                         jnp.broadcast_to(kpe_all[:, :, None, :], (b, t, HEADS_B, ROPE_B)).astype(kv.dtype)], axis=-1)
    qf = jnp.concatenate([q_nope, q_pe], axis=-1)[:, :, :, None, :]
    o = block_attention(qf, k, kv[..., NOPE_B:], (NOPE_B + ROPE_B) ** -0.5)
    return o.reshape(b, n, HEADS_B * VD_B), new


def chunk_gated_delta(q, k, v, g, beta, s0):
    b, n, h, _ = q.shape
    dv = v.shape[-1]
    nc = n // CHUNK

    def to_chunks(t):
        return jnp.transpose(t.reshape(b, nc, CHUNK, h, t.shape[-1]), (1, 0, 3, 2, 4))

    qc, kc, vc = to_chunks(q), to_chunks(k), to_chunks(v)
    gc = jnp.cumsum(to_chunks(g[..., None])[..., 0], axis=-1)
    bc = to_chunks(beta[..., None])[..., 0]
    idx = jnp.arange(CHUNK)
    lower = idx[:, None] >= idx[None, :]
    strict = idx[:, None] > idx[None, :]
    diff = gc[..., :, None] - gc[..., None, :]
    decay = jnp.where(lower, jnp.exp(jnp.where(lower, diff, 0.0)), 0.0)
    kb = kc * bc[..., None]
    a_mat = jnp.where(strict, jnp.einsum('nbhid,nbhjd->nbhij', kb, kc) * decay, 0.0)
    t_mat = a_mat + jnp.eye(CHUNK, dtype=jnp.float32)
    rhs = jnp.concatenate([vc * bc[..., None], kb * jnp.exp(gc)[..., None]], axis=-1)
    sol = lax.linalg.triangular_solve(t_mat, rhs, left_side=True, lower=True, unit_diagonal=True)
    u, w = sol[..., :dv], sol[..., dv:]
    qk = jnp.einsum('nbhid,nbhjd->nbhij', qc, kc) * decay

    def step(s, xs):
        q_i, k_i, u_i, w_i, g_i, qk_i = xs
        v_new = u_i - jnp.einsum('bhck,bhkv->bhcv', w_i, s)
        o = (jnp.einsum('bhck,bhkv->bhcv', q_i * jnp.exp(g_i)[..., None], s)
             + jnp.einsum('bhij,bhjv->bhiv', qk_i, v_new))
        g_last = g_i[..., -1:]
        s = (s * jnp.exp(g_last)[..., None]
             + jnp.einsum('bhck,bhcv->bhkv', k_i * jnp.exp(g_last - g_i)[..., None], v_new))
        return s, o

    s_fin, o = lax.scan(step, s0, (qc, kc, u, w, gc, qk))
    o = jnp.transpose(o, (1, 0, 3, 2, 4)).reshape(b, n, h, dv)
    return o, s_fin


def gated_deltanet(qkv, z, a_in, b_in, p, ctx):
    b, n = qkv.shape[:2]
    w = p['c_conv'].astype(qkv.dtype)
    qkv = jax.nn.silu(lax.conv_general_dilated(
        qkv, w, window_strides=(1,), padding=[(CONV_K // 2, CONV_K // 2)],
        dimension_numbers=('NWC', 'WIO', 'NWC'), feature_group_count=CONV_CH))
    q, k, v = split_cols(qkv, (HEADS_C * DK_C, HEADS_C * DK_C, HEADS_C * DV_C))
    q = l2_norm(q.reshape(b, n, HEADS_C, DK_C)) * (DK_C ** -0.5)
    k = l2_norm(k.reshape(b, n, HEADS_C, DK_C))
    v = v.reshape(b, n, HEADS_C, DV_C).astype(jnp.float32)
    a = a_in.reshape(b, n, 2, HEADS_C).astype(jnp.float32)
    g = -jnp.exp(p['c_alog'].astype(jnp.float32)) * jax.nn.softplus(a + p['c_dt_bias'].astype(jnp.float32))
    beta = jax.nn.sigmoid(b_in.reshape(b, n, 2, HEADS_C).astype(jnp.float32))
    if ctx is None:
        s0_f = jnp.zeros((b, HEADS_C, DK_C, DV_C), jnp.float32)
        s0_b = s0_f
    else:
        s0_f, s0_b = ctx[0].astype(jnp.float32), ctx[1].astype(jnp.float32)
    o_f, s_f = chunk_gated_delta(q, k, v, g[:, :, 0], beta[:, :, 0], s0_f)
    flip = lambda t: jnp.flip(t, axis=1)
    o_b, s_b = chunk_gated_delta(flip(q), flip(k), flip(v), flip(g[:, :, 1]), flip(beta[:, :, 1]), s0_b)
    o = o_f + flip(o_b)
    o = rms_norm(o, p['c_onorm']).astype(z.dtype) * jax.nn.silu(z.reshape(b, n, HEADS_C, DV_C))
    return o.reshape(b, n, HEADS_C * DV_C), (s_f, s_b)


def mixer_block(h, p, ctx, angs):
    (qa, ka, va, cq, ckv, kpe, qkv, z, a_in, b_in, gates) = split_cols(h @ p['w_in'], SPLIT_SIZES)
    ctx_a = None if ctx is None else ctx[0:2]
    ctx_b = None if ctx is None else ctx[2:4]
    ctx_c = None if ctx is None else ctx[4:6]
    ya, new_a = attn_gqa(qa, ka, va, p, ctx_a, None if angs is None else angs['a'])
    yb, new_b = attn_mla(cq, ckv, kpe, p, ctx_b, None if angs is None else angs['b'])
    yc, new_c = gated_deltanet(qkv, z, a_in, b_in, p, ctx_c)
    g_a, g_b, g_c = jnp.split(jax.nn.sigmoid(gates), 3, axis=-1)
    merged = g_a * (ya @ p['w_pa']) + g_b * (yb @ p['w_pb']) + g_c * (yc @ p['w_pc'])
    new_ctx = (new_a + new_b + new_c) if ctx is None else None
    return merged @ p['w_out'], new_ctx


def trunk_layer(x, cvec, p, ctx, angs):
    mod = (jax.nn.silu(cvec) @ p['w_mod'] + p['b_mod'])[:, None, :]
    sh1, sc1, g1, sh2, sc2, g2 = jnp.split(mod, N_MOD, axis=-1)
    h = rms_norm(x, p['norm1']) * (1 + sc1) + sh1
    mix, new_ctx = mixer_block(h, p, ctx, angs)
    x = x + g1 * mix
    h = rms_norm(x, p['norm2']) * (1 + sc2) + sh2
    x = x + g2 * ((jax.nn.silu(h @ p['w_gate']) * (h @ p['w_up'])) @ p['w_down'])
    return x, new_ctx


def setup_inputs(seed: int = 0) -> dict:
    key = jax.random.key(seed)
    ks = iter(jax.random.split(key, 48))

    def nrm(shape, scale=1.0):
        return jax.random.normal(next(ks), shape, jnp.float32) * scale

    def gain(shape):
        return 1.0 + 0.05 * jax.random.normal(next(ks), shape, jnp.float32)

    L, D = DEPTH, D_MODEL
    dt = jnp.exp(jax.random.uniform(next(ks), (L, 2, HEADS_C), jnp.float32, np.log(1e-3), np.log(1e-1)))
    return {
        'x_prompt': nrm((BATCH, SEQ, D)),
        'x_sample': nrm((DEC_BATCH, DEC_SEQ, D)),
        'cache_ka': nrm((DEC_BATCH, L, PAST_LEN, KV_HEADS_A, HD_A)),
        'cache_va': nrm((DEC_BATCH, L, PAST_LEN, KV_HEADS_A, HD_A)),
        'cache_ckv': nrm((DEC_BATCH, L, PAST_LEN, KV_LORA)),
        'cache_kpe': nrm((DEC_BATCH, L, PAST_LEN, ROPE_B)),
        'state_fwd': nrm((DEC_BATCH, L, HEADS_C, DK_C, DV_C), 0.1),
        'state_bwd': nrm((DEC_BATCH, L, HEADS_C, DK_C, DV_C), 0.1),
        'c': nrm((DEC_BATCH, D)),
        'c_ctx': nrm((D,)),
        'w_mod': nrm((L, D, N_MOD * D), D ** -0.5),
        'b_mod': nrm((L, N_MOD * D), 0.02),
        'norm1': gain((L, D)),
        'norm2': gain((L, D)),
        'w_in': nrm((L, D, IN_COLS), D ** -0.5),
        'a_qnorm': gain((L, HD_A)),
        'a_knorm': gain((L, HD_A)),
        'b_qnorm': gain((L, Q_LORA)),
        'b_kvnorm': gain((L, KV_LORA)),
        'w_uq': nrm((L, Q_LORA, HEADS_B * (NOPE_B + ROPE_B)), Q_LORA ** -0.5),
        'w_ukv': nrm((L, KV_LORA, HEADS_B * (NOPE_B + VD_B)), KV_LORA ** -0.5),
        'c_conv': nrm((L, CONV_K, 1, CONV_CH), CONV_K ** -0.5),
        'c_alog': jnp.log(jax.random.uniform(next(ks), (L, 2, HEADS_C), jnp.float32, 1.0, 16.0)),
        'c_dt_bias': jnp.log(jnp.expm1(dt)),
        'c_onorm': gain((L, DV_C)),
        'w_pa': nrm((L, HEADS_A * HD_A, D), (HEADS_A * HD_A) ** -0.5),
        'w_pb': nrm((L, HEADS_B * VD_B, D), (HEADS_B * VD_B) ** -0.5),
        'w_pc': nrm((L, HEADS_C * DV_C, D), (HEADS_C * DV_C) ** -0.5),
        'w_out': nrm((L, D, D), D ** -0.5),
        'w_gate': nrm((L, D, D_FF), D ** -0.5),
        'w_up': nrm((L, D, D_FF), D ** -0.5),
        'w_down': nrm((L, D_FF, D), D_FF ** -0.5),
        'final_norm': gain((D,)),
    }


def reference(x_prompt, x_sample, cache_ka, cache_va, cache_ckv, cache_kpe, state_fwd, state_bwd,
              c, c_ctx, w_mod, b_mod, norm1, norm2, w_in, a_qnorm, a_knorm, b_qnorm, b_kvnorm,
              w_uq, w_ukv, c_conv, c_alog, c_dt_bias, c_onorm, w_pa, w_pb, w_pc, w_out,
              w_gate, w_up, w_down, final_norm):
    def layer_params(l):
        return {'w_mod': w_mod[l], 'b_mod': b_mod[l], 'norm1': norm1[l], 'norm2': norm2[l],
                'w_in': w_in[l], 'a_qnorm': a_qnorm[l], 'a_knorm': a_knorm[l],
                'b_qnorm': b_qnorm[l], 'b_kvnorm': b_kvnorm[l], 'w_uq': w_uq[l], 'w_ukv': w_ukv[l],
                'c_conv': c_conv[l], 'c_alog': c_alog[l], 'c_dt_bias': c_dt_bias[l], 'c_onorm': c_onorm[l],
                'w_pa': w_pa[l], 'w_pb': w_pb[l], 'w_pc': w_pc[l], 'w_out': w_out[l],
                'w_gate': w_gate[l], 'w_up': w_up[l], 'w_down': w_down[l]}

    x = x_prompt
    outs = []
    for l in range(DEPTH):
        x, ctx_l = trunk_layer(x, c_ctx[None, :], layer_params(l), None, None)
        outs.append(ctx_l)
    y_prompt = rms_norm(x, final_norm)
    new_ka, new_va, new_ckv, new_kpe, new_sf, new_sb = [
        jnp.stack([o[i] for o in outs], axis=1) for i in range(6)]

    n_lat = x_sample.shape[1]
    angs = {'a': axial_angles(n_lat, HD_A), 'b': axial_angles(n_lat, ROPE_B)}
    x = x_sample
    for l in range(DEPTH):
        ctx = (cache_ka[:, l], cache_va[:, l], cache_ckv[:, l], cache_kpe[:, l],
               state_fwd[:, l], state_bwd[:, l])
        x, _ = trunk_layer(x, c, layer_params(l), ctx, angs)
    y_sample = rms_norm(x, final_norm)

    return (y_prompt, y_sample, new_ka, new_va, new_ckv, new_kpe, new_sf, new_sb)
```

```python
import functools

import numpy as np
import jax
import jax.numpy as jnp
from jax import lax
from jax.experimental import pallas as pl
from jax.experimental.pallas import tpu as pltpu

F32 = jnp.float32
BF16 = jnp.bfloat16

D_MODEL = 1024
GRID_W = 64
ROPE_THETA = 10000.0
EPS = 1e-6
N_MOD = 6
HEADS = 8
KV_HEADS_A = 2
HD = 64
Q_LORA = 384
KV_LORA = 256
NOPE_B = 64
ROPE_B = 32
CHUNK = 64
CONV_CH = 3 * HEADS * HD
D_FF = 2816

C_GATES = 0
C_QKV = 3072
C_B = 4608
C_KA = 5376
C_VA = 5504
C_Z = 5632
C_QA = 6144
N_IN = 6656

LANES = 128
TOK = 256
VMEM_LIMIT = 56 * 1024 * 1024


def _cparams(sem):
    return pltpu.CompilerParams(dimension_semantics=sem, vmem_limit_bytes=VMEM_LIMIT)


def _dot(a, b):
    return jnp.dot(a, b, preferred_element_type=F32)


def _dot_nt(a, b):
    return lax.dot_general(a, b, (((1,), (1,)), ((), ())), preferred_element_type=F32)


def _dot_tn(a, b):
    return lax.dot_general(a, b, (((0,), (0,)), ((), ())), preferred_element_type=F32)


def _split3(x):
    hi = x.astype(BF16)
    r = x - hi.astype(F32)
    mid = r.astype(BF16)
    lo = (r - mid.astype(F32)).astype(BF16)
    return hi, mid, lo


def _dot3(x, w):
    hi, mid, lo = _split3(x)
    return _dot(hi, w) + _dot(mid, w) + _dot(lo, w)


def _dot3_rhs(w, x):
    hi, mid, lo = _split3(x)
    return _dot(w, hi) + _dot(w, mid) + _dot(w, lo)


def _silu(x):
    return x * jax.nn.sigmoid(x)


def _rms_rows(x, g):
    return x * lax.rsqrt(jnp.mean(x * x, axis=-1, keepdims=True) + EPS) * g


def _swap_pairs(x, half):
    lane = lax.broadcasted_iota(jnp.int32, x.shape, 1)
    n = x.shape[1]
    return jnp.where((lane & half) == 0, pltpu.roll(x, n - half, 1), pltpu.roll(x, half, 1))


def _rope_cols(x, cos, sin, half):
    outs = []
    for s in range(x.shape[1] // LANES):
        xs = x[:, s * LANES:(s + 1) * LANES]
        outs.append(xs * cos + _swap_pairs(xs, half) * sin)
    return outs[0] if len(outs) == 1 else jnp.concatenate(outs, axis=1)


def _mod_kernel(c_ref, w_ref, b_ref, o_ref):
    s = _silu(c_ref[...]).astype(BF16)
    o_ref[0] = _dot(s, w_ref[0].astype(BF16)) + b_ref[0]


def _modulation(cvec, w_mod, b_mod):
    n_layers = w_mod.shape[0]
    n_out = w_mod.shape[2]
    tn = 1536
    return pl.pallas_call(
        _mod_kernel,
        grid=(n_layers, n_out // tn),
        in_specs=[pl.BlockSpec((8, D_MODEL), lambda l, j: (0, 0)),
                  pl.BlockSpec((1, D_MODEL, tn), lambda l, j: (l, 0, j)),
                  pl.BlockSpec((1, 1, tn), lambda l, j: (l, 0, j))],
        out_specs=pl.BlockSpec((1, 8, tn), lambda l, j: (l, 0, j)),
        out_shape=jax.ShapeDtypeStruct((n_layers, 8, n_out), F32),
        compiler_params=_cparams(("parallel", "parallel")),
        name="modulation",
    )(cvec, w_mod, b_mod.reshape(n_layers, 1, n_out))


def _in_proj_kernel(x_ref, mod_ref, g_ref, w_ref, o_ref, h_ref):
    @pl.when(pl.program_id(1) == 0)
    def _():
        m = mod_ref[0, 0]
        y = _rms_rows(x_ref[...], g_ref[...])
        h_ref[...] = (y * (1.0 + m[1:2]) + m[0:1]).astype(BF16)

    o_ref[...] = _dot(h_ref[...], w_ref[...])


def _in_proj(x, mod, layer, gain, w, mod_row, tm):
    m_rows = x.shape[0]
    tn = N_IN // 4
    return pl.pallas_call(
        _in_proj_kernel,
        grid=(m_rows // tm, N_IN // tn),
        in_specs=[pl.BlockSpec((tm, D_MODEL), lambda i, j: (i, 0)),
                  pl.BlockSpec((1, 1, N_MOD, D_MODEL), lambda i, j: (layer, mod_row(i, tm), 0, 0)),
                  pl.BlockSpec((1, D_MODEL), lambda i, j: (0, 0)),
                  pl.BlockSpec((D_MODEL, tn), lambda i, j: (0, j))],
        out_specs=pl.BlockSpec((tm, tn), lambda i, j: (i, j)),
        out_shape=jax.ShapeDtypeStruct((m_rows, N_IN), F32),
        scratch_shapes=[pltpu.VMEM((tm, D_MODEL), BF16)],
        compiler_params=_cparams(("parallel", "arbitrary")),
        name="in_proj",
    )(x, mod, gain, w)


def _rope_table_kernel(ang_ref, sign_ref, cos_ref, sin_ref):
    a = ang_ref[...]
    cos_ref[...] = jnp.cos(a)
    sin_ref[...] = jnp.sin(a) * sign_ref[...]


def _rope_tables(n_lat):
    t = np.arange(n_lat)
    row = jnp.asarray((t // GRID_W).astype(np.float32))[:, None]
    col = jnp.asarray((t % GRID_W).astype(np.float32))[:, None]
    lane = np.arange(LANES)

    def table(nf, lo, width):
        inv_freq = ROPE_THETA ** (-jnp.arange(nf, dtype=F32) / nf)
        d = lane - lo
        active = (d >= 0) & (d < width)
        dd = np.where(active, d, 0)
        f_idx = dd % nf
        use_col = dd >= 2 * nf
        freq = jnp.where(jnp.asarray(active), inv_freq[f_idx], 0.0)[None, :]
        ang = jnp.where(jnp.asarray(use_col)[None, :], col, row) * freq
        sign = np.where((dd % (2 * nf)) < nf, -1.0, 1.0).astype(np.float32)
        return ang, sign

    period_a = HD
    lane_a = lane % period_a
    nf_a = HD // 4
    inv_a = ROPE_THETA ** (-jnp.arange(nf_a, dtype=F32) / nf_a)
    ang_a = jnp.where(jnp.asarray(lane_a >= 2 * nf_a)[None, :], col, row) * inv_a[lane_a % nf_a][None, :]
    sign_a = np.where((lane_a % (2 * nf_a)) < nf_a, -1.0, 1.0).astype(np.float32)
    ang_q, sign_q = table(ROPE_B // 4, NOPE_B, ROPE_B)
    ang_k, sign_k = table(ROPE_B // 4, 0, ROPE_B)
    ang = jnp.stack([ang_a, ang_q, ang_k])
    ang = jnp.concatenate([jnp.zeros((3, TOK, LANES), F32), ang], axis=1)
    sign = jnp.asarray(np.stack([sign_a, sign_q, sign_k]))[:, None, :]
    rows = TOK + n_lat
    spec = pl.BlockSpec((1, TOK, LANES), lambda a, i: (a, i, 0))
    return pl.pallas_call(
        _rope_table_kernel,
        grid=(3, rows // TOK),
        in_specs=[spec, pl.BlockSpec((1, 1, LANES), lambda a, i: (a, 0, 0))],
        out_specs=[spec, spec],
        out_shape=[jax.ShapeDtypeStruct((3, rows, LANES), F32)] * 2,
        compiler_params=_cparams(("parallel", "parallel")),
        name="rope_tables",
    )(ang, sign)


def _prep_a_kernel(qa_ref, kv_ref, cos_ref, sin_ref, gq_ref, gk_ref, ones_ref, place_ref,
                   q_ref, k4_ref, v4_ref, kn_ref):
    cos = cos_ref[0]
    sin = sin_ref[0]
    ones = ones_ref[...]
    q = qa_ref[...]
    qn = q * lax.rsqrt(_dot3(q * q, ones) * (1.0 / HD) + EPS) * gq_ref[...]
    q_ref[...] = (_rope_cols(qn, cos, sin, HD // 4) * (HD ** -0.5)).astype(BF16)
    k = kv_ref[:, :LANES]
    kn = k * lax.rsqrt(_dot3(k * k, ones[:LANES, :LANES]) * (1.0 / HD) + EPS) * gk_ref[...]
    kn_ref[...] = kn
    place = place_ref[...]
    k4_ref[...] = _dot(_rope_cols(kn, cos, sin, HD // 4).astype(BF16), place).astype(BF16)
    v4_ref[...] = _dot(kv_ref[:, LANES:].astype(BF16), place).astype(BF16)


def _place4():
    p = np.zeros((LANES, 4 * LANES), np.float32)
    for kv in range(KV_HEADS_A):
        for side in range(2):
            for d in range(HD):
                p[kv * HD + d, (2 * kv + side) * LANES + side * HD + d] = 1.0
    return jnp.asarray(p, BF16)


def _ones_blockdiag(n, group):
    idx = np.arange(n) // group
    return jnp.asarray((idx[:, None] == idx[None, :]).astype(np.float32), BF16)


def _prep_a(y, cos, sin, gq, gk, tab_row):
    m_rows = y.shape[0]
    wide = pl.BlockSpec((TOK, 4 * LANES), lambda i: (i, 0))
    tab = pl.BlockSpec((1, TOK, LANES), lambda i: (0, tab_row(i), 0))
    return pl.pallas_call(
        _prep_a_kernel,
        grid=(m_rows // TOK,),
        in_specs=[pl.BlockSpec((TOK, 512), lambda i: (i, C_QA // 512)),
                  pl.BlockSpec((TOK, 256), lambda i: (i, C_KA // 256)),
                  tab, tab,
                  pl.BlockSpec((1, 512), lambda i: (0, 0)),
                  pl.BlockSpec((1, LANES), lambda i: (0, 0)),
                  pl.BlockSpec((512, 512), lambda i: (0, 0)),
                  pl.BlockSpec((LANES, 512), lambda i: (0, 0))],
        out_specs=[wide, wide, wide, pl.BlockSpec((TOK, LANES), lambda i: (i, 0))],
        out_shape=[jax.ShapeDtypeStruct((m_rows, 512), BF16)] * 3
        + [jax.ShapeDtypeStruct((m_rows, LANES), F32)],
        compiler_params=_cparams(("parallel",)),
        name="prep_a",
    )(y, y, cos, sin, gq, gk, _ones_blockdiag(512, HD), _place4())


def _cache_a_kernel(k_ref, v_ref, place_ref, k4_ref, v4_ref):
    place = place_ref[...]
    k4_ref[...] = _dot(k_ref[...].astype(BF16), place).astype(BF16)
    v4_ref[...] = _dot(v_ref[...].astype(BF16), place).astype(BF16)


def _cache_a(ck, cv):
    rows = ck.shape[0]
    narrow = pl.BlockSpec((TOK, LANES), lambda i: (i, 0))
    wide = pl.BlockSpec((TOK, 512), lambda i: (i, 0))
    return pl.pallas_call(
        _cache_a_kernel,
        grid=(rows // TOK,),
        in_specs=[narrow, narrow, pl.BlockSpec((LANES, 512), lambda i: (0, 0))],
        out_specs=[wide, wide],
        out_shape=[jax.ShapeDtypeStruct((rows, 512), BF16)] * 2,
        compiler_params=_cparams(("parallel",)),
        name="cache_a",
    )(ck, cv, _place4())


def _prep_b_kernel(g_ref, cq_ref, sq_ref, ck_ref, sk_ref, gq_ref, gkv_ref, wq_ref, wk_ref, wv_ref,
                   q_ref, k_ref, v_ref, ckv_ref):
    g = g_ref[...]
    cqn = _rms_rows(g[:, :Q_LORA], gq_ref[...])
    q = _dot(cqn.astype(BF16), wq_ref[0])
    q = _rope_cols(q, cq_ref[0], sq_ref[0], ROPE_B // 4) * ((NOPE_B + ROPE_B) ** -0.5)
    q_ref[...] = q.astype(BF16)
    ckvn = _rms_rows(g[:, Q_LORA:Q_LORA + KV_LORA], gkv_ref[...])
    ckv_ref[...] = ckvn
    kpe = _rope_cols(g[:, Q_LORA + KV_LORA:], ck_ref[0], sk_ref[0], ROPE_B // 4)
    ckv_b = ckvn.astype(BF16)
    kin = jnp.concatenate([ckv_b, kpe.astype(BF16)], axis=1)
    k_ref[...] = _dot(kin, wk_ref[0]).astype(BF16)
    v_ref[...] = _dot(ckv_b, wv_ref[0]).astype(BF16)


def _prep_b(y, cos, sin, gq, gkv, wq, wk, wv, layer, tab_row):
    m_rows = y.shape[0]
    wide = pl.BlockSpec((TOK, HEADS * LANES), lambda i: (i, 0))

    def tab(a):
        return pl.BlockSpec((1, TOK, LANES), lambda i: (a, tab_row(i), 0))

    def wspec(k):
        return pl.BlockSpec((1, k, HEADS * LANES), lambda i: (layer, 0, 0))

    return pl.pallas_call(
        _prep_b_kernel,
        grid=(m_rows // TOK,),
        in_specs=[pl.BlockSpec((TOK, 768), lambda i: (i, C_B // 768)),
                  tab(1), tab(1), tab(2), tab(2),
                  pl.BlockSpec((1, Q_LORA), lambda i: (0, 0)),
                  pl.BlockSpec((1, KV_LORA), lambda i: (0, 0)),
                  wspec(Q_LORA), wspec(KV_LORA + LANES), wspec(KV_LORA)],
        out_specs=[wide, wide, wide, pl.BlockSpec((TOK, KV_LORA), lambda i: (i, 0))],
        out_shape=[jax.ShapeDtypeStruct((m_rows, HEADS * LANES), BF16)] * 3
        + [jax.ShapeDtypeStruct((m_rows, KV_LORA), F32)],
        compiler_params=_cparams(("parallel",)),
        name="prep_b",
    )(y, cos, sin, cos, sin, gq, gkv, wq, wk, wv)


def _cache_b_kernel(c_ref, p_ref, wk_ref, wv_ref, k_ref, v_ref):
    ckv_b = c_ref[...].astype(BF16)
    kin = jnp.concatenate([ckv_b, p_ref[...].astype(BF16)], axis=1)
    k_ref[...] = _dot(kin, wk_ref[0]).astype(BF16)
    v_ref[...] = _dot(ckv_b, wv_ref[0]).astype(BF16)


def _cache_b(ckv, kpe, wk, wv, layer_of_block):
    rows = ckv.shape[0]
    wide = pl.BlockSpec((TOK, HEADS * LANES), lambda i: (i, 0))
    return pl.pallas_call(
        _cache_b_kernel,
        grid=(rows // TOK,),
        in_specs=[pl.BlockSpec((TOK, KV_LORA), lambda i: (i, 0)),
                  pl.BlockSpec((TOK, LANES), lambda i: (i, 0)),
                  pl.BlockSpec((1, KV_LORA + LANES, HEADS * LANES), lambda i: (layer_of_block(i), 0, 0)),
                  pl.BlockSpec((1, KV_LORA, HEADS * LANES), lambda i: (layer_of_block(i), 0, 0))],
        out_specs=[wide, wide],
        out_shape=[jax.ShapeDtypeStruct((rows, HEADS * LANES), BF16)] * 2,
        compiler_params=_cparams(("parallel",)),
        name="cache_b",
    )(ckv, kpe, wk, wv)


def _attn_kernel(*refs, groups, has_cache):
    if has_cache:
        q_ref, kc_ref, vc_ref, kn_ref, vn_ref, o_ref = refs
    else:
        q_ref, kn_ref, vn_ref, o_ref = refs
    for pair in range(HEADS // 2):
        acc = None
        for side in range(2):
            qg, kg = groups[2 * pair + side]
            q = q_ref[:, qg * LANES:(qg + 1) * LANES]
            ksl = slice(kg * LANES, (kg + 1) * LANES)
            s_n = _dot_nt(q, kn_ref[:, ksl])
            m = jnp.max(s_n, axis=-1, keepdims=True)
            if has_cache:
                s_c = _dot_nt(q, kc_ref[:, ksl])
                m = jnp.maximum(m, jnp.max(s_c, axis=-1, keepdims=True))
            p_n = jnp.exp(s_n - m)
            den = jnp.sum(p_n, axis=-1, keepdims=True)
            o = _dot(p_n.astype(BF16), vn_ref[:, ksl])
            if has_cache:
                p_c = jnp.exp(s_c - m)
                den = den + jnp.sum(p_c, axis=-1, keepdims=True)
                o = o + _dot(p_c.astype(BF16), vc_ref[:, ksl])
            o = o / den
            acc = o if acc is None else acc + o
        o_ref[:, pair * LANES:(pair + 1) * LANES] = acc.astype(BF16)


def _attention(q, kn, vn, cache, groups, row0, n_seq, seq, tq, cache_block):
    m_rows = q.shape[0]
    qw, kw = q.shape[1], kn.shape[1]
    nq = seq // tq
    qb0, sb0 = row0 // tq, row0 // seq
    in_specs = [pl.BlockSpec((tq, qw), lambda b, i: (qb0 + b * nq + i, 0))]
    args = [q]
    if cache is not None:
        past = cache_block[0]
        spec = pl.BlockSpec((past, kw), lambda b, i: (cache_block[1](b), 0))
        in_specs += [spec, spec]
        args += list(cache)
    kv_spec = pl.BlockSpec((seq, kw), lambda b, i: (sb0 + b, 0))
    in_specs += [kv_spec, kv_spec]
    args += [kn, vn]
    return pl.pallas_call(
        functools.partial(_attn_kernel, groups=groups, has_cache=cache is not None),
        grid=(n_seq, nq),
        in_specs=in_specs,
        out_specs=pl.BlockSpec((tq, 512), lambda b, i: (b * nq + i, 0)),
        out_shape=jax.ShapeDtypeStruct((n_seq * seq, 512), BF16),
        compiler_params=_cparams(("parallel", "parallel")),
        name="attention",
    )(*args)


GROUPS_A = tuple((h // 2, 2 * (h // 4) + (h % 2)) for h in range(HEADS))
GROUPS_B = tuple((h, h) for h in range(HEADS))


def _prep_c_kernel(x_ref, prev_ref, next_ref, ab_ref, w_ref, alog_ref, dtb_ref, ones_ref, exp_ref,
                   q_ref, k_ref, v_ref, gf_ref, gb_ref, bf_ref, bb_ref, *, first_last):
    i = pl.program_id(0)
    is_first, is_last = first_last(i)
    x = x_ref[...]
    rows = x.shape[0]
    rid = lax.broadcasted_iota(jnp.int32, x.shape, 0)
    halo_p = jnp.where(is_first, 0.0, prev_ref[7:8, :])
    halo_n = jnp.where(is_last, 0.0, next_ref[0:1, :])
    x_prev = jnp.where(rid == 0, halo_p, pltpu.roll(x, 1, 0))
    x_next = jnp.where(rid == rows - 1, halo_n, pltpu.roll(x, rows - 1, 0))
    w = w_ref[...]
    s = _silu(x_prev * w[0:1] + x * w[1:2] + x_next * w[2:3])
    ones = ones_ref[...]
    hw = HEADS * HD
    q, k = s[:, :hw], s[:, hw:2 * hw]
    q_ref[...] = q * lax.rsqrt(_dot3(q * q, ones) + EPS) * (HD ** -0.5)
    k_ref[...] = k * lax.rsqrt(_dot3(k * k, ones) + EPS)
    v_ref[...] = s[:, 2 * hw:]
    ab = ab_ref[...]
    t = ab + dtb_ref[...]
    softplus = jnp.maximum(t, 0.0) + jnp.log1p(jnp.exp(-jnp.abs(t)))
    g = -jnp.exp(alog_ref[...]) * softplus
    lane = lax.broadcasted_iota(jnp.int32, ab.shape, 1)
    both = jnp.where(lane < 48, g, jax.nn.sigmoid(ab))
    e = _dot3(both, exp_ref[...])
    gf_ref[...] = e[:, 0 * hw:1 * hw]
    gb_ref[...] = e[:, 1 * hw:2 * hw]
    bf_ref[...] = e[:, 2 * hw:3 * hw]
    bb_ref[...] = e[:, 3 * hw:4 * hw]


def _expand_matrix():
    e = np.zeros((LANES, 4 * HEADS * HD), np.float32)
    for kind in range(2):
        for d in range(2):
            for h in range(HEADS):
                src = 32 + 16 * kind + 8 * d + h
                dst = (2 * kind + d) * HEADS * HD + h * HD
                e[src, dst:dst + HD] = 1.0
    return jnp.asarray(e, BF16)


def _prep_c(y, w_conv, alog, dtb, first_last):
    m_rows = y.shape[0]
    nb8 = TOK // 8
    last8 = m_rows // 8 - 1
    qkv_blk = C_QKV // CONV_CH
    out = pl.BlockSpec((TOK, HEADS * HD), lambda i: (i, 0))
    vec = pl.BlockSpec((1, LANES), lambda i: (0, 0))
    return pl.pallas_call(
        functools.partial(_prep_c_kernel, first_last=first_last),
        grid=(m_rows // TOK,),
        in_specs=[pl.BlockSpec((TOK, CONV_CH), lambda i: (i, qkv_blk)),
                  pl.BlockSpec((8, CONV_CH), lambda i: (jnp.maximum(i * nb8 - 1, 0), qkv_blk)),
                  pl.BlockSpec((8, CONV_CH), lambda i: (jnp.minimum((i + 1) * nb8, last8), qkv_blk)),
                  pl.BlockSpec((TOK, LANES), lambda i: (i, (C_B + 640) // LANES)),
                  pl.BlockSpec((3, CONV_CH), lambda i: (0, 0)),
                  vec, vec,
                  pl.BlockSpec((512, 512), lambda i: (0, 0)),
                  pl.BlockSpec((LANES, 4 * HEADS * HD), lambda i: (0, 0))],
        out_specs=[out] * 7,
        out_shape=[jax.ShapeDtypeStruct((m_rows, HEADS * HD), F32)] * 7,
        compiler_params=_cparams(("parallel",)),
        name="prep_c",
    )(y, y, y, y, w_conv, alog, dtb, _ones_blockdiag(512, HD), _expand_matrix())


def _gdn_kernel(*refs, has_s0, n_chunks):
    ins = refs[:11]
    tri_ref = ins[10]
    pos = 11
    if has_s0:
        s0_refs = refs[pos:pos + 2]
        pos += 2
    o_refs = refs[pos:pos + 2]
    s_out_refs = refs[pos + 2:pos + 4]
    s_scr = refs[pos + 4]
    c = pl.program_id(1)

    @pl.when(c == 0)
    def _():
        for d in range(2):
            s_scr[d] = s0_refs[d][0] if has_s0 else jnp.zeros(s_scr.shape[1:], F32)

    n2 = 2 * CHUNK
    ri = lax.broadcasted_iota(jnp.int32, (n2, n2), 0)
    ci = lax.broadcasted_iota(jnp.int32, (n2, n2), 1)
    same = (ri // CHUNK) == (ci // CHUNK)
    left = lax.broadcasted_iota(jnp.int32, (CHUNK, n2), 1) < CHUNK
    eye = (ri == ci).astype(F32)

    def level_mask(lvl, lower):
        bit = 1 << lvl
        hi, lo = (ri, ci) if lower else (ci, ri)
        return ((ri >> (lvl + 1)) == (ci >> (lvl + 1))) & ((hi & bit) != 0) & ((lo & bit) == 0)

    levels = [[level_mask(lvl, d == 0) for lvl in range(6)] for d in range(2)]

    def block_diag(x):
        return jnp.concatenate([jnp.where(left, x, 0.0), jnp.where(left, 0.0, x)], axis=0)

    def anti_diag(x):
        xr = pltpu.roll(x, CHUNK, 1)
        return jnp.concatenate([jnp.where(left, 0.0, xr), jnp.where(left, xr, 0.0)], axis=0)

    for d in range(2):
        q_ref, k_ref, v_ref, g_ref, b_ref = ins[5 * d:5 * d + 5]
        tri = tri_ref[d]
        incl = same & ((ri >= ci) if d == 0 else (ri <= ci))
        strict = same & ((ri > ci) if d == 0 else (ri < ci))
        for p in range(HEADS // 2):
            sl = slice(p * n2, (p + 1) * n2)
            q, k, v, g, beta = q_ref[:, sl], k_ref[:, sl], v_ref[:, sl], g_ref[:, sl], b_ref[:, sl]
            gc = _dot3_rhs(tri, g)
            gtot = gc[CHUNK - 1:CHUNK, :] if d == 0 else gc[0:1, :]
            eg = jnp.exp(gc)
            kb = k * beta
            gcr = pltpu.roll(gc, CHUNK, 1)
            gcol = jnp.concatenate([jnp.where(left, gc, gcr), jnp.where(left, gcr, gc)], axis=0)
            diff = gcol - gcol.T
            decay = jnp.where(incl, jnp.exp(jnp.where(incl, diff, 0.0)), 0.0)
            k_bd = block_diag(k).astype(BF16)
            a_mat = jnp.where(strict, _dot_nt(block_diag(kb).astype(BF16), k_bd) * decay, 0.0)
            qk = (_dot_nt(block_diag(q).astype(BF16), k_bd) * decay).astype(BF16)
            vb = v * beta
            kbe_r = pltpu.roll(kb * eg, CHUNK, 1)
            x = jnp.concatenate([jnp.where(left, vb, kbe_r), jnp.where(left, kbe_r, vb)], axis=0)
            corr = -jnp.where(levels[d][0], a_mat, 0.0)
            for lvl in range(1, 6):
                d_b = (eye + corr).astype(BF16)
                l_b = jnp.where(levels[d][lvl], a_mat, 0.0).astype(BF16)
                corr = corr - _dot(d_b, _dot(l_b, d_b).astype(BF16))
            x = x + _dot(corr.astype(BF16), x.astype(BF16))
            s = s_scr[d, p]
            s_b = s.astype(BF16)
            v_new = jnp.where(same, x, 0.0) - _dot(jnp.where(same, 0.0, x).astype(BF16), s_b)
            v_new_b = v_new.astype(BF16)
            o = _dot(anti_diag(q * eg).astype(BF16), s_b) + _dot(qk, v_new_b)
            o_refs[d][:, sl] = o[:CHUNK] + o[CHUNK:]
            kd = anti_diag(k * jnp.exp(gtot - gc)).astype(BF16)
            s_scr[d, p] = s * jnp.exp(gtot) + _dot_tn(kd, v_new_b)

    @pl.when(c == n_chunks - 1)
    def _():
        for d in range(2):
            s_out_refs[d][0] = s_scr[d]


def _tri_ones():
    r = np.arange(CHUNK)
    low = (r[:, None] >= r[None, :]).astype(np.float32)
    return jnp.asarray(np.stack([low, low.T]), BF16)


def _gdn_scan(qn, kn, v, gf, gb, bf, bb, s0, row0, n_seq, seq):
    nc = seq // CHUNK
    cb0 = row0 // CHUNK
    hw = HEADS * HD
    n2 = 2 * CHUNK
    fwd = pl.BlockSpec((CHUNK, hw), lambda b, c: (cb0 + b * nc + c, 0))
    bwd = pl.BlockSpec((CHUNK, hw), lambda b, c: (cb0 + b * nc + nc - 1 - c, 0))
    o_fwd = pl.BlockSpec((CHUNK, hw), lambda b, c: (b * nc + c, 0))
    o_bwd = pl.BlockSpec((CHUNK, hw), lambda b, c: (b * nc + nc - 1 - c, 0))
    state = pl.BlockSpec((1, HEADS // 2, n2, n2), lambda b, c: (b, 0, 0, 0))
    in_specs = [fwd] * 5 + [bwd] * 5 + [pl.BlockSpec((2, CHUNK, CHUNK), lambda b, c: (0, 0, 0))]
    args = [qn, kn, v, gf, bf, qn, kn, v, gb, bb, _tri_ones()]
    if s0 is not None:
        in_specs += [state, state]
        args += list(s0)
    st_shape = jax.ShapeDtypeStruct((n_seq, HEADS // 2, n2, n2), F32)
    o_shape = jax.ShapeDtypeStruct((n_seq * seq, hw), F32)
    return pl.pallas_call(
        functools.partial(_gdn_kernel, has_s0=s0 is not None, n_chunks=nc),
        grid=(n_seq, nc),
        in_specs=in_specs,
        out_specs=[o_fwd, o_bwd, state, state],
        out_shape=[o_shape, o_shape, st_shape, st_shape],
        scratch_shapes=[pltpu.VMEM((2, HEADS // 2, n2, n2), F32)],
        compiler_params=_cparams(("parallel", "arbitrary")),
        name="gdn_scan",
    )(*args)


def _state_to_pairs(s):
    b = s.shape[0]
    s = s.reshape(b, HEADS // 2, 2, HD, HD)
    z = jnp.zeros_like(s[:, :, 0])
    top = jnp.concatenate([z, s[:, :, 1]], axis=-1)
    bot = jnp.concatenate([s[:, :, 0], z], axis=-1)
    return jnp.concatenate([top, bot], axis=-2)


def _pairs_to_state(s):
    b = s.shape[0]
    even = s[:, :, HD:, :HD]
    odd = s[:, :, :HD, HD:]
    return jnp.stack([even, odd], axis=2).reshape(b, HEADS, HD, HD)


def _merge_kernel(x_ref, mod_ref, ya_ref, yb_ref, of_ref, ob_ref, z_ref, gates_ref, gain_ref, ones_ref,
                  wa_ref, wb_ref, wc_ref, wo_ref, o_ref):
    o = of_ref[...] + ob_ref[...]
    ms = _dot3(o * o, ones_ref[...]) * (1.0 / HD)
    yc = o * lax.rsqrt(ms + EPS) * gain_ref[...] * _silu(z_ref[...])
    gates = gates_ref[...]
    merged = (jax.nn.sigmoid(gates[:, :D_MODEL]) * _dot(ya_ref[...], wa_ref[0])
              + jax.nn.sigmoid(gates[:, D_MODEL:2 * D_MODEL]) * _dot(yb_ref[...], wb_ref[0])
              + jax.nn.sigmoid(gates[:, 2 * D_MODEL:]) * _dot(yc.astype(BF16), wc_ref[0]))
    mix = _dot(merged.astype(BF16), wo_ref[0])
    o_ref[...] = x_ref[...] + mod_ref[0, 0][2:3] * mix


def _merge(x, mod, layer, ya, yb, o_f, o_b, y, gain, wa, wb, wc, wo, mod_row):
    m_rows = x.shape[0]
    tm = TOK
    half = pl.BlockSpec((tm, 512), lambda i: (i, 0))
    full = pl.BlockSpec((tm, D_MODEL), lambda i: (i, 0))

    def wspec(k):
        return pl.BlockSpec((1, k, D_MODEL), lambda i: (layer, 0, 0))

    return pl.pallas_call(
        _merge_kernel,
        grid=(m_rows // tm,),
        in_specs=[full,
                  pl.BlockSpec((1, 1, N_MOD, D_MODEL), lambda i: (layer, mod_row(i, tm), 0, 0)),
                  half, half, half, half,
                  pl.BlockSpec((tm, 512), lambda i: (i, C_Z // 512)),
                  pl.BlockSpec((tm, 3 * D_MODEL), lambda i: (i, 0)),
                  pl.BlockSpec((1, 512), lambda i: (0, 0)),
                  pl.BlockSpec((512, 512), lambda i: (0, 0)),
                  wspec(512), wspec(512), wspec(512), wspec(D_MODEL)],
        out_specs=full,
        out_shape=jax.ShapeDtypeStruct((m_rows, D_MODEL), F32),
        compiler_params=_cparams(("parallel",)),
        name="merge_out",
    )(x, mod, ya, yb, o_f, o_b, y, y, gain, _ones_blockdiag(512, HD), wa, wb, wc, wo)


def _ffn_kernel(x_ref, mod_ref, g_ref, wg_ref, wu_ref, wd_ref, fin_ref, o_ref, *, final):
    x = x_ref[...]
    m = mod_ref[0, 0]
    h = (_rms_rows(x, g_ref[...]) * (1.0 + m[4:5]) + m[3:4]).astype(BF16)
    act = (_silu(_dot(h, wg_ref[0])) * _dot(h, wu_ref[0])).astype(BF16)
    out = x + m[5:6] * _dot(act, wd_ref[0])
    if final:
        out = _rms_rows(out, fin_ref[...])
    o_ref[...] = out


def _ffn(x, mod, layer, gain, wg, wu, wd, fin, mod_row, final):
    m_rows = x.shape[0]
    tm = TOK
    full = pl.BlockSpec((tm, D_MODEL), lambda i: (i, 0))
    vec = pl.BlockSpec((1, D_MODEL), lambda i: (0, 0))
    once = pl.Buffered(1)
    return pl.pallas_call(
        functools.partial(_ffn_kernel, final=final),
        grid=(m_rows // tm,),
        in_specs=[full,
                  pl.BlockSpec((1, 1, N_MOD, D_MODEL), lambda i: (layer, mod_row(i, tm), 0, 0)),
                  vec,
                  pl.BlockSpec((1, D_MODEL, D_FF), lambda i: (layer, 0, 0), pipeline_mode=once),
                  pl.BlockSpec((1, D_MODEL, D_FF), lambda i: (layer, 0, 0), pipeline_mode=once),
                  pl.BlockSpec((1, D_FF, D_MODEL), lambda i: (layer, 0, 0), pipeline_mode=once),
                  vec],
        out_specs=full,
        out_shape=jax.ShapeDtypeStruct((m_rows, D_MODEL), F32),
        compiler_params=_cparams(("parallel",)),
        name="ffn",
    )(x, mod, gain, wg, wu, wd, fin)


def _layout_w_in(w_in):
    def cols(a, b):
        return w_in[:, :, a:b]

    pad = jnp.zeros(w_in.shape[:2] + (64,), w_in.dtype)
    parts = [cols(3520, 6592), cols(1440, 2976), cols(768, 1152), cols(1152, 1408), cols(1408, 1440),
             cols(3488, 3504), cols(3504, 3520), pad, cols(512, 640), cols(640, 768), cols(2976, 3488),
             cols(0, 512)]
    return jnp.concatenate(parts, axis=-1).astype(BF16)


def _layout_w_uq(w_uq):
    n_layers = w_uq.shape[0]
    w = w_uq.reshape(n_layers, Q_LORA, HEADS, NOPE_B + ROPE_B)
    w = jnp.pad(w, ((0, 0), (0, 0), (0, 0), (0, LANES - NOPE_B - ROPE_B)))
    return w.reshape(n_layers, Q_LORA, HEADS * LANES).astype(BF16)


def _layout_w_ukv(w_ukv):
    n_layers = w_ukv.shape[0]
    w = w_ukv.reshape(n_layers, KV_LORA, HEADS, 2, HD)
    zero = jnp.zeros((n_layers, KV_LORA, HEADS, HD), w.dtype)
    wk_top = jnp.concatenate([w[:, :, :, 0], zero], axis=-1).reshape(n_layers, KV_LORA, HEADS * LANES)
    place = np.zeros((LANES, HEADS, LANES), np.float32)
    for d in range(ROPE_B):
        place[d, :, NOPE_B + d] = 1.0
    place = jnp.broadcast_to(jnp.asarray(place.reshape(LANES, HEADS * LANES)), (n_layers, LANES, HEADS * LANES))
    wk = jnp.concatenate([wk_top, place], axis=1).astype(BF16)
    wv = w[:, :, :, 1].reshape(n_layers, KV_LORA, HEADS // 2, 2, HD)
    zero = jnp.zeros((n_layers, KV_LORA, HEADS // 2, HD), w.dtype)
    even = jnp.concatenate([wv[:, :, :, 0], zero], axis=-1)
    odd = jnp.concatenate([zero, wv[:, :, :, 1]], axis=-1)
    wv = jnp.stack([even, odd], axis=3).reshape(n_layers, KV_LORA, HEADS * LANES).astype(BF16)
    return wk, wv


def kernel(x_prompt, x_sample, cache_ka, cache_va, cache_ckv, cache_kpe, state_fwd, state_bwd, c, c_ctx,
           w_mod, b_mod, norm1, norm2, w_in, a_qnorm, a_knorm, b_qnorm, b_kvnorm, w_uq, w_ukv, c_conv,
           c_alog, c_dt_bias, c_onorm, w_pa, w_pb, w_pc, w_out, w_gate, w_up, w_down, final_norm):
    n_ctx, s_ctx, _ = x_prompt.shape
    n_lat, s_lat, _ = x_sample.shape
    n_layers = w_mod.shape[0]
    past = cache_ka.shape[2]
    ctx_rows = n_ctx * s_ctx
    lat_rows = n_lat * s_lat
    assert s_ctx == TOK and s_lat % TOK == 0 and ctx_rows % 512 == 0 and s_lat % 512 == 0
    assert past % TOK == 0 and n_lat + 1 <= 8 and ctx_rows % s_lat == 0

    def mod_row(i, tm):
        return jnp.where(i < ctx_rows // tm, 0, 1 + (i - ctx_rows // tm) // (s_lat // tm))

    def tab_row(i):
        nb = ctx_rows // TOK
        return jnp.where(i < nb, 0, 1 + (i - nb) % (s_lat // TOK))

    def first_last(i):
        nb = ctx_rows // TOK
        j = (i - nb) % (s_lat // TOK)
        is_ctx = i < nb
        return is_ctx | (j == 0), is_ctx | (j == s_lat // TOK - 1)

    x = jnp.concatenate([x_prompt.reshape(ctx_rows, D_MODEL), x_sample.reshape(lat_rows, D_MODEL)], axis=0)
    cvec = jnp.zeros((8, D_MODEL), F32).at[0].set(c_ctx).at[1:1 + n_lat].set(c)
    mod = _modulation(cvec, w_mod, b_mod).reshape(n_layers, 8, N_MOD, D_MODEL)
    cos, sin = _rope_tables(s_lat)

    w_in_p = _layout_w_in(w_in)
    w_uq_p = _layout_w_uq(w_uq)
    w_k_p, w_v_p = _layout_w_ukv(w_ukv)
    bf = lambda w: w.astype(BF16)
    w_pa_b, w_pb_b, w_pc_b, w_out_b = bf(w_pa), bf(w_pb), bf(w_pc), bf(w_out)
    w_gate_b, w_up_b, w_down_b = bf(w_gate), bf(w_up), bf(w_down)

    cache_rows = n_lat * n_layers * past
    ck4, cv4 = _cache_a(cache_ka.reshape(cache_rows, LANES), cache_va.reshape(cache_rows, LANES))
    kpe_pad = jnp.pad(cache_kpe.reshape(cache_rows, ROPE_B), ((0, 0), (0, LANES - ROPE_B)))
    ckb, cvb = _cache_b(cache_ckv.reshape(cache_rows, KV_LORA), kpe_pad, w_k_p, w_v_p,
                        lambda i: (i // (past // TOK)) % n_layers)

    lane_vec = lambda v16: jnp.zeros((1, LANES), F32).at[0, 32:48].set(v16.reshape(16))
    new = []
    for l in range(n_layers):
        y = _in_proj(x, mod, l, norm1[l][None], w_in_p[l], mod_row, 512)

        qa, k4, v4, kn_a = _prep_a(y, cos, sin, jnp.tile(a_qnorm[l], HEADS)[None],
                                   jnp.tile(a_knorm[l], KV_HEADS_A)[None], tab_row)
        qb, kb, vb, ckvn = _prep_b(y, cos, sin, b_qnorm[l][None], b_kvnorm[l][None],
                                   w_uq_p, w_k_p, w_v_p, l, tab_row)
        qn, kn, vv, gf, gb, bef, beb = _prep_c(y, c_conv[l, :, 0, :], lane_vec(c_alog[l]),
                                               lane_vec(c_dt_bias[l]), first_last)

        cache_blk = (past, lambda b, l=l: b * n_layers + l)
        ya = jnp.concatenate([
            _attention(qa, k4, v4, None, GROUPS_A, 0, n_ctx, s_ctx, s_ctx, None),
            _attention(qa, k4, v4, (ck4, cv4), GROUPS_A, ctx_rows, n_lat, s_lat, 256, cache_blk)], axis=0)
        yb = jnp.concatenate([
            _attention(qb, kb, vb, None, GROUPS_B, 0, n_ctx, s_ctx, s_ctx, None),
            _attention(qb, kb, vb, (ckb, cvb), GROUPS_B, ctx_rows, n_lat, s_lat, 256, cache_blk)], axis=0)

        of_c, ob_c, sf_c, sb_c = _gdn_scan(qn, kn, vv, gf, gb, bef, beb, None, 0, n_ctx, s_ctx)
        s0 = (_state_to_pairs(state_fwd[:, l]), _state_to_pairs(state_bwd[:, l]))
        of_l, ob_l, _, _ = _gdn_scan(qn, kn, vv, gf, gb, bef, beb, s0, ctx_rows, n_lat, s_lat)
        o_f = jnp.concatenate([of_c, of_l], axis=0)
        o_b = jnp.concatenate([ob_c, ob_l], axis=0)

        x = _merge(x, mod, l, ya, yb, o_f, o_b, y, jnp.tile(c_onorm[l], HEADS)[None],
                   w_pa_b, w_pb_b, w_pc_b, w_out_b, mod_row)
        x = _ffn(x, mod, l, norm2[l][None], w_gate_b, w_up_b, w_down_b, final_norm[None], mod_row,
                 final=(l == n_layers - 1))

        new.append((kn_a[:ctx_rows].reshape(n_ctx, s_ctx, KV_HEADS_A, HD),
                    y[:ctx_rows, C_VA:C_VA + LANES].reshape(n_ctx, s_ctx, KV_HEADS_A, HD),
                    ckvn[:ctx_rows].reshape(n_ctx, s_ctx, KV_LORA),
                    y[:ctx_rows, C_B + 640:C_B + 640 + ROPE_B].reshape(n_ctx, s_ctx, ROPE_B),
                    _pairs_to_state(sf_c), _pairs_to_state(sb_c)))

    y_prompt = x[:ctx_rows].reshape(n_ctx, s_ctx, D_MODEL)
    y_sample = x[ctx_rows:].reshape(n_lat, s_lat, D_MODEL)
    stacked = [jnp.stack([n[i] for n in new], axis=1) for i in range(6)]
    return (y_prompt, y_sample) + tuple(stacked)
```

```python
import functools

import numpy as np
import jax
import jax.numpy as jnp
from jax import lax
from jax.experimental import pallas as pl
from jax.experimental.pallas import tpu as pltpu

F32 = jnp.float32
BF16 = jnp.bfloat16

D_MODEL = 1024
GRID_W = 64
ROPE_THETA = 10000.0
EPS = 1e-6
N_MOD = 6
HEADS = 8
KV_HEADS_A = 2
HD = 64
Q_LORA = 384
KV_LORA = 256
NOPE_B = 64
ROPE_B = 32
CHUNK = 64
CONV_CH = 3 * HEADS * HD
D_FF = 2816

C_GATES = 0
C_QKV = 3072
C_B = 4608
C_KA = 5376
C_VA = 5504
C_Z = 5632
C_QA = 6144
N_IN = 6656

LANES = 128
TOK = 256
VMEM_LIMIT = 56 * 1024 * 1024


def _cparams(sem):
    return pltpu.CompilerParams(dimension_semantics=sem, vmem_limit_bytes=VMEM_LIMIT)


def _dot(a, b):
    return jnp.dot(a, b, preferred_element_type=F32)


def _dot_nt(a, b):
    return lax.dot_general(a, b, (((1,), (1,)), ((), ())), preferred_element_type=F32)


def _dot_tn(a, b):
    return lax.dot_general(a, b, (((0,), (0,)), ((), ())), preferred_element_type=F32)


def _split3(x):
    hi = x.astype(BF16)
    r = x - hi.astype(F32)
    mid = r.astype(BF16)
    lo = (r - mid.astype(F32)).astype(BF16)
    return hi, mid, lo


def _dot3(x, w):
    hi, mid, lo = _split3(x)
    return _dot(hi, w) + _dot(mid, w) + _dot(lo, w)


def _dot3_rhs(w, x):
    hi, mid, lo = _split3(x)
    return _dot(w, hi) + _dot(w, mid) + _dot(w, lo)


def _silu(x):
    return x * jax.nn.sigmoid(x)


def _rms_rows(x, g):
    return x * lax.rsqrt(jnp.mean(x * x, axis=-1, keepdims=True) + EPS) * g


def _swap_pairs(x, half):
    lane = lax.broadcasted_iota(jnp.int32, x.shape, 1)
    n = x.shape[1]
    return jnp.where((lane & half) == 0, pltpu.roll(x, n - half, 1), pltpu.roll(x, half, 1))


def _rope_cols(x, cos, sin, half):
    outs = []
    for s in range(x.shape[1] // LANES):
        xs = x[:, s * LANES:(s + 1) * LANES]
        outs.append(xs * cos + _swap_pairs(xs, half) * sin)
    return outs[0] if len(outs) == 1 else jnp.concatenate(outs, axis=1)


def _mod_kernel(c_ref, w_ref, b_ref, o_ref):
    s = _silu(c_ref[...]).astype(BF16)
    o_ref[0] = _dot(s, w_ref[0].astype(BF16)) + b_ref[0]


def _modulation(cvec, w_mod, b_mod):
    n_layers = w_mod.shape[0]
    n_out = w_mod.shape[2]
    tn = 1536
    return pl.pallas_call(
        _mod_kernel,
        grid=(n_layers, n_out // tn),
        in_specs=[pl.BlockSpec((8, D_MODEL), lambda l, j: (0, 0)),
                  pl.BlockSpec((1, D_MODEL, tn), lambda l, j: (l, 0, j)),
                  pl.BlockSpec((1, 1, tn), lambda l, j: (l, 0, j))],
        out_specs=pl.BlockSpec((1, 8, tn), lambda l, j: (l, 0, j)),
        out_shape=jax.ShapeDtypeStruct((n_layers, 8, n_out), F32),
        compiler_params=_cparams(("parallel", "parallel")),
        name="modulation",
    )(cvec, w_mod, b_mod.reshape(n_layers, 1, n_out))


def _in_proj_kernel(x_ref, mod_ref, g_ref, w_ref, o_ref, h_ref):
    @pl.when(pl.program_id(1) == 0)
    def _():
        m = mod_ref[0, 0]
        y = _rms_rows(x_ref[...], g_ref[...])
        h_ref[...] = (y * (1.0 + m[1:2]) + m[0:1]).astype(BF16)

    o_ref[...] = _dot(h_ref[...], w_ref[...])


def _in_proj(x, mod, layer, gain, w, mod_row, tm):
    m_rows = x.shape[0]
    tn = N_IN // 4
    return pl.pallas_call(
        _in_proj_kernel,
        grid=(m_rows // tm, N_IN // tn),
        in_specs=[pl.BlockSpec((tm, D_MODEL), lambda i, j: (i, 0)),
                  pl.BlockSpec((1, 1, N_MOD, D_MODEL), lambda i, j: (layer, mod_row(i, tm), 0, 0)),
                  pl.BlockSpec((1, D_MODEL), lambda i, j: (0, 0)),
                  pl.BlockSpec((D_MODEL, tn), lambda i, j: (0, j))],
        out_specs=pl.BlockSpec((tm, tn), lambda i, j: (i, j)),
        out_shape=jax.ShapeDtypeStruct((m_rows, N_IN), F32),
        scratch_shapes=[pltpu.VMEM((tm, D_MODEL), BF16)],
        compiler_params=_cparams(("parallel", "arbitrary")),
        name="in_proj",
    )(x, mod, gain, w)


def _rope_table_kernel(ang_ref, sign_ref, cos_ref, sin_ref):
    a = ang_ref[...]
    cos_ref[...] = jnp.cos(a)
    sin_ref[...] = jnp.sin(a) * sign_ref[...]


def _rope_tables(n_lat):
    t = np.arange(n_lat)
    row = jnp.asarray((t // GRID_W).astype(np.float32))[:, None]
    col = jnp.asarray((t % GRID_W).astype(np.float32))[:, None]
    lane = np.arange(LANES)

    def table(nf, lo, width):
        inv_freq = ROPE_THETA ** (-jnp.arange(nf, dtype=F32) / nf)
        d = lane - lo
        active = (d >= 0) & (d < width)
        dd = np.where(active, d, 0)
        f_idx = dd % nf
        use_col = dd >= 2 * nf
        freq = jnp.where(jnp.asarray(active), inv_freq[f_idx], 0.0)[None, :]
        ang = jnp.where(jnp.asarray(use_col)[None, :], col, row) * freq
        sign = np.where((dd % (2 * nf)) < nf, -1.0, 1.0).astype(np.float32)
        return ang, sign

    period_a = HD
    lane_a = lane % period_a
    nf_a = HD // 4
    inv_a = ROPE_THETA ** (-jnp.arange(nf_a, dtype=F32) / nf_a)
    ang_a = jnp.where(jnp.asarray(lane_a >= 2 * nf_a)[None, :], col, row) * inv_a[lane_a % nf_a][None, :]
    sign_a = np.where((lane_a % (2 * nf_a)) < nf_a, -1.0, 1.0).astype(np.float32)
    ang_q, sign_q = table(ROPE_B // 4, NOPE_B, ROPE_B)
    ang_k, sign_k = table(ROPE_B // 4, 0, ROPE_B)
    ang = jnp.stack([ang_a, ang_q, ang_k])
    ang = jnp.concatenate([jnp.zeros((3, TOK, LANES), F32), ang], axis=1)
    sign = jnp.asarray(np.stack([sign_a, sign_q, sign_k]))[:, None, :]
    rows = TOK + n_lat
    spec = pl.BlockSpec((1, TOK, LANES), lambda a, i: (a, i, 0))
    return pl.pallas_call(
        _rope_table_kernel,
        grid=(3, rows // TOK),
        in_specs=[spec, pl.BlockSpec((1, 1, LANES), lambda a, i: (a, 0, 0))],
        out_specs=[spec, spec],
        out_shape=[jax.ShapeDtypeStruct((3, rows, LANES), F32)] * 2,
        compiler_params=_cparams(("parallel", "parallel")),
        name="rope_tables",
    )(ang, sign)


def _prep_a_kernel(qa_ref, kv_ref, cos_ref, sin_ref, gq_ref, gk_ref, ones_ref, place_ref,
                   q_ref, k4_ref, v4_ref, kn_ref):
    cos = cos_ref[0]
    sin = sin_ref[0]
    ones = ones_ref[...]
    q = qa_ref[...]
    qn = q * lax.rsqrt(_dot3(q * q, ones) * (1.0 / HD) + EPS) * gq_ref[...]
    q_ref[...] = (_rope_cols(qn, cos, sin, HD // 4) * (HD ** -0.5)).astype(BF16)
    k = kv_ref[:, :LANES]
    kn = k * lax.rsqrt(_dot3(k * k, ones[:LANES, :LANES]) * (1.0 / HD) + EPS) * gk_ref[...]
    kn_ref[...] = kn
    place = place_ref[...]
    k4_ref[...] = _dot(_rope_cols(kn, cos, sin, HD // 4).astype(BF16), place).astype(BF16)
    v4_ref[...] = _dot(kv_ref[:, LANES:].astype(BF16), place).astype(BF16)


def _place4():
    p = np.zeros((LANES, 4 * LANES), np.float32)
    for kv in range(KV_HEADS_A):
        for side in range(2):
            for d in range(HD):
                p[kv * HD + d, (2 * kv + side) * LANES + side * HD + d] = 1.0
    return jnp.asarray(p, BF16)


def _ones_blockdiag(n, group):
    idx = np.arange(n) // group
    return jnp.asarray((idx[:, None] == idx[None, :]).astype(np.float32), BF16)


def _prep_a(y, cos, sin, gq, gk, tab_row):
    m_rows = y.shape[0]
    wide = pl.BlockSpec((TOK, 4 * LANES), lambda i: (i, 0))
    tab = pl.BlockSpec((1, TOK, LANES), lambda i: (0, tab_row(i), 0))
    return pl.pallas_call(
        _prep_a_kernel,
        grid=(m_rows // TOK,),
        in_specs=[pl.BlockSpec((TOK, 512), lambda i: (i, C_QA // 512)),
                  pl.BlockSpec((TOK, 256), lambda i: (i, C_KA // 256)),
                  tab, tab,
                  pl.BlockSpec((1, 512), lambda i: (0, 0)),
                  pl.BlockSpec((1, LANES), lambda i: (0, 0)),
                  pl.BlockSpec((512, 512), lambda i: (0, 0)),
                  pl.BlockSpec((LANES, 512), lambda i: (0, 0))],
        out_specs=[wide, wide, wide, pl.BlockSpec((TOK, LANES), lambda i: (i, 0))],
        out_shape=[jax.ShapeDtypeStruct((m_rows, 512), BF16)] * 3
        + [jax.ShapeDtypeStruct((m_rows, LANES), F32)],
        compiler_params=_cparams(("parallel",)),
        name="prep_a",
    )(y, y, cos, sin, gq, gk, _ones_blockdiag(512, HD), _place4())


def _cache_a_kernel(k_ref, v_ref, place_ref, k4_ref, v4_ref):
    place = place_ref[...]
    k4_ref[...] = _dot(k_ref[...].astype(BF16), place).astype(BF16)
    v4_ref[...] = _dot(v_ref[...].astype(BF16), place).astype(BF16)


def _cache_a(ck, cv):
    rows = ck.shape[0]
    narrow = pl.BlockSpec((TOK, LANES), lambda i: (i, 0))
    wide = pl.BlockSpec((TOK, 512), lambda i: (i, 0))
    return pl.pallas_call(
        _cache_a_kernel,
        grid=(rows // TOK,),
        in_specs=[narrow, narrow, pl.BlockSpec((LANES, 512), lambda i: (0, 0))],
        out_specs=[wide, wide],
        out_shape=[jax.ShapeDtypeStruct((rows, 512), BF16)] * 2,
        compiler_params=_cparams(("parallel",)),
        name="cache_a",
    )(ck, cv, _place4())


def _prep_b_kernel(g_ref, cq_ref, sq_ref, ck_ref, sk_ref, gq_ref, gkv_ref, wq_ref, wk_ref, wv_ref,
                   q_ref, k_ref, v_ref, ckv_ref):
    g = g_ref[...]
    cqn = _rms_rows(g[:, :Q_LORA], gq_ref[...])
    q = _dot(cqn.astype(BF16), wq_ref[0])
    q = _rope_cols(q, cq_ref[0], sq_ref[0], ROPE_B // 4) * ((NOPE_B + ROPE_B) ** -0.5)
    q_ref[...] = q.astype(BF16)
    ckvn = _rms_rows(g[:, Q_LORA:Q_LORA + KV_LORA], gkv_ref[...])
    ckv_ref[...] = ckvn
    kpe = _rope_cols(g[:, Q_LORA + KV_LORA:], ck_ref[0], sk_ref[0], ROPE_B // 4)
    ckv_b = ckvn.astype(BF16)
    kin = jnp.concatenate([ckv_b, kpe.astype(BF16)], axis=1)
    k_ref[...] = _dot(kin, wk_ref[0]).astype(BF16)
    v_ref[...] = _dot(ckv_b, wv_ref[0]).astype(BF16)


def _prep_b(y, cos, sin, gq, gkv, wq, wk, wv, layer, tab_row):
    m_rows = y.shape[0]
    wide = pl.BlockSpec((TOK, HEADS * LANES), lambda i: (i, 0))

    def tab(a):
        return pl.BlockSpec((1, TOK, LANES), lambda i: (a, tab_row(i), 0))

    def wspec(k):
        return pl.BlockSpec((1, k, HEADS * LANES), lambda i: (layer, 0, 0))

    return pl.pallas_call(
        _prep_b_kernel,
        grid=(m_rows // TOK,),
        in_specs=[pl.BlockSpec((TOK, 768), lambda i: (i, C_B // 768)),
                  tab(1), tab(1), tab(2), tab(2),
                  pl.BlockSpec((1, Q_LORA), lambda i: (0, 0)),
                  pl.BlockSpec((1, KV_LORA), lambda i: (0, 0)),
                  wspec(Q_LORA), wspec(KV_LORA + LANES), wspec(KV_LORA)],
        out_specs=[wide, wide, wide, pl.BlockSpec((TOK, KV_LORA), lambda i: (i, 0))],
        out_shape=[jax.ShapeDtypeStruct((m_rows, HEADS * LANES), BF16)] * 3
        + [jax.ShapeDtypeStruct((m_rows, KV_LORA), F32)],
        compiler_params=_cparams(("parallel",)),
        name="prep_b",
    )(y, cos, sin, cos, sin, gq, gkv, wq, wk, wv)


def _cache_b_kernel(c_ref, p_ref, wk_ref, wv_ref, k_ref, v_ref):
    ckv_b = c_ref[...].astype(BF16)
    kin = jnp.concatenate([ckv_b, p_ref[...].astype(BF16)], axis=1)
    k_ref[...] = _dot(kin, wk_ref[0]).astype(BF16)
    v_ref[...] = _dot(ckv_b, wv_ref[0]).astype(BF16)


def _cache_b(ckv, kpe, wk, wv, layer_of_block):
    rows = ckv.shape[0]
    wide = pl.BlockSpec((TOK, HEADS * LANES), lambda i: (i, 0))
    return pl.pallas_call(
        _cache_b_kernel,
        grid=(rows // TOK,),
        in_specs=[pl.BlockSpec((TOK, KV_LORA), lambda i: (i, 0)),
                  pl.BlockSpec((TOK, LANES), lambda i: (i, 0)),
                  pl.BlockSpec((1, KV_LORA + LANES, HEADS * LANES), lambda i: (layer_of_block(i), 0, 0)),
                  pl.BlockSpec((1, KV_LORA, HEADS * LANES), lambda i: (layer_of_block(i), 0, 0))],
        out_specs=[wide, wide],
        out_shape=[jax.ShapeDtypeStruct((rows, HEADS * LANES), BF16)] * 2,
        compiler_params=_cparams(("parallel",)),
        name="cache_b",
    )(ckv, kpe, wk, wv)


def _attn_kernel(*refs, groups, has_cache):
    if has_cache:
        q_ref, kc_ref, vc_ref, kn_ref, vn_ref, o_ref = refs
    else:
        q_ref, kn_ref, vn_ref, o_ref = refs
    for pair in range(HEADS // 2):
        acc = None
        for side in range(2):
            qg, kg = groups[2 * pair + side]
            q = q_ref[:, qg * LANES:(qg + 1) * LANES]
            ksl = slice(kg * LANES, (kg + 1) * LANES)
            s_n = _dot_nt(q, kn_ref[:, ksl])
            m = jnp.max(s_n, axis=-1, keepdims=True)
            if has_cache:
                s_c = _dot_nt(q, kc_ref[:, ksl])
                m = jnp.maximum(m, jnp.max(s_c, axis=-1, keepdims=True))
            p_n = jnp.exp(s_n - m)
            den = jnp.sum(p_n, axis=-1, keepdims=True)
            o = _dot(p_n.astype(BF16), vn_ref[:, ksl])
            if has_cache:
                p_c = jnp.exp(s_c - m)
                den = den + jnp.sum(p_c, axis=-1, keepdims=True)
                o = o + _dot(p_c.astype(BF16), vc_ref[:, ksl])
            o = o / den
            acc = o if acc is None else acc + o
        o_ref[:, pair * LANES:(pair + 1) * LANES] = acc.astype(BF16)


def _attention(q, kn, vn, cache, groups, row0, n_seq, seq, tq, cache_block):
    m_rows = q.shape[0]
    qw, kw = q.shape[1], kn.shape[1]
    nq = seq // tq
    qb0, sb0 = row0 // tq, row0 // seq
    in_specs = [pl.BlockSpec((tq, qw), lambda b, i: (qb0 + b * nq + i, 0))]
    args = [q]
    if cache is not None:
        past = cache_block[0]
        spec = pl.BlockSpec((past, kw), lambda b, i: (cache_block[1](b), 0))
        in_specs += [spec, spec]
        args += list(cache)
    kv_spec = pl.BlockSpec((seq, kw), lambda b, i: (sb0 + b, 0))
    in_specs += [kv_spec, kv_spec]
    args += [kn, vn]
    return pl.pallas_call(
        functools.partial(_attn_kernel, groups=groups, has_cache=cache is not None),
        grid=(n_seq, nq),
        in_specs=in_specs,
        out_specs=pl.BlockSpec((tq, 512), lambda b, i: (b * nq + i, 0)),
        out_shape=jax.ShapeDtypeStruct((n_seq * seq, 512), BF16),
        compiler_params=_cparams(("parallel", "parallel")),
        name="attention",
    )(*args)


GROUPS_A = tuple((h // 2, 2 * (h // 4) + (h % 2)) for h in range(HEADS))
GROUPS_B = tuple((h, h) for h in range(HEADS))


def _prep_c_kernel(x_ref, prev_ref, next_ref, ab_ref, w_ref, alog_ref, dtb_ref, ones_ref, exp_ref,
                   q_ref, k_ref, v_ref, gf_ref, gb_ref, bf_ref, bb_ref, *, first_last):
    i = pl.program_id(0)
    is_first, is_last = first_last(i)
    x = x_ref[...]
    rows = x.shape[0]
    rid = lax.broadcasted_iota(jnp.int32, x.shape, 0)
    halo_p = jnp.where(is_first, 0.0, prev_ref[7:8, :])
    halo_n = jnp.where(is_last, 0.0, next_ref[0:1, :])
    x_prev = jnp.where(rid == 0, halo_p, pltpu.roll(x, 1, 0))
    x_next = jnp.where(rid == rows - 1, halo_n, pltpu.roll(x, rows - 1, 0))
    w = w_ref[...]
    s = _silu(x_prev * w[0:1] + x * w[1:2] + x_next * w[2:3])
    ones = ones_ref[...]
    hw = HEADS * HD
    q, k = s[:, :hw], s[:, hw:2 * hw]
    q_ref[...] = q * lax.rsqrt(_dot3(q * q, ones) + EPS) * (HD ** -0.5)
    k_ref[...] = k * lax.rsqrt(_dot3(k * k, ones) + EPS)
    v_ref[...] = s[:, 2 * hw:]
    ab = ab_ref[...]
    t = ab + dtb_ref[...]
    softplus = jnp.maximum(t, 0.0) + jnp.log1p(jnp.exp(-jnp.abs(t)))
    g = -jnp.exp(alog_ref[...]) * softplus
    lane = lax.broadcasted_iota(jnp.int32, ab.shape, 1)
    both = jnp.where(lane < 48, g, jax.nn.sigmoid(ab))
    e = _dot3(both, exp_ref[...])
    gf_ref[...] = e[:, 0 * hw:1 * hw]
    gb_ref[...] = e[:, 1 * hw:2 * hw]
    bf_ref[...] = e[:, 2 * hw:3 * hw]
    bb_ref[...] = e[:, 3 * hw:4 * hw]


def _expand_matrix():
    e = np.zeros((LANES, 4 * HEADS * HD), np.float32)
    for kind in range(2):
        for d in range(2):
            for h in range(HEADS):
                src = 32 + 16 * kind + 8 * d + h
                dst = (2 * kind + d) * HEADS * HD + h * HD
                e[src, dst:dst + HD] = 1.0
    return jnp.asarray(e, BF16)


def _prep_c(y, w_conv, alog, dtb, first_last):
    m_rows = y.shape[0]
    nb8 = TOK // 8
    last8 = m_rows // 8 - 1
    qkv_blk = C_QKV // CONV_CH
    out = pl.BlockSpec((TOK, HEADS * HD), lambda i: (i, 0))
    vec = pl.BlockSpec((1, LANES), lambda i: (0, 0))
    return pl.pallas_call(
        functools.partial(_prep_c_kernel, first_last=first_last),
        grid=(m_rows // TOK,),
        in_specs=[pl.BlockSpec((TOK, CONV_CH), lambda i: (i, qkv_blk)),
                  pl.BlockSpec((8, CONV_CH), lambda i: (jnp.maximum(i * nb8 - 1, 0), qkv_blk)),
                  pl.BlockSpec((8, CONV_CH), lambda i: (jnp.minimum((i + 1) * nb8, last8), qkv_blk)),
                  pl.BlockSpec((TOK, LANES), lambda i: (i, (C_B + 640) // LANES)),
                  pl.BlockSpec((3, CONV_CH), lambda i: (0, 0)),
                  vec, vec,
                  pl.BlockSpec((512, 512), lambda i: (0, 0)),
                  pl.BlockSpec((LANES, 4 * HEADS * HD), lambda i: (0, 0))],
        out_specs=[out] * 7,
        out_shape=[jax.ShapeDtypeStruct((m_rows, HEADS * HD), F32)] * 7,
        compiler_params=_cparams(("parallel",)),
        name="prep_c",
    )(y, y, y, y, w_conv, alog, dtb, _ones_blockdiag(512, HD), _expand_matrix())


def _gdn_kernel(*refs, has_s0, n_chunks):
    ins = refs[:11]
    tri_ref = ins[10]
    pos = 11
    if has_s0:
        s0_refs = refs[pos:pos + 2]
        pos += 2
    o_refs = refs[pos:pos + 2]
    s_out_refs = refs[pos + 2:pos + 4]
    s_scr = refs[pos + 4]
    c = pl.program_id(1)

    @pl.when(c == 0)
    def _():
        for d in range(2):
            s_scr[d] = s0_refs[d][0] if has_s0 else jnp.zeros(s_scr.shape[1:], F32)

    n2 = 2 * CHUNK
    ri = lax.broadcasted_iota(jnp.int32, (n2, n2), 0)
    ci = lax.broadcasted_iota(jnp.int32, (n2, n2), 1)
    same = (ri // CHUNK) == (ci // CHUNK)
    left = lax.broadcasted_iota(jnp.int32, (CHUNK, n2), 1) < CHUNK
    eye = (ri == ci).astype(F32)

    def level_mask(lvl, lower):
        bit = 1 << lvl
        hi, lo = (ri, ci) if lower else (ci, ri)
        return ((ri >> (lvl + 1)) == (ci >> (lvl + 1))) & ((hi & bit) != 0) & ((lo & bit) == 0)

    levels = [[level_mask(lvl, d == 0) for lvl in range(6)] for d in range(2)]

    def block_diag(x):
        return jnp.concatenate([jnp.where(left, x, 0.0), jnp.where(left, 0.0, x)], axis=0)

    def anti_diag(x):
        xr = pltpu.roll(x, CHUNK, 1)
        return jnp.concatenate([jnp.where(left, 0.0, xr), jnp.where(left, xr, 0.0)], axis=0)

    chains = [(d, p) for d in range(2) for p in range(HEADS // 2)]
    incl = [same & ((ri >= ci) if d == 0 else (ri <= ci)) for d in range(2)]
    strict = [same & ((ri > ci) if d == 0 else (ri < ci)) for d in range(2)]

    def load(which):
        return [ins[5 * d + which][:, p * n2:(p + 1) * n2] for d, p in chains]

    q, k, v, g, beta = load(0), load(1), load(2), load(3), load(4)
    gc = [_dot3_rhs(tri_ref[d], gi) for (d, _), gi in zip(chains, g)]
    gtot = [gi[CHUNK - 1:CHUNK, :] if d == 0 else gi[0:1, :] for (d, _), gi in zip(chains, gc)]
    eg = [jnp.exp(gi) for gi in gc]
    kb = [ki * bi for ki, bi in zip(k, beta)]
    decay = []
    for (d, _), gi in zip(chains, gc):
        gcr = pltpu.roll(gi, CHUNK, 1)
        gcol = jnp.concatenate([jnp.where(left, gi, gcr), jnp.where(left, gcr, gi)], axis=0)
        diff = gcol - gcol.T
        decay.append(jnp.where(incl[d], jnp.exp(jnp.where(incl[d], diff, 0.0)), 0.0))
    k_bd = [block_diag(ki).astype(BF16) for ki in k]
    kk = [_dot_nt(block_diag(kbi).astype(BF16), kbd) for kbi, kbd in zip(kb, k_bd)]
    qk = [_dot_nt(block_diag(qi).astype(BF16), kbd) for qi, kbd in zip(q, k_bd)]
    a_mat = [jnp.where(strict[d], kki * di, 0.0) for (d, _), kki, di in zip(chains, kk, decay)]
    qk = [(qki * di).astype(BF16) for qki, di in zip(qk, decay)]
    x = []
    for vi, bi, kbi, egi in zip(v, beta, kb, eg):
        vb = vi * bi
        kbe_r = pltpu.roll(kbi * egi, CHUNK, 1)
        x.append(jnp.concatenate([jnp.where(left, vb, kbe_r), jnp.where(left, kbe_r, vb)], axis=0))
    corr = [-jnp.where(levels[d][0], ai, 0.0) for (d, _), ai in zip(chains, a_mat)]
    for lvl in range(1, 6):
        d_b = [(eye + ci_).astype(BF16) for ci_ in corr]
        t = [_dot(jnp.where(levels[d][lvl], ai, 0.0).astype(BF16), di).astype(BF16)
             for (d, _), ai, di in zip(chains, a_mat, d_b)]
        corr = [ci_ - _dot(di, ti) for ci_, di, ti in zip(corr, d_b, t)]
    x = [xi + _dot(ci_.astype(BF16), xi.astype(BF16)) for xi, ci_ in zip(x, corr)]
    s = [s_scr[d, p] for d, p in chains]
    s_b = [si.astype(BF16) for si in s]
    ws = [_dot(jnp.where(same, 0.0, xi).astype(BF16), sbi) for xi, sbi in zip(x, s_b)]
    v_new = [(jnp.where(same, xi, 0.0) - wi).astype(BF16) for xi, wi in zip(x, ws)]
    o_s = [_dot(anti_diag(qi * egi).astype(BF16), sbi) for qi, egi, sbi in zip(q, eg, s_b)]
    o_v = [_dot(qki, vni) for qki, vni in zip(qk, v_new)]
    kd = [anti_diag(ki * jnp.exp(gt - gi)).astype(BF16) for ki, gt, gi in zip(k, gtot, gc)]
    s_upd = [_dot_tn(kdi, vni) for kdi, vni in zip(kd, v_new)]
    for i, (d, p) in enumerate(chains):
        o = o_s[i] + o_v[i]
        o_refs[d][:, p * n2:(p + 1) * n2] = o[:CHUNK] + o[CHUNK:]
        s_scr[d, p] = s[i] * jnp.exp(gtot[i]) + s_upd[i]

    @pl.when(c == n_chunks - 1)
    def _():
        for d in range(2):
            s_out_refs[d][0] = s_scr[d]


def _tri_ones():
    r = np.arange(CHUNK)
    low = (r[:, None] >= r[None, :]).astype(np.float32)
    return jnp.asarray(np.stack([low, low.T]), BF16)


def _gdn_scan(qn, kn, v, gf, gb, bf, bb, s0, row0, n_seq, seq):
    nc = seq // CHUNK
    cb0 = row0 // CHUNK
    hw = HEADS * HD
    n2 = 2 * CHUNK
    fwd = pl.BlockSpec((CHUNK, hw), lambda b, c: (cb0 + b * nc + c, 0))
    bwd = pl.BlockSpec((CHUNK, hw), lambda b, c: (cb0 + b * nc + nc - 1 - c, 0))
    o_fwd = pl.BlockSpec((CHUNK, hw), lambda b, c: (b * nc + c, 0))
    o_bwd = pl.BlockSpec((CHUNK, hw), lambda b, c: (b * nc + nc - 1 - c, 0))
    state = pl.BlockSpec((1, HEADS // 2, n2, n2), lambda b, c: (b, 0, 0, 0))
    in_specs = [fwd] * 5 + [bwd] * 5 + [pl.BlockSpec((2, CHUNK, CHUNK), lambda b, c: (0, 0, 0))]
    args = [qn, kn, v, gf, bf, qn, kn, v, gb, bb, _tri_ones()]
    if s0 is not None:
        in_specs += [state, state]
        args += list(s0)
    st_shape = jax.ShapeDtypeStruct((n_seq, HEADS // 2, n2, n2), F32)
    o_shape = jax.ShapeDtypeStruct((n_seq * seq, hw), F32)
    return pl.pallas_call(
        functools.partial(_gdn_kernel, has_s0=s0 is not None, n_chunks=nc),
        grid=(n_seq, nc),
        in_specs=in_specs,
        out_specs=[o_fwd, o_bwd, state, state],
        out_shape=[o_shape, o_shape, st_shape, st_shape],
        scratch_shapes=[pltpu.VMEM((2, HEADS // 2, n2, n2), F32)],
        compiler_params=_cparams(("parallel", "arbitrary")),
        name="gdn_scan",
    )(*args)


def _state_to_pairs(s):
    b = s.shape[0]
    s = s.reshape(b, HEADS // 2, 2, HD, HD)
    z = jnp.zeros_like(s[:, :, 0])
    top = jnp.concatenate([z, s[:, :, 1]], axis=-1)
    bot = jnp.concatenate([s[:, :, 0], z], axis=-1)
    return jnp.concatenate([top, bot], axis=-2)


def _pairs_to_state(s):
    b = s.shape[0]
    even = s[:, :, HD:, :HD]
    odd = s[:, :, :HD, HD:]
    return jnp.stack([even, odd], axis=2).reshape(b, HEADS, HD, HD)


def _merge_kernel(x_ref, mod_ref, ya_ref, yb_ref, of_ref, ob_ref, z_ref, gates_ref, gain_ref, ones_ref,
                  wa_ref, wb_ref, wc_ref, wo_ref, o_ref):
    o = of_ref[...] + ob_ref[...]
    ms = _dot3(o * o, ones_ref[...]) * (1.0 / HD)
    yc = o * lax.rsqrt(ms + EPS) * gain_ref[...] * _silu(z_ref[...])
    gates = gates_ref[...]
    merged = (jax.nn.sigmoid(gates[:, :D_MODEL]) * _dot(ya_ref[...], wa_ref[0])
              + jax.nn.sigmoid(gates[:, D_MODEL:2 * D_MODEL]) * _dot(yb_ref[...], wb_ref[0])
              + jax.nn.sigmoid(gates[:, 2 * D_MODEL:]) * _dot(yc.astype(BF16), wc_ref[0]))
    mix = _dot(merged.astype(BF16), wo_ref[0])
    o_ref[...] = x_ref[...] + mod_ref[0, 0][2:3] * mix


def _merge(x, mod, layer, ya, yb, o_f, o_b, y, gain, wa, wb, wc, wo, mod_row):
    m_rows = x.shape[0]
    tm = TOK
    half = pl.BlockSpec((tm, 512), lambda i: (i, 0))
    full = pl.BlockSpec((tm, D_MODEL), lambda i: (i, 0))

    def wspec(k):
        return pl.BlockSpec((1, k, D_MODEL), lambda i: (layer, 0, 0))

    return pl.pallas_call(
        _merge_kernel,
        grid=(m_rows // tm,),
        in_specs=[full,
                  pl.BlockSpec((1, 1, N_MOD, D_MODEL), lambda i: (layer, mod_row(i, tm), 0, 0)),
                  half, half, half, half,
                  pl.BlockSpec((tm, 512), lambda i: (i, C_Z // 512)),
                  pl.BlockSpec((tm, 3 * D_MODEL), lambda i: (i, 0)),
                  pl.BlockSpec((1, 512), lambda i: (0, 0)),
                  pl.BlockSpec((512, 512), lambda i: (0, 0)),
                  wspec(512), wspec(512), wspec(512), wspec(D_MODEL)],
        out_specs=full,
        out_shape=jax.ShapeDtypeStruct((m_rows, D_MODEL), F32),
        compiler_params=_cparams(("parallel",)),
        name="merge_out",
    )(x, mod, ya, yb, o_f, o_b, y, y, gain, _ones_blockdiag(512, HD), wa, wb, wc, wo)


def _ffn_kernel(x_ref, mod_ref, g_ref, wg_ref, wu_ref, wd_ref, fin_ref, o_ref, *, final):
    x = x_ref[...]
    m = mod_ref[0, 0]
    h = (_rms_rows(x, g_ref[...]) * (1.0 + m[4:5]) + m[3:4]).astype(BF16)
    act = (_silu(_dot(h, wg_ref[0])) * _dot(h, wu_ref[0])).astype(BF16)
    out = x + m[5:6] * _dot(act, wd_ref[0])
    if final:
        out = _rms_rows(out, fin_ref[...])
    o_ref[...] = out


def _ffn(x, mod, layer, gain, wg, wu, wd, fin, mod_row, final):
    m_rows = x.shape[0]
    tm = TOK
    full = pl.BlockSpec((tm, D_MODEL), lambda i: (i, 0))
    vec = pl.BlockSpec((1, D_MODEL), lambda i: (0, 0))
    once = pl.Buffered(1)
    return pl.pallas_call(
        functools.partial(_ffn_kernel, final=final),
        grid=(m_rows // tm,),
        in_specs=[full,
                  pl.BlockSpec((1, 1, N_MOD, D_MODEL), lambda i: (layer, mod_row(i, tm), 0, 0)),
                  vec,
                  pl.BlockSpec((1, D_MODEL, D_FF), lambda i: (layer, 0, 0), pipeline_mode=once),
                  pl.BlockSpec((1, D_MODEL, D_FF), lambda i: (layer, 0, 0), pipeline_mode=once),
                  pl.BlockSpec((1, D_FF, D_MODEL), lambda i: (layer, 0, 0), pipeline_mode=once),
                  vec],
        out_specs=full,
        out_shape=jax.ShapeDtypeStruct((m_rows, D_MODEL), F32),
        compiler_params=_cparams(("parallel",)),
        name="ffn",
    )(x, mod, gain, wg, wu, wd, fin)


def _layout_w_in(w_in):
    def cols(a, b):
        return w_in[:, :, a:b]

    pad = jnp.zeros(w_in.shape[:2] + (64,), w_in.dtype)
    parts = [cols(3520, 6592), cols(1440, 2976), cols(768, 1152), cols(1152, 1408), cols(1408, 1440),
             cols(3488, 3504), cols(3504, 3520), pad, cols(512, 640), cols(640, 768), cols(2976, 3488),
             cols(0, 512)]
    return jnp.concatenate(parts, axis=-1).astype(BF16)


def _layout_w_uq(w_uq):
    n_layers = w_uq.shape[0]
    w = w_uq.reshape(n_layers, Q_LORA, HEADS, NOPE_B + ROPE_B)
    w = jnp.pad(w, ((0, 0), (0, 0), (0, 0), (0, LANES - NOPE_B - ROPE_B)))
    return w.reshape(n_layers, Q_LORA, HEADS * LANES).astype(BF16)


def _layout_w_ukv(w_ukv):
    n_layers = w_ukv.shape[0]
    w = w_ukv.reshape(n_layers, KV_LORA, HEADS, 2, HD)
    zero = jnp.zeros((n_layers, KV_LORA, HEADS, HD), w.dtype)
    wk_top = jnp.concatenate([w[:, :, :, 0], zero], axis=-1).reshape(n_layers, KV_LORA, HEADS * LANES)
    place = np.zeros((LANES, HEADS, LANES), np.float32)
    for d in range(ROPE_B):
        place[d, :, NOPE_B + d] = 1.0
    place = jnp.broadcast_to(jnp.asarray(place.reshape(LANES, HEADS * LANES)), (n_layers, LANES, HEADS * LANES))
    wk = jnp.concatenate([wk_top, place], axis=1).astype(BF16)
    wv = w[:, :, :, 1].reshape(n_layers, KV_LORA, HEADS // 2, 2, HD)
    zero = jnp.zeros((n_layers, KV_LORA, HEADS // 2, HD), w.dtype)
    even = jnp.concatenate([wv[:, :, :, 0], zero], axis=-1)
    odd = jnp.concatenate([zero, wv[:, :, :, 1]], axis=-1)
    wv = jnp.stack([even, odd], axis=3).reshape(n_layers, KV_LORA, HEADS * LANES).astype(BF16)
    return wk, wv


def kernel(x_prompt, x_sample, cache_ka, cache_va, cache_ckv, cache_kpe, state_fwd, state_bwd, c, c_ctx,
           w_mod, b_mod, norm1, norm2, w_in, a_qnorm, a_knorm, b_qnorm, b_kvnorm, w_uq, w_ukv, c_conv,
           c_alog, c_dt_bias, c_onorm, w_pa, w_pb, w_pc, w_out, w_gate, w_up, w_down, final_norm):
    n_ctx, s_ctx, _ = x_prompt.shape
    n_lat, s_lat, _ = x_sample.shape
    n_layers = w_mod.shape[0]
    past = cache_ka.shape[2]
    ctx_rows = n_ctx * s_ctx
    lat_rows = n_lat * s_lat
    assert s_ctx == TOK and s_lat % TOK == 0 and ctx_rows % 512 == 0 and s_lat % 512 == 0
    assert past % TOK == 0 and n_lat + 1 <= 8 and ctx_rows % s_lat == 0

    def mod_row(i, tm):
        return jnp.where(i < ctx_rows // tm, 0, 1 + (i - ctx_rows // tm) // (s_lat // tm))

    def tab_row(i):
        nb = ctx_rows // TOK
        return jnp.where(i < nb, 0, 1 + (i - nb) % (s_lat // TOK))

    def first_last(i):
        nb = ctx_rows // TOK
        j = (i - nb) % (s_lat // TOK)
        is_ctx = i < nb
        return is_ctx | (j == 0), is_ctx | (j == s_lat // TOK - 1)

    x = jnp.concatenate([x_prompt.reshape(ctx_rows, D_MODEL), x_sample.reshape(lat_rows, D_MODEL)], axis=0)
    cvec = jnp.zeros((8, D_MODEL), F32).at[0].set(c_ctx).at[1:1 + n_lat].set(c)
    mod = _modulation(cvec, w_mod, b_mod).reshape(n_layers, 8, N_MOD, D_MODEL)
    cos, sin = _rope_tables(s_lat)

    w_in_p = _layout_w_in(w_in)
    w_uq_p = _layout_w_uq(w_uq)
    w_k_p, w_v_p = _layout_w_ukv(w_ukv)
    bf = lambda w: w.astype(BF16)
    w_pa_b, w_pb_b, w_pc_b, w_out_b = bf(w_pa), bf(w_pb), bf(w_pc), bf(w_out)
    w_gate_b, w_up_b, w_down_b = bf(w_gate), bf(w_up), bf(w_down)

    cache_rows = n_lat * n_layers * past
    ck4, cv4 = _cache_a(cache_ka.reshape(cache_rows, LANES), cache_va.reshape(cache_rows, LANES))
    kpe_pad = jnp.pad(cache_kpe.reshape(cache_rows, ROPE_B), ((0, 0), (0, LANES - ROPE_B)))
    ckb, cvb = _cache_b(cache_ckv.reshape(cache_rows, KV_LORA), kpe_pad, w_k_p, w_v_p,
                        lambda i: (i // (past // TOK)) % n_layers)

    lane_vec = lambda v16: jnp.zeros((1, LANES), F32).at[0, 32:48].set(v16.reshape(16))
    new = []
    for l in range(n_layers):
        y = _in_proj(x, mod, l, norm1[l][None], w_in_p[l], mod_row, 512)

        qa, k4, v4, kn_a = _prep_a(y, cos, sin, jnp.tile(a_qnorm[l], HEADS)[None],
                                   jnp.tile(a_knorm[l], KV_HEADS_A)[None], tab_row)
        qb, kb, vb, ckvn = _prep_b(y, cos, sin, b_qnorm[l][None], b_kvnorm[l][None],
                                   w_uq_p, w_k_p, w_v_p, l, tab_row)
        qn, kn, vv, gf, gb, bef, beb = _prep_c(y, c_conv[l, :, 0, :], lane_vec(c_alog[l]),
                                               lane_vec(c_dt_bias[l]), first_last)

        cache_blk = (past, lambda b, l=l: b * n_layers + l)
        ya = jnp.concatenate([
            _attention(qa, k4, v4, None, GROUPS_A, 0, n_ctx, s_ctx, s_ctx, None),
            _attention(qa, k4, v4, (ck4, cv4), GROUPS_A, ctx_rows, n_lat, s_lat, 256, cache_blk)], axis=0)
        yb = jnp.concatenate([
            _attention(qb, kb, vb, None, GROUPS_B, 0, n_ctx, s_ctx, s_ctx, None),
            _attention(qb, kb, vb, (ckb, cvb), GROUPS_B, ctx_rows, n_lat, s_lat, 256, cache_blk)], axis=0)

        of_c, ob_c, sf_c, sb_c = _gdn_scan(qn, kn, vv, gf, gb, bef, beb, None, 0, n_ctx, s_ctx)
        s0 = (_state_to_pairs(state_fwd[:, l]), _state_to_pairs(state_bwd[:, l]))
        of_l, ob_l, _, _ = _gdn_scan(qn, kn, vv, gf, gb, bef, beb, s0, ctx_rows, n_lat, s_lat)
        o_f = jnp.concatenate([of_c, of_l], axis=0)
        o_b = jnp.concatenate([ob_c, ob_l], axis=0)

        x = _merge(x, mod, l, ya, yb, o_f, o_b, y, jnp.tile(c_onorm[l], HEADS)[None],
                   w_pa_b, w_pb_b, w_pc_b, w_out_b, mod_row)
        x = _ffn(x, mod, l, norm2[l][None], w_gate_b, w_up_b, w_down_b, final_norm[None], mod_row,
                 final=(l == n_layers - 1))

        new.append((kn_a[:ctx_rows].reshape(n_ctx, s_ctx, KV_HEADS_A, HD),
                    y[:ctx_rows, C_VA:C_VA + LANES].reshape(n_ctx, s_ctx, KV_HEADS_A, HD),
                    ckvn[:ctx_rows].reshape(n_ctx, s_ctx, KV_LORA),
                    y[:ctx_rows, C_B + 640:C_B + 640 + ROPE_B].reshape(n_ctx, s_ctx, ROPE_B),
                    _pairs_to_state(sf_c), _pairs_to_state(sb_c)))

    y_prompt = x[:ctx_rows].reshape(n_ctx, s_ctx, D_MODEL)
    y_sample = x[ctx_rows:].reshape(n_lat, s_lat, D_MODEL)
    stacked = [jnp.stack([n[i] for n in new], axis=1) for i in range(6)]
    return (y_prompt, y_sample) + tuple(stacked)
```

```python
import functools

import numpy as np
import jax
import jax.numpy as jnp
from jax import lax
from jax.experimental import pallas as pl
from jax.experimental.pallas import tpu as pltpu

F32 = jnp.float32
BF16 = jnp.bfloat16

D_MODEL = 1024
GRID_W = 64
ROPE_THETA = 10000.0
EPS = 1e-6
N_MOD = 6
HEADS = 8
KV_HEADS_A = 2
HD = 64
Q_LORA = 384
KV_LORA = 256
NOPE_B = 64
ROPE_B = 32
CHUNK = 64
CONV_CH = 3 * HEADS * HD
D_FF = 2816

C_GATES = 0
C_QKV = 3072
C_B = 4608
C_KA = 5376
C_VA = 5504
C_Z = 5632
C_QA = 6144
N_IN = 6656

LOG2E = 1.4426950408889634
LANES = 128
TOK = 256
VMEM_LIMIT = 56 * 1024 * 1024


def _cparams(sem):
    return pltpu.CompilerParams(dimension_semantics=sem, vmem_limit_bytes=VMEM_LIMIT)


def _dot(a, b):
    return jnp.dot(a, b, preferred_element_type=F32)


def _dot_nt(a, b):
    return lax.dot_general(a, b, (((1,), (1,)), ((), ())), preferred_element_type=F32)


def _dot_tn(a, b):
    return lax.dot_general(a, b, (((0,), (0,)), ((), ())), preferred_element_type=F32)


def _split3(x):
    hi = x.astype(BF16)
    r = x - hi.astype(F32)
    mid = r.astype(BF16)
    lo = (r - mid.astype(F32)).astype(BF16)
    return hi, mid, lo


def _dot3(x, w):
    hi, mid, lo = _split3(x)
    return _dot(hi, w) + _dot(mid, w) + _dot(lo, w)


def _dot3_rhs(w, x):
    hi, mid, lo = _split3(x)
    return _dot(w, hi) + _dot(w, mid) + _dot(w, lo)


def _silu(x):
    return x * jax.nn.sigmoid(x)


def _rms_rows(x, g):
    return x * lax.rsqrt(jnp.mean(x * x, axis=-1, keepdims=True) + EPS) * g


def _swap_pairs(x, half):
    lane = lax.broadcasted_iota(jnp.int32, x.shape, 1)
    n = x.shape[1]
    return jnp.where((lane & half) == 0, pltpu.roll(x, n - half, 1), pltpu.roll(x, half, 1))


def _rope_cols(x, cos, sin, half):
    outs = []
    for s in range(x.shape[1] // LANES):
        xs = x[:, s * LANES:(s + 1) * LANES]
        outs.append(xs * cos + _swap_pairs(xs, half) * sin)
    return outs[0] if len(outs) == 1 else jnp.concatenate(outs, axis=1)


def _mod_kernel(c_ref, w_ref, b_ref, o_ref):
    s = _silu(c_ref[...]).astype(BF16)
    o_ref[0] = _dot(s, w_ref[0].astype(BF16)) + b_ref[0]


def _modulation(cvec, w_mod, b_mod):
    n_layers = w_mod.shape[0]
    n_out = w_mod.shape[2]
    tn = 1536
    return pl.pallas_call(
        _mod_kernel,
        grid=(n_layers, n_out // tn),
        in_specs=[pl.BlockSpec((8, D_MODEL), lambda l, j: (0, 0)),
                  pl.BlockSpec((1, D_MODEL, tn), lambda l, j: (l, 0, j)),
                  pl.BlockSpec((1, 1, tn), lambda l, j: (l, 0, j))],
        out_specs=pl.BlockSpec((1, 8, tn), lambda l, j: (l, 0, j)),
        out_shape=jax.ShapeDtypeStruct((n_layers, 8, n_out), F32),
        compiler_params=_cparams(("parallel", "parallel")),
        name="modulation",
    )(cvec, w_mod, b_mod.reshape(n_layers, 1, n_out))


def _in_proj_kernel(x_ref, mod_ref, g_ref, w_ref, o_ref, h_ref):
    @pl.when(pl.program_id(1) == 0)
    def _():
        m = mod_ref[0, 0]
        y = _rms_rows(x_ref[...], g_ref[...])
        h_ref[...] = (y * (1.0 + m[1:2]) + m[0:1]).astype(BF16)

    o_ref[...] = _dot(h_ref[...], w_ref[...])


def _in_proj(x, mod, layer, gain, w, mod_row, tm):
    m_rows = x.shape[0]
    tn = N_IN // 4
    return pl.pallas_call(
        _in_proj_kernel,
        grid=(m_rows // tm, N_IN // tn),
        in_specs=[pl.BlockSpec((tm, D_MODEL), lambda i, j: (i, 0)),
                  pl.BlockSpec((1, 1, N_MOD, D_MODEL), lambda i, j: (layer, mod_row(i, tm), 0, 0)),
                  pl.BlockSpec((1, D_MODEL), lambda i, j: (0, 0)),
                  pl.BlockSpec((D_MODEL, tn), lambda i, j: (0, j))],
        out_specs=pl.BlockSpec((tm, tn), lambda i, j: (i, j)),
        out_shape=jax.ShapeDtypeStruct((m_rows, N_IN), F32),
        scratch_shapes=[pltpu.VMEM((tm, D_MODEL), BF16)],
        compiler_params=_cparams(("parallel", "arbitrary")),
        name="in_proj",
    )(x, mod, gain, w)


def _rope_table_kernel(ang_ref, sign_ref, cos_ref, sin_ref):
    a = ang_ref[...]
    cos_ref[...] = jnp.cos(a)
    sin_ref[...] = jnp.sin(a) * sign_ref[...]


def _rope_tables(n_lat):
    t = np.arange(n_lat)
    row = jnp.asarray((t // GRID_W).astype(np.float32))[:, None]
    col = jnp.asarray((t % GRID_W).astype(np.float32))[:, None]
    lane = np.arange(LANES)

    def table(nf, lo, width):
        inv_freq = ROPE_THETA ** (-jnp.arange(nf, dtype=F32) / nf)
        d = lane - lo
        active = (d >= 0) & (d < width)
        dd = np.where(active, d, 0)
        f_idx = dd % nf
        use_col = dd >= 2 * nf
        freq = jnp.where(jnp.asarray(active), inv_freq[f_idx], 0.0)[None, :]
        ang = jnp.where(jnp.asarray(use_col)[None, :], col, row) * freq
        sign = np.where((dd % (2 * nf)) < nf, -1.0, 1.0).astype(np.float32)
        return ang, sign

    period_a = HD
    lane_a = lane % period_a
    nf_a = HD // 4
    inv_a = ROPE_THETA ** (-jnp.arange(nf_a, dtype=F32) / nf_a)
    ang_a = jnp.where(jnp.asarray(lane_a >= 2 * nf_a)[None, :], col, row) * inv_a[lane_a % nf_a][None, :]
    sign_a = np.where((lane_a % (2 * nf_a)) < nf_a, -1.0, 1.0).astype(np.float32)
    ang_q, sign_q = table(ROPE_B // 4, NOPE_B, ROPE_B)
    ang_k, sign_k = table(ROPE_B // 4, 0, ROPE_B)
    ang = jnp.stack([ang_a, ang_q, ang_k])
    ang = jnp.concatenate([jnp.zeros((3, TOK, LANES), F32), ang], axis=1)
    sign = jnp.asarray(np.stack([sign_a, sign_q, sign_k]))[:, None, :]
    rows = TOK + n_lat
    spec = pl.BlockSpec((1, TOK, LANES), lambda a, i: (a, i, 0))
    return pl.pallas_call(
        _rope_table_kernel,
        grid=(3, rows // TOK),
        in_specs=[spec, pl.BlockSpec((1, 1, LANES), lambda a, i: (a, 0, 0))],
        out_specs=[spec, spec],
        out_shape=[jax.ShapeDtypeStruct((3, rows, LANES), F32)] * 2,
        compiler_params=_cparams(("parallel", "parallel")),
        name="rope_tables",
    )(ang, sign)


def _prep_a_kernel(qa_ref, kv_ref, cos_ref, sin_ref, gq_ref, gk_ref, ones_ref, place_ref,
                   q_ref, k4_ref, v4_ref, kn_ref):
    cos = cos_ref[0]
    sin = sin_ref[0]
    ones = ones_ref[...]
    q = qa_ref[...]
    qn = q * lax.rsqrt(_dot3(q * q, ones) * (1.0 / HD) + EPS) * gq_ref[...]
    q_ref[...] = (_rope_cols(qn, cos, sin, HD // 4) * (HD ** -0.5 * LOG2E)).astype(BF16)
    k = kv_ref[:, :LANES]
    kn = k * lax.rsqrt(_dot3(k * k, ones[:LANES, :LANES]) * (1.0 / HD) + EPS) * gk_ref[...]
    kn_ref[...] = kn
    place = place_ref[...]
    k4_ref[...] = _dot(_rope_cols(kn, cos, sin, HD // 4).astype(BF16), place).astype(BF16)
    v4_ref[...] = _dot(kv_ref[:, LANES:].astype(BF16), place).astype(BF16)


def _place4():
    p = np.zeros((LANES, 4 * LANES), np.float32)
    for kv in range(KV_HEADS_A):
        for side in range(2):
            for d in range(HD):
                p[kv * HD + d, (2 * kv + side) * LANES + side * HD + d] = 1.0
    return jnp.asarray(p, BF16)


def _ones_blockdiag(n, group):
    idx = np.arange(n) // group
    return jnp.asarray((idx[:, None] == idx[None, :]).astype(np.float32), BF16)


def _prep_a(y, cos, sin, gq, gk, tab_row):
    m_rows = y.shape[0]
    wide = pl.BlockSpec((TOK, 4 * LANES), lambda i: (i, 0))
    tab = pl.BlockSpec((1, TOK, LANES), lambda i: (0, tab_row(i), 0))
    return pl.pallas_call(
        _prep_a_kernel,
        grid=(m_rows // TOK,),
        in_specs=[pl.BlockSpec((TOK, 512), lambda i: (i, C_QA // 512)),
                  pl.BlockSpec((TOK, 256), lambda i: (i, C_KA // 256)),
                  tab, tab,
                  pl.BlockSpec((1, 512), lambda i: (0, 0)),
                  pl.BlockSpec((1, LANES), lambda i: (0, 0)),
                  pl.BlockSpec((512, 512), lambda i: (0, 0)),
                  pl.BlockSpec((LANES, 512), lambda i: (0, 0))],
        out_specs=[wide, wide, wide, pl.BlockSpec((TOK, LANES), lambda i: (i, 0))],
        out_shape=[jax.ShapeDtypeStruct((m_rows, 512), BF16)] * 3
        + [jax.ShapeDtypeStruct((m_rows, LANES), F32)],
        compiler_params=_cparams(("parallel",)),
        name="prep_a",
    )(y, y, cos, sin, gq, gk, _ones_blockdiag(512, HD), _place4())


def _cache_a_kernel(k_ref, v_ref, place_ref, k4_ref, v4_ref):
    place = place_ref[...]
    k4_ref[...] = _dot(k_ref[...].astype(BF16), place).astype(BF16)
    v4_ref[...] = _dot(v_ref[...].astype(BF16), place).astype(BF16)


def _cache_a(ck, cv):
    rows = ck.shape[0]
    narrow = pl.BlockSpec((TOK, LANES), lambda i: (i, 0))
    wide = pl.BlockSpec((TOK, 512), lambda i: (i, 0))
    return pl.pallas_call(
        _cache_a_kernel,
        grid=(rows // TOK,),
        in_specs=[narrow, narrow, pl.BlockSpec((LANES, 512), lambda i: (0, 0))],
        out_specs=[wide, wide],
        out_shape=[jax.ShapeDtypeStruct((rows, 512), BF16)] * 2,
        compiler_params=_cparams(("parallel",)),
        name="cache_a",
    )(ck, cv, _place4())


def _prep_b_kernel(g_ref, cq_ref, sq_ref, ck_ref, sk_ref, gq_ref, gkv_ref, wq_ref, wk_ref, wv_ref,
                   q_ref, k_ref, v_ref, ckv_ref):
    g = g_ref[...]
    cqn = _rms_rows(g[:, :Q_LORA], gq_ref[...])
    q = _dot(cqn.astype(BF16), wq_ref[0])
    q = _rope_cols(q, cq_ref[0], sq_ref[0], ROPE_B // 4) * ((NOPE_B + ROPE_B) ** -0.5 * LOG2E)
    q_ref[...] = q.astype(BF16)
    ckvn = _rms_rows(g[:, Q_LORA:Q_LORA + KV_LORA], gkv_ref[...])
    ckv_ref[...] = ckvn
    kpe = _rope_cols(g[:, Q_LORA + KV_LORA:], ck_ref[0], sk_ref[0], ROPE_B // 4)
    ckv_b = ckvn.astype(BF16)
    kin = jnp.concatenate([ckv_b, kpe.astype(BF16)], axis=1)
    k_ref[...] = _dot(kin, wk_ref[0]).astype(BF16)
    v_ref[...] = _dot(ckv_b, wv_ref[0]).astype(BF16)


def _prep_b(y, cos, sin, gq, gkv, wq, wk, wv, layer, tab_row):
    m_rows = y.shape[0]
    wide = pl.BlockSpec((TOK, HEADS * LANES), lambda i: (i, 0))

    def tab(a):
        return pl.BlockSpec((1, TOK, LANES), lambda i: (a, tab_row(i), 0))

    def wspec(k):
        return pl.BlockSpec((1, k, HEADS * LANES), lambda i: (layer, 0, 0))

    return pl.pallas_call(
        _prep_b_kernel,
        grid=(m_rows // TOK,),
        in_specs=[pl.BlockSpec((TOK, 768), lambda i: (i, C_B // 768)),
                  tab(1), tab(1), tab(2), tab(2),
                  pl.BlockSpec((1, Q_LORA), lambda i: (0, 0)),
                  pl.BlockSpec((1, KV_LORA), lambda i: (0, 0)),
                  wspec(Q_LORA), wspec(KV_LORA + LANES), wspec(KV_LORA)],
        out_specs=[wide, wide, wide, pl.BlockSpec((TOK, KV_LORA), lambda i: (i, 0))],
        out_shape=[jax.ShapeDtypeStruct((m_rows, HEADS * LANES), BF16)] * 3
        + [jax.ShapeDtypeStruct((m_rows, KV_LORA), F32)],
        compiler_params=_cparams(("parallel",)),
        name="prep_b",
    )(y, cos, sin, cos, sin, gq, gkv, wq, wk, wv)


def _cache_b_kernel(c_ref, p_ref, wk_ref, wv_ref, k_ref, v_ref):
    ckv_b = c_ref[...].astype(BF16)
    kin = jnp.concatenate([ckv_b, p_ref[...].astype(BF16)], axis=1)
    k_ref[...] = _dot(kin, wk_ref[0]).astype(BF16)
    v_ref[...] = _dot(ckv_b, wv_ref[0]).astype(BF16)


def _cache_b(ckv, kpe, wk, wv, layer_of_block):
    rows = ckv.shape[0]
    wide = pl.BlockSpec((TOK, HEADS * LANES), lambda i: (i, 0))
    return pl.pallas_call(
        _cache_b_kernel,
        grid=(rows // TOK,),
        in_specs=[pl.BlockSpec((TOK, KV_LORA), lambda i: (i, 0)),
                  pl.BlockSpec((TOK, LANES), lambda i: (i, 0)),
                  pl.BlockSpec((1, KV_LORA + LANES, HEADS * LANES), lambda i: (layer_of_block(i), 0, 0)),
                  pl.BlockSpec((1, KV_LORA, HEADS * LANES), lambda i: (layer_of_block(i), 0, 0))],
        out_specs=[wide, wide],
        out_shape=[jax.ShapeDtypeStruct((rows, HEADS * LANES), BF16)] * 2,
        compiler_params=_cparams(("parallel",)),
        name="cache_b",
    )(ckv, kpe, wk, wv)


def _attn_kernel(*refs, groups, has_cache):
    if has_cache:
        q_ref, kc_ref, vc_ref, kn_ref, vn_ref, o_ref = refs
    else:
        q_ref, kn_ref, vn_ref, o_ref = refs
    for pair in range(HEADS // 2):
        acc = None
        for side in range(2):
            qg, kg = groups[2 * pair + side]
            q = q_ref[:, qg * LANES:(qg + 1) * LANES]
            ksl = slice(kg * LANES, (kg + 1) * LANES)
            s_n = _dot_nt(q, kn_ref[:, ksl])
            m = jnp.max(s_n, axis=-1, keepdims=True)
            if has_cache:
                s_c = _dot_nt(q, kc_ref[:, ksl])
                m = jnp.maximum(m, jnp.max(s_c, axis=-1, keepdims=True))
            p_n = jnp.exp2(s_n - m)
            den = jnp.sum(p_n, axis=-1, keepdims=True)
            o = _dot(p_n.astype(BF16), vn_ref[:, ksl])
            if has_cache:
                p_c = jnp.exp2(s_c - m)
                den = den + jnp.sum(p_c, axis=-1, keepdims=True)
                o = o + _dot(p_c.astype(BF16), vc_ref[:, ksl])
            o = o / den
            acc = o if acc is None else acc + o
        o_ref[:, pair * LANES:(pair + 1) * LANES] = acc.astype(BF16)


def _attention(q, kn, vn, cache, groups, row0, n_seq, seq, tq, cache_block):
    m_rows = q.shape[0]
    qw, kw = q.shape[1], kn.shape[1]
    nq = seq // tq
    qb0, sb0 = row0 // tq, row0 // seq
    in_specs = [pl.BlockSpec((tq, qw), lambda b, i: (qb0 + b * nq + i, 0))]
    args = [q]
    if cache is not None:
        past = cache_block[0]
        spec = pl.BlockSpec((past, kw), lambda b, i: (cache_block[1](b), 0))
        in_specs += [spec, spec]
        args += list(cache)
    kv_spec = pl.BlockSpec((seq, kw), lambda b, i: (sb0 + b, 0))
    in_specs += [kv_spec, kv_spec]
    args += [kn, vn]
    return pl.pallas_call(
        functools.partial(_attn_kernel, groups=groups, has_cache=cache is not None),
        grid=(n_seq, nq),
        in_specs=in_specs,
        out_specs=pl.BlockSpec((tq, 512), lambda b, i: (b * nq + i, 0)),
        out_shape=jax.ShapeDtypeStruct((n_seq * seq, 512), BF16),
        compiler_params=_cparams(("parallel", "parallel")),
        name="attention",
    )(*args)


GROUPS_A = tuple((h // 2, 2 * (h // 4) + (h % 2)) for h in range(HEADS))
GROUPS_B = tuple((h, h) for h in range(HEADS))


def _prep_c_kernel(x_ref, prev_ref, next_ref, ab_ref, w_ref, alog_ref, dtb_ref, ones_ref, exp_ref, tri_ref,
                   q_ref, k_ref, v_ref, gf_ref, gb_ref, bf_ref, bb_ref, *, blocks_per_seq):
    j = pl.program_id(0) % blocks_per_seq
    is_first, is_last = j == 0, j == blocks_per_seq - 1
    x = x_ref[...]
    rows = x.shape[0]
    rid = lax.broadcasted_iota(jnp.int32, x.shape, 0)
    halo_p = jnp.where(is_first, 0.0, prev_ref[7:8, :])
    halo_n = jnp.where(is_last, 0.0, next_ref[0:1, :])
    x_prev = jnp.where(rid == 0, halo_p, pltpu.roll(x, 1, 0))
    x_next = jnp.where(rid == rows - 1, halo_n, pltpu.roll(x, rows - 1, 0))
    w = w_ref[...]
    s = _silu(x_prev * w[0:1] + x * w[1:2] + x_next * w[2:3])
    ones = ones_ref[...]
    hw = HEADS * HD
    q, k = s[:, :hw], s[:, hw:2 * hw]
    q_ref[...] = q * lax.rsqrt(_dot3(q * q, ones) + EPS) * (HD ** -0.5)
    k_ref[...] = k * lax.rsqrt(_dot3(k * k, ones) + EPS)
    v_ref[...] = s[:, 2 * hw:]
    ab = ab_ref[...]
    t = ab + dtb_ref[...]
    softplus = jnp.maximum(t, 0.0) + jnp.log1p(jnp.exp(-jnp.abs(t)))
    g = -jnp.exp(alog_ref[...]) * softplus
    lane = lax.broadcasted_iota(jnp.int32, ab.shape, 1)
    parts = _split3(g)
    cum_f = sum(_dot(tri_ref[0], part) for part in parts)
    cum_b = sum(_dot(tri_ref[1], part) for part in parts)
    both = jnp.where(lane < 40, cum_f, jnp.where(lane < 48, cum_b, jax.nn.sigmoid(ab)))
    e = _dot3(both, exp_ref[...])
    gf_ref[...] = e[:, 0 * hw:1 * hw]
    gb_ref[...] = e[:, 1 * hw:2 * hw]
    bf_ref[...] = e[:, 2 * hw:3 * hw]
    bb_ref[...] = e[:, 3 * hw:4 * hw]


def _expand_matrix():
    e = np.zeros((LANES, 4 * HEADS * HD), np.float32)
    for kind in range(2):
        for d in range(2):
            for h in range(HEADS):
                src = 32 + 16 * kind + 8 * d + h
                dst = (2 * kind + d) * HEADS * HD + h * HD
                e[src, dst:dst + HD] = 1.0
    return jnp.asarray(e, BF16)


def _chunk_tri():
    r = np.arange(TOK)
    same = (r[:, None] // CHUNK) == (r[None, :] // CHUNK)
    low = same & (r[:, None] >= r[None, :])
    up = same & (r[:, None] <= r[None, :])
    return jnp.asarray(np.stack([low, up]).astype(np.float32), BF16)


def _prep_c(y, w_conv, alog, dtb, row0, n_rows, seq):
    nb8 = TOK // 8
    b0 = row0 // TOK
    last8 = (row0 + n_rows) // 8 - 1
    qkv_blk = C_QKV // CONV_CH
    out = pl.BlockSpec((TOK, HEADS * HD), lambda i: (i, 0))
    vec = pl.BlockSpec((1, LANES), lambda i: (0, 0))
    return pl.pallas_call(
        functools.partial(_prep_c_kernel, blocks_per_seq=seq // TOK),
        grid=(n_rows // TOK,),
        in_specs=[pl.BlockSpec((TOK, CONV_CH), lambda i: (b0 + i, qkv_blk)),
                  pl.BlockSpec((8, CONV_CH), lambda i: (jnp.maximum((b0 + i) * nb8 - 1, 0), qkv_blk)),
                  pl.BlockSpec((8, CONV_CH), lambda i: (jnp.minimum((b0 + i + 1) * nb8, last8), qkv_blk)),
                  pl.BlockSpec((TOK, LANES), lambda i: (b0 + i, (C_B + 640) // LANES)),
                  pl.BlockSpec((3, CONV_CH), lambda i: (0, 0)),
                  vec, vec,
                  pl.BlockSpec((512, 512), lambda i: (0, 0)),
                  pl.BlockSpec((LANES, 4 * HEADS * HD), lambda i: (0, 0)),
                  pl.BlockSpec((2, TOK, TOK), lambda i: (0, 0, 0))],
        out_specs=[out] * 7,
        out_shape=[jax.ShapeDtypeStruct((n_rows, HEADS * HD), F32)] * 7,
        compiler_params=_cparams(("parallel",)),
        name="prep_c",
    )(y, y, y, y, w_conv, alog, dtb, _ones_blockdiag(512, HD), _expand_matrix(), _chunk_tri())


def _gdn_kernel(*refs, has_s0, n_chunks, n_par):
    ins = refs[:10]
    pos = 10
    if has_s0:
        s0_refs = refs[pos:pos + 2]
        pos += 2
    o_refs = refs[pos:pos + 2]
    s_out_refs = refs[pos + 2:pos + 4]
    s_scr = refs[pos + 4]
    c = pl.program_id(1)

    @pl.when(c == 0)
    def _():
        for d in range(2):
            s_scr[d] = s0_refs[d][...] if has_s0 else jnp.zeros(s_scr.shape[1:], F32)

    n2 = 2 * CHUNK
    ri = lax.broadcasted_iota(jnp.int32, (n2, n2), 0)
    ci = lax.broadcasted_iota(jnp.int32, (n2, n2), 1)
    same = (ri // CHUNK) == (ci // CHUNK)
    left = lax.broadcasted_iota(jnp.int32, (CHUNK, n2), 1) < CHUNK
    eye = (ri == ci).astype(F32)

    def level_mask(lvl, lower):
        bit = 1 << lvl
        hi, lo = (ri, ci) if lower else (ci, ri)
        return ((ri >> (lvl + 1)) == (ci >> (lvl + 1))) & ((hi & bit) != 0) & ((lo & bit) == 0)

    levels = [[level_mask(lvl, d == 0) for lvl in range(6)] for d in range(2)]

    def block_diag(x):
        return jnp.concatenate([jnp.where(left, x, 0.0), jnp.where(left, 0.0, x)], axis=0)

    def anti_diag(x):
        xr = pltpu.roll(x, CHUNK, 1)
        return jnp.concatenate([jnp.where(left, 0.0, xr), jnp.where(left, xr, 0.0)], axis=0)

    chains = [(d, n, p) for n in range(n_par) for d in range(2) for p in range(HEADS // 2)]
    incl = [same & ((ri >= ci) if d == 0 else (ri <= ci)) for d in range(2)]
    strict = [same & ((ri > ci) if d == 0 else (ri < ci)) for d in range(2)]

    def load(which):
        return [ins[5 * d + which][n, 0, :, p * n2:(p + 1) * n2] for d, n, p in chains]

    q, k, v, gc, beta = load(0), load(1), load(2), load(3), load(4)
    gtot = [gi[CHUNK - 1:CHUNK, :] if d == 0 else gi[0:1, :] for (d, _, _), gi in zip(chains, gc)]
    eg = [jnp.exp(gi) for gi in gc]
    kb = [ki * bi for ki, bi in zip(k, beta)]
    decay = []
    for (d, _, _), gi in zip(chains, gc):
        gcr = pltpu.roll(gi, CHUNK, 1)
        gcol = jnp.concatenate([jnp.where(left, gi, gcr), jnp.where(left, gcr, gi)], axis=0)
        diff = gcol - gcol.T
        decay.append(jnp.where(incl[d], jnp.exp(jnp.where(incl[d], diff, 0.0)), 0.0))
    k_bd = [block_diag(ki).astype(BF16) for ki in k]
    kk = [_dot_nt(block_diag(kbi).astype(BF16), kbd) for kbi, kbd in zip(kb, k_bd)]
    qk = [_dot_nt(block_diag(qi).astype(BF16), kbd) for qi, kbd in zip(q, k_bd)]
    a_mat = [jnp.where(strict[d], kki * di, 0.0) for (d, _, _), kki, di in zip(chains, kk, decay)]
    qk = [(qki * di).astype(BF16) for qki, di in zip(qk, decay)]
    x = []
    for vi, bi, kbi, egi in zip(v, beta, kb, eg):
        vb = vi * bi
        kbe_r = pltpu.roll(kbi * egi, CHUNK, 1)
        x.append(jnp.concatenate([jnp.where(left, vb, kbe_r), jnp.where(left, kbe_r, vb)], axis=0))
    corr = [-jnp.where(levels[d][0], ai, 0.0) for (d, _, _), ai in zip(chains, a_mat)]
    for lvl in range(1, 6):
        d_b = [(eye + ci_).astype(BF16) for ci_ in corr]
        t = [_dot(jnp.where(levels[d][lvl], ai, 0.0).astype(BF16), di).astype(BF16)
             for (d, _, _), ai, di in zip(chains, a_mat, d_b)]
        corr = [ci_ - _dot(di, ti) for ci_, di, ti in zip(corr, d_b, t)]
    x = [xi + _dot(ci_.astype(BF16), xi.astype(BF16)) for xi, ci_ in zip(x, corr)]
    s = [s_scr[d, n, p] for d, n, p in chains]
    s_b = [si.astype(BF16) for si in s]
    ws = [_dot(jnp.where(same, 0.0, xi).astype(BF16), sbi) for xi, sbi in zip(x, s_b)]
    v_new = [(jnp.where(same, xi, 0.0) - wi).astype(BF16) for xi, wi in zip(x, ws)]
    o_s = [_dot(anti_diag(qi * egi).astype(BF16), sbi) for qi, egi, sbi in zip(q, eg, s_b)]
    o_v = [_dot(qki, vni) for qki, vni in zip(qk, v_new)]
    kd = [anti_diag(ki * jnp.exp(gt - gi)).astype(BF16) for ki, gt, gi in zip(k, gtot, gc)]
    s_upd = [_dot_tn(kdi, vni) for kdi, vni in zip(kd, v_new)]
    for i, (d, n, p) in enumerate(chains):
        o = o_s[i] + o_v[i]
        o_refs[d][n, 0, :, p * n2:(p + 1) * n2] = o[:CHUNK] + o[CHUNK:]
        s_scr[d, n, p] = s[i] * jnp.exp(gtot[i]) + s_upd[i]

    @pl.when(c == n_chunks - 1)
    def _():
        for d in range(2):
            s_out_refs[d][...] = s_scr[d]


GDN_SEQS_PER_STEP = 2


def _gdn_scan(qn, kn, v, gcf, gcb, bf, bb, s0, n_seq, seq):
    nc = seq // CHUNK
    hw = HEADS * HD
    n2 = 2 * CHUNK
    n_par = GDN_SEQS_PER_STEP if n_seq % GDN_SEQS_PER_STEP == 0 else 1
    view = lambda a: a.reshape(n_seq, nc, CHUNK, hw)
    fwd = pl.BlockSpec((n_par, 1, CHUNK, hw), lambda b, c: (b, c, 0, 0))
    bwd = pl.BlockSpec((n_par, 1, CHUNK, hw), lambda b, c: (b, nc - 1 - c, 0, 0))
    state = pl.BlockSpec((n_par, HEADS // 2, n2, n2), lambda b, c: (b, 0, 0, 0))
    in_specs = [fwd] * 5 + [bwd] * 5
    args = [view(a) for a in (qn, kn, v, gcf, bf, qn, kn, v, gcb, bb)]
    if s0 is not None:
        in_specs += [state, state]
        args += list(s0)
    st_shape = jax.ShapeDtypeStruct((n_seq, HEADS // 2, n2, n2), F32)
    o_shape = jax.ShapeDtypeStruct((n_seq, nc, CHUNK, hw), F32)
    o_f, o_b, s_f, s_b = pl.pallas_call(
        functools.partial(_gdn_kernel, has_s0=s0 is not None, n_chunks=nc, n_par=n_par),
        grid=(n_seq // n_par, nc),
        in_specs=in_specs,
        out_specs=[fwd, bwd, state, state],
        out_shape=[o_shape, o_shape, st_shape, st_shape],
        scratch_shapes=[pltpu.VMEM((2, n_par, HEADS // 2, n2, n2), F32)],
        compiler_params=_cparams(("parallel", "arbitrary")),
        name="gdn_scan",
    )(*args)
    return o_f.reshape(n_seq * seq, hw), o_b.reshape(n_seq * seq, hw), s_f, s_b


def _state_to_pairs(s):
    b = s.shape[0]
    s = s.reshape(b, HEADS // 2, 2, HD, HD)
    z = jnp.zeros_like(s[:, :, 0])
    top = jnp.concatenate([z, s[:, :, 1]], axis=-1)
    bot = jnp.concatenate([s[:, :, 0], z], axis=-1)
    return jnp.concatenate([top, bot], axis=-2)


def _pairs_to_state(s):
    b = s.shape[0]
    even = s[:, :, HD:, :HD]
    odd = s[:, :, :HD, HD:]
    return jnp.stack([even, odd], axis=2).reshape(b, HEADS, HD, HD)


def _merge_kernel(x_ref, mod_ref, *refs, ctx_blocks):
    pairs, (z_ref, gates_ref, gain_ref, ones_ref, wa_ref, wb_ref, wc_ref, wo_ref, o_ref) = refs[:8], refs[8:]
    is_ctx = pl.program_id(0) < ctx_blocks
    ya, yb, o_f, o_b = [jnp.where(is_ctx, pairs[2 * t][...], pairs[2 * t + 1][...]) for t in range(4)]
    o = o_f + o_b
    ms = _dot3(o * o, ones_ref[...]) * (1.0 / HD)
    yc = o * lax.rsqrt(ms + EPS) * gain_ref[...] * _silu(z_ref[...])
    gates = gates_ref[...]
    merged = (jax.nn.sigmoid(gates[:, :D_MODEL]) * _dot(ya, wa_ref[0])
              + jax.nn.sigmoid(gates[:, D_MODEL:2 * D_MODEL]) * _dot(yb, wb_ref[0])
              + jax.nn.sigmoid(gates[:, 2 * D_MODEL:]) * _dot(yc.astype(BF16), wc_ref[0]))
    mix = _dot(merged.astype(BF16), wo_ref[0])
    o_ref[...] = x_ref[...] + mod_ref[0, 0][2:3] * mix


def _merge(x, mod, layer, mixers, y, gain, wa, wb, wc, wo, mod_row, ctx_rows):
    m_rows = x.shape[0]
    tm = TOK
    nb_ctx = ctx_rows // tm
    nb_lat = (m_rows - ctx_rows) // tm
    ctx_half = pl.BlockSpec((tm, 512), lambda i: (jnp.minimum(i, nb_ctx - 1), 0))
    lat_half = pl.BlockSpec((tm, 512), lambda i: (jnp.clip(i - nb_ctx, 0, nb_lat - 1), 0))
    full = pl.BlockSpec((tm, D_MODEL), lambda i: (i, 0))

    def wspec(k):
        return pl.BlockSpec((1, k, D_MODEL), lambda i: (layer, 0, 0))

    return pl.pallas_call(
        functools.partial(_merge_kernel, ctx_blocks=nb_ctx),
        grid=(m_rows // tm,),
        in_specs=[full,
                  pl.BlockSpec((1, 1, N_MOD, D_MODEL), lambda i: (layer, mod_row(i, tm), 0, 0))]
        + [ctx_half, lat_half] * 4
        + [pl.BlockSpec((tm, 512), lambda i: (i, C_Z // 512)),
           pl.BlockSpec((tm, 3 * D_MODEL), lambda i: (i, 0)),
           pl.BlockSpec((1, 512), lambda i: (0, 0)),
           pl.BlockSpec((512, 512), lambda i: (0, 0)),
           wspec(512), wspec(512), wspec(512), wspec(D_MODEL)],
        out_specs=full,
        out_shape=jax.ShapeDtypeStruct((m_rows, D_MODEL), F32),
        compiler_params=_cparams(("parallel",)),
        name="merge_out",
    )(x, mod, *mixers, y, y, gain, _ones_blockdiag(512, HD), wa, wb, wc, wo)


def _ffn_kernel(x_ref, mod_ref, g_ref, wg_ref, wu_ref, wd_ref, fin_ref, o_ref, *, final):
    x = x_ref[...]
    m = mod_ref[0, 0]
    h = (_rms_rows(x, g_ref[...]) * (1.0 + m[4:5]) + m[3:4]).astype(BF16)
    act = (_silu(_dot(h, wg_ref[0])) * _dot(h, wu_ref[0])).astype(BF16)
    out = x + m[5:6] * _dot(act, wd_ref[0])
    if final:
        out = _rms_rows(out, fin_ref[...])
    o_ref[...] = out


def _ffn(x, mod, layer, gain, wg, wu, wd, fin, mod_row, final):
    m_rows = x.shape[0]
    tm = TOK
    full = pl.BlockSpec((tm, D_MODEL), lambda i: (i, 0))
    vec = pl.BlockSpec((1, D_MODEL), lambda i: (0, 0))
    once = pl.Buffered(1)
    return pl.pallas_call(
        functools.partial(_ffn_kernel, final=final),
        grid=(m_rows // tm,),
        in_specs=[full,
                  pl.BlockSpec((1, 1, N_MOD, D_MODEL), lambda i: (layer, mod_row(i, tm), 0, 0)),
                  vec,
                  pl.BlockSpec((1, D_MODEL, D_FF), lambda i: (layer, 0, 0), pipeline_mode=once),
                  pl.BlockSpec((1, D_MODEL, D_FF), lambda i: (layer, 0, 0), pipeline_mode=once),
                  pl.BlockSpec((1, D_FF, D_MODEL), lambda i: (layer, 0, 0), pipeline_mode=once),
                  vec],
        out_specs=full,
        out_shape=jax.ShapeDtypeStruct((m_rows, D_MODEL), F32),
        compiler_params=_cparams(("parallel",)),
        name="ffn",
    )(x, mod, gain, wg, wu, wd, fin)


def _layout_w_in(w_in):
    def cols(a, b):
        return w_in[:, :, a:b]

    pad = jnp.zeros(w_in.shape[:2] + (64,), w_in.dtype)
    parts = [cols(3520, 6592), cols(1440, 2976), cols(768, 1152), cols(1152, 1408), cols(1408, 1440),
             cols(3488, 3504), cols(3504, 3520), pad, cols(512, 640), cols(640, 768), cols(2976, 3488),
             cols(0, 512)]
    return jnp.concatenate(parts, axis=-1).astype(BF16)


def _layout_w_uq(w_uq):
    n_layers = w_uq.shape[0]
    w = w_uq.reshape(n_layers, Q_LORA, HEADS, NOPE_B + ROPE_B)
    w = jnp.pad(w, ((0, 0), (0, 0), (0, 0), (0, LANES - NOPE_B - ROPE_B)))
    return w.reshape(n_layers, Q_LORA, HEADS * LANES).astype(BF16)


def _layout_w_ukv(w_ukv):
    n_layers = w_ukv.shape[0]
    w = w_ukv.reshape(n_layers, KV_LORA, HEADS, 2, HD)
    zero = jnp.zeros((n_layers, KV_LORA, HEADS, HD), w.dtype)
    wk_top = jnp.concatenate([w[:, :, :, 0], zero], axis=-1).reshape(n_layers, KV_LORA, HEADS * LANES)
    place = np.zeros((LANES, HEADS, LANES), np.float32)
    for d in range(ROPE_B):
        place[d, :, NOPE_B + d] = 1.0
    place = jnp.broadcast_to(jnp.asarray(place.reshape(LANES, HEADS * LANES)), (n_layers, LANES, HEADS * LANES))
    wk = jnp.concatenate([wk_top, place], axis=1).astype(BF16)
    wv = w[:, :, :, 1].reshape(n_layers, KV_LORA, HEADS // 2, 2, HD)
    zero = jnp.zeros((n_layers, KV_LORA, HEADS // 2, HD), w.dtype)
    even = jnp.concatenate([wv[:, :, :, 0], zero], axis=-1)
    odd = jnp.concatenate([zero, wv[:, :, :, 1]], axis=-1)
    wv = jnp.stack([even, odd], axis=3).reshape(n_layers, KV_LORA, HEADS * LANES).astype(BF16)
    return wk, wv


def kernel(x_prompt, x_sample, cache_ka, cache_va, cache_ckv, cache_kpe, state_fwd, state_bwd, c, c_ctx,
           w_mod, b_mod, norm1, norm2, w_in, a_qnorm, a_knorm, b_qnorm, b_kvnorm, w_uq, w_ukv, c_conv,
           c_alog, c_dt_bias, c_onorm, w_pa, w_pb, w_pc, w_out, w_gate, w_up, w_down, final_norm):
    n_ctx, s_ctx, _ = x_prompt.shape
    n_lat, s_lat, _ = x_sample.shape
    n_layers = w_mod.shape[0]
    past = cache_ka.shape[2]
    ctx_rows = n_ctx * s_ctx
    lat_rows = n_lat * s_lat
    assert s_ctx == TOK and s_lat % TOK == 0 and ctx_rows % 512 == 0 and s_lat % 512 == 0
    assert past % TOK == 0 and n_lat + 1 <= 8 and ctx_rows % s_lat == 0

    def mod_row(i, tm):
        return jnp.where(i < ctx_rows // tm, 0, 1 + (i - ctx_rows // tm) // (s_lat // tm))

    def tab_row(i):
        nb = ctx_rows // TOK
        return jnp.where(i < nb, 0, 1 + (i - nb) % (s_lat // TOK))

    tm_in = 1024 if ctx_rows % 1024 == 0 and s_lat % 1024 == 0 else 512

    x = jnp.concatenate([x_prompt.reshape(ctx_rows, D_MODEL), x_sample.reshape(lat_rows, D_MODEL)], axis=0)
    cvec = jnp.zeros((8, D_MODEL), F32).at[0].set(c_ctx).at[1:1 + n_lat].set(c)
    mod = _modulation(cvec, w_mod, b_mod).reshape(n_layers, 8, N_MOD, D_MODEL)
    cos, sin = _rope_tables(s_lat)

    w_in_p = _layout_w_in(w_in)
    w_uq_p = _layout_w_uq(w_uq)
    w_k_p, w_v_p = _layout_w_ukv(w_ukv)
    bf = lambda w: w.astype(BF16)
    w_pa_b, w_pb_b, w_pc_b, w_out_b = bf(w_pa), bf(w_pb), bf(w_pc), bf(w_out)
    w_gate_b, w_up_b, w_down_b = bf(w_gate), bf(w_up), bf(w_down)

    cache_rows = n_lat * n_layers * past
    ck4, cv4 = _cache_a(cache_ka.reshape(cache_rows, LANES), cache_va.reshape(cache_rows, LANES))
    kpe_pad = jnp.pad(cache_kpe.reshape(cache_rows, ROPE_B), ((0, 0), (0, LANES - ROPE_B)))
    ckb, cvb = _cache_b(cache_ckv.reshape(cache_rows, KV_LORA), kpe_pad, w_k_p, w_v_p,
                        lambda i: (i // (past // TOK)) % n_layers)

    lane_vec = lambda v16: jnp.zeros((1, LANES), F32).at[0, 32:48].set(v16.reshape(16))
    new = []
    for l in range(n_layers):
        y = _in_proj(x, mod, l, norm1[l][None], w_in_p[l], mod_row, tm_in)

        qa, k4, v4, kn_a = _prep_a(y, cos, sin, jnp.tile(a_qnorm[l], HEADS)[None],
                                   jnp.tile(a_knorm[l], KV_HEADS_A)[None], tab_row)
        qb, kb, vb, ckvn = _prep_b(y, cos, sin, b_qnorm[l][None], b_kvnorm[l][None],
                                   w_uq_p, w_k_p, w_v_p, l, tab_row)
        conv_args = (c_conv[l, :, 0, :], lane_vec(c_alog[l]), lane_vec(c_dt_bias[l]))
        gdn_ctx = _prep_c(y, *conv_args, 0, ctx_rows, s_ctx)
        gdn_lat = _prep_c(y, *conv_args, ctx_rows, lat_rows, s_lat)

        cache_blk = (past, lambda b, l=l: b * n_layers + l)
        ya_c = _attention(qa, k4, v4, None, GROUPS_A, 0, n_ctx, s_ctx, s_ctx, None)
        ya_l = _attention(qa, k4, v4, (ck4, cv4), GROUPS_A, ctx_rows, n_lat, s_lat, 256, cache_blk)
        yb_c = _attention(qb, kb, vb, None, GROUPS_B, 0, n_ctx, s_ctx, s_ctx, None)
        yb_l = _attention(qb, kb, vb, (ckb, cvb), GROUPS_B, ctx_rows, n_lat, s_lat, 256, cache_blk)

        of_c, ob_c, sf_c, sb_c = _gdn_scan(*gdn_ctx, None, n_ctx, s_ctx)
        s0 = (_state_to_pairs(state_fwd[:, l]), _state_to_pairs(state_bwd[:, l]))
        of_l, ob_l, _, _ = _gdn_scan(*gdn_lat, s0, n_lat, s_lat)

        x = _merge(x, mod, l, (ya_c, ya_l, yb_c, yb_l, of_c, of_l, ob_c, ob_l), y,
                   jnp.tile(c_onorm[l], HEADS)[None], w_pa_b, w_pb_b, w_pc_b, w_out_b, mod_row, ctx_rows)
        x = _ffn(x, mod, l, norm2[l][None], w_gate_b, w_up_b, w_down_b, final_norm[None], mod_row,
                 final=(l == n_layers - 1))

        new.append((kn_a[:ctx_rows].reshape(n_ctx, s_ctx, KV_HEADS_A, HD),
                    y[:ctx_rows, C_VA:C_VA + LANES].reshape(n_ctx, s_ctx, KV_HEADS_A, HD),
                    ckvn[:ctx_rows].reshape(n_ctx, s_ctx, KV_LORA),
                    y[:ctx_rows, C_B + 640:C_B + 640 + ROPE_B].reshape(n_ctx, s_ctx, ROPE_B),
                    _pairs_to_state(sf_c), _pairs_to_state(sb_c)))

    y_prompt = x[:ctx_rows].reshape(n_ctx, s_ctx, D_MODEL)
    y_sample = x[ctx_rows:].reshape(n_lat, s_lat, D_MODEL)
    stacked = [jnp.stack([n[i] for n in new], axis=1) for i in range(6)]
    return (y_prompt, y_sample) + tuple(stacked)
```

```python
import functools

import numpy as np
import jax
import jax.numpy as jnp
from jax import lax
from jax.experimental import pallas as pl
from jax.experimental.pallas import tpu as pltpu

F32 = jnp.float32
BF16 = jnp.bfloat16

D_MODEL = 1024
GRID_W = 64
ROPE_THETA = 10000.0
EPS = 1e-6
N_MOD = 6
HEADS = 8
KV_HEADS_A = 2
HD = 64
Q_LORA = 384
KV_LORA = 256
NOPE_B = 64
ROPE_B = 32
CHUNK = 64
CONV_CH = 3 * HEADS * HD
D_FF = 2816

C_GATES = 0
C_QKV = 3072
C_B = 4608
C_KA = 5376
C_VA = 5504
C_Z = 5632
C_QA = 6144
N_IN = 6656

LOG2E = 1.4426950408889634
LANES = 128
TOK = 256
VMEM_LIMIT = 56 * 1024 * 1024


def _cparams(sem):
    return pltpu.CompilerParams(dimension_semantics=sem, vmem_limit_bytes=VMEM_LIMIT)


def _dot(a, b):
    return jnp.dot(a, b, preferred_element_type=F32)


def _dot_nt(a, b):
    return lax.dot_general(a, b, (((1,), (1,)), ((), ())), preferred_element_type=F32)


def _dot_tn(a, b):
    return lax.dot_general(a, b, (((0,), (0,)), ((), ())), preferred_element_type=F32)


def _split3(x):
    hi = x.astype(BF16)
    r = x - hi.astype(F32)
    mid = r.astype(BF16)
    lo = (r - mid.astype(F32)).astype(BF16)
    return hi, mid, lo


def _dot3(x, w):
    hi, mid, lo = _split3(x)
    return _dot(hi, w) + _dot(mid, w) + _dot(lo, w)


def _group_sums(x, ones):
    w = min(x.shape[1], ones.shape[0])
    blk = ones[:w, :w]
    slabs = []
    for s in range(x.shape[1] // w):
        xs = x[:, s * w:(s + 1) * w]
        hi = xs.astype(BF16)
        lo = (xs - hi.astype(F32)).astype(BF16)
        slabs.append(_dot(hi, blk) + _dot(lo, blk))
    return slabs[0] if len(slabs) == 1 else jnp.concatenate(slabs, axis=1)


def _silu(x):
    return x * jax.nn.sigmoid(x)


def _rms_rows(x, g):
    return x * lax.rsqrt(jnp.mean(x * x, axis=-1, keepdims=True) + EPS) * g


def _swap_pairs(x, half):
    lane = lax.broadcasted_iota(jnp.int32, x.shape, 1)
    n = x.shape[1]
    return jnp.where((lane & half) == 0, pltpu.roll(x, n - half, 1), pltpu.roll(x, half, 1))


def _rope_cols(x, cos, sin, half):
    outs = []
    for s in range(x.shape[1] // LANES):
        xs = x[:, s * LANES:(s + 1) * LANES]
        outs.append(xs * cos + _swap_pairs(xs, half) * sin)
    return outs[0] if len(outs) == 1 else jnp.concatenate(outs, axis=1)


def _mod_kernel(c_ref, w_ref, b_ref, o_ref):
    s = _silu(c_ref[...]).astype(BF16)
    o_ref[0] = _dot(s, w_ref[0].astype(BF16)) + b_ref[0]


def _modulation(cvec, w_mod, b_mod):
    n_layers = w_mod.shape[0]
    n_out = w_mod.shape[2]
    tn = 1536
    return pl.pallas_call(
        _mod_kernel,
        grid=(n_layers, n_out // tn),
        in_specs=[pl.BlockSpec((8, D_MODEL), lambda l, j: (0, 0)),
                  pl.BlockSpec((1, D_MODEL, tn), lambda l, j: (l, 0, j)),
                  pl.BlockSpec((1, 1, tn), lambda l, j: (l, 0, j))],
        out_specs=pl.BlockSpec((1, 8, tn), lambda l, j: (l, 0, j)),
        out_shape=jax.ShapeDtypeStruct((n_layers, 8, n_out), F32),
        compiler_params=_cparams(("parallel", "parallel")),
        name="modulation",
    )(cvec, w_mod, b_mod.reshape(n_layers, 1, n_out))


def _in_proj_kernel(x_ref, mod_ref, g_ref, w_ref, o_ref, h_ref):
    @pl.when(pl.program_id(1) == 0)
    def _():
        m = mod_ref[0, 0]
        y = _rms_rows(x_ref[...], g_ref[...])
        h_ref[...] = (y * (1.0 + m[1:2]) + m[0:1]).astype(BF16)

    o_ref[...] = _dot(h_ref[...], w_ref[0])


def _in_proj(x, mod, layer, gain, w, mod_row, tm):
    m_rows = x.shape[0]
    tn = N_IN // 4
    return pl.pallas_call(
        _in_proj_kernel,
        grid=(m_rows // tm, N_IN // tn),
        in_specs=[pl.BlockSpec((tm, D_MODEL), lambda i, j: (i, 0)),
                  pl.BlockSpec((1, 1, N_MOD, D_MODEL), lambda i, j: (layer, mod_row(i, tm), 0, 0)),
                  pl.BlockSpec((1, D_MODEL), lambda i, j: (0, 0)),
                  pl.BlockSpec((1, D_MODEL, tn), lambda i, j: (layer, 0, j))],
        out_specs=pl.BlockSpec((tm, tn), lambda i, j: (i, j)),
        out_shape=jax.ShapeDtypeStruct((m_rows, N_IN), F32),
        scratch_shapes=[pltpu.VMEM((tm, D_MODEL), BF16)],
        compiler_params=_cparams(("parallel", "arbitrary")),
        name="in_proj",
    )(x, mod, gain, w)


def _rope_table_kernel(ang_ref, sign_ref, cos_ref, sin_ref):
    a = ang_ref[...]
    cos_ref[...] = jnp.cos(a)
    sin_ref[...] = jnp.sin(a) * sign_ref[...]


def _rope_tables(n_lat):
    t = np.arange(n_lat)
    row = jnp.asarray((t // GRID_W).astype(np.float32))[:, None]
    col = jnp.asarray((t % GRID_W).astype(np.float32))[:, None]
    lane = np.arange(LANES)

    def table(nf, lo, width):
        inv_freq = ROPE_THETA ** (-jnp.arange(nf, dtype=F32) / nf)
        d = lane - lo
        active = (d >= 0) & (d < width)
        dd = np.where(active, d, 0)
        f_idx = dd % nf
        use_col = dd >= 2 * nf
        freq = jnp.where(jnp.asarray(active), inv_freq[f_idx], 0.0)[None, :]
        ang = jnp.where(jnp.asarray(use_col)[None, :], col, row) * freq
        sign = np.where((dd % (2 * nf)) < nf, -1.0, 1.0).astype(np.float32)
        return ang, sign

    period_a = HD
    lane_a = lane % period_a
    nf_a = HD // 4
    inv_a = ROPE_THETA ** (-jnp.arange(nf_a, dtype=F32) / nf_a)
    ang_a = jnp.where(jnp.asarray(lane_a >= 2 * nf_a)[None, :], col, row) * inv_a[lane_a % nf_a][None, :]
    sign_a = np.where((lane_a % (2 * nf_a)) < nf_a, -1.0, 1.0).astype(np.float32)
    ang_q, sign_q = table(ROPE_B // 4, NOPE_B, ROPE_B)
    ang_k, sign_k = table(ROPE_B // 4, 0, ROPE_B)
    ang = jnp.stack([ang_a, ang_q, ang_k])
    ang = jnp.concatenate([jnp.zeros((3, TOK, LANES), F32), ang], axis=1)
    sign = jnp.asarray(np.stack([sign_a, sign_q, sign_k]))[:, None, :]
    rows = TOK + n_lat
    spec = pl.BlockSpec((1, TOK, LANES), lambda a, i: (a, i, 0))
    return pl.pallas_call(
        _rope_table_kernel,
        grid=(3, rows // TOK),
        in_specs=[spec, pl.BlockSpec((1, 1, LANES), lambda a, i: (a, 0, 0))],
        out_specs=[spec, spec],
        out_shape=[jax.ShapeDtypeStruct((3, rows, LANES), F32)] * 2,
        compiler_params=_cparams(("parallel", "parallel")),
        name="rope_tables",
    )(ang, sign)


def _prep_a_kernel(qa_ref, kv_ref, cos_ref, sin_ref, gq_ref, gk_ref, ones_ref, place_ref,
                   q_ref, k4_ref, vt_ref, kn_ref):
    cos = cos_ref[0]
    sin = sin_ref[0]
    ones = ones_ref[...]
    q = qa_ref[...]
    qn = q * lax.rsqrt(_group_sums(q * q, ones) * (1.0 / HD) + EPS) * gq_ref[...]
    q_ref[...] = (_rope_cols(qn, cos, sin, HD // 4) * (HD ** -0.5 * LOG2E)).astype(BF16)
    k = kv_ref[:, :LANES]
    kn = k * lax.rsqrt(_group_sums(k * k, ones) * (1.0 / HD) + EPS) * gk_ref[...]
    kn_ref[...] = kn
    k4_ref[...] = _dot(_rope_cols(kn, cos, sin, HD // 4).astype(BF16), place_ref[...]).astype(BF16)
    vt_ref[...] = kv_ref[:, LANES:].T.astype(BF16)


def _place4():
    p = np.zeros((LANES, 4 * LANES), np.float32)
    for kv in range(KV_HEADS_A):
        for side in range(2):
            for d in range(HD):
                p[kv * HD + d, (2 * kv + side) * LANES + side * HD + d] = 1.0
    return jnp.asarray(p, BF16)


def _ones_blockdiag(n, group):
    idx = np.arange(n) // group
    return jnp.asarray((idx[:, None] == idx[None, :]).astype(np.float32), BF16)


def _prep_a(y, cos, sin, gq, gk, tab_row):
    m_rows = y.shape[0]
    wide = pl.BlockSpec((TOK, 4 * LANES), lambda i: (i, 0))
    tab = pl.BlockSpec((1, TOK, LANES), lambda i: (0, tab_row(i), 0))
    return pl.pallas_call(
        _prep_a_kernel,
        grid=(m_rows // TOK,),
        in_specs=[pl.BlockSpec((TOK, 512), lambda i: (i, C_QA // 512)),
                  pl.BlockSpec((TOK, 256), lambda i: (i, C_KA // 256)),
                  tab, tab,
                  pl.BlockSpec((1, 512), lambda i: (0, 0)),
                  pl.BlockSpec((1, LANES), lambda i: (0, 0)),
                  pl.BlockSpec((256, 256), lambda i: (0, 0)),
                  pl.BlockSpec((LANES, 512), lambda i: (0, 0))],
        out_specs=[wide, wide, pl.BlockSpec((LANES, TOK), lambda i: (0, i)),
                   pl.BlockSpec((TOK, LANES), lambda i: (i, 0))],
        out_shape=[jax.ShapeDtypeStruct((m_rows, 512), BF16)] * 2
        + [jax.ShapeDtypeStruct((LANES, m_rows), BF16), jax.ShapeDtypeStruct((m_rows, LANES), F32)],
        compiler_params=_cparams(("parallel",)),
        name="prep_a",
    )(y, y, cos, sin, gq, gk, _ones_blockdiag(256, HD), _place4())


def _cache_a_kernel(k_ref, v_ref, place_ref, k4_ref, vt_ref):
    k4_ref[...] = _dot(k_ref[...].astype(BF16), place_ref[...]).astype(BF16)
    vt_ref[...] = v_ref[...].T.astype(BF16)


def _cache_a(ck, cv):
    rows = ck.shape[0]
    narrow = pl.BlockSpec((TOK, LANES), lambda i: (i, 0))
    return pl.pallas_call(
        _cache_a_kernel,
        grid=(rows // TOK,),
        in_specs=[narrow, narrow, pl.BlockSpec((LANES, 512), lambda i: (0, 0))],
        out_specs=[pl.BlockSpec((TOK, 512), lambda i: (i, 0)), pl.BlockSpec((LANES, TOK), lambda i: (0, i))],
        out_shape=[jax.ShapeDtypeStruct((rows, 512), BF16), jax.ShapeDtypeStruct((LANES, rows), BF16)],
        compiler_params=_cparams(("parallel",)),
        name="cache_a",
    )(ck, cv, _place4())


def _prep_b_kernel(g_ref, cq_ref, sq_ref, ck_ref, sk_ref, gq_ref, gkv_ref, wq_ref, wk_ref, wv_ref,
                   q_ref, k_ref, vt_ref, ckv_ref):
    g = g_ref[...]
    cqn = _rms_rows(g[:, :Q_LORA], gq_ref[...])
    q = _dot(cqn.astype(BF16), wq_ref[0])
    q = _rope_cols(q, cq_ref[0], sq_ref[0], ROPE_B // 4) * ((NOPE_B + ROPE_B) ** -0.5 * LOG2E)
    q_ref[...] = q.astype(BF16)
    ckvn = _rms_rows(g[:, Q_LORA:Q_LORA + KV_LORA], gkv_ref[...])
    ckv_ref[...] = ckvn
    kpe = _rope_cols(g[:, Q_LORA + KV_LORA:], ck_ref[0], sk_ref[0], ROPE_B // 4)
    ckv_b = ckvn.astype(BF16)
    kin = jnp.concatenate([ckv_b, kpe.astype(BF16)], axis=1)
    k_ref[...] = _dot(kin, wk_ref[0]).astype(BF16)
    vt_ref[...] = _dot(ckv_b, wv_ref[0]).T.astype(BF16)


def _prep_b(y, cos, sin, gq, gkv, wq, wk, wv, layer, tab_row):
    m_rows = y.shape[0]
    wide = pl.BlockSpec((TOK, HEADS * LANES), lambda i: (i, 0))

    def tab(a):
        return pl.BlockSpec((1, TOK, LANES), lambda i: (a, tab_row(i), 0))

    def wspec(k, n):
        return pl.BlockSpec((1, k, n), lambda i: (layer, 0, 0))

    return pl.pallas_call(
        _prep_b_kernel,
        grid=(m_rows // TOK,),
        in_specs=[pl.BlockSpec((TOK, 768), lambda i: (i, C_B // 768)),
                  tab(1), tab(1), tab(2), tab(2),
                  pl.BlockSpec((1, Q_LORA), lambda i: (0, 0)),
                  pl.BlockSpec((1, KV_LORA), lambda i: (0, 0)),
                  wspec(Q_LORA, HEADS * LANES), wspec(KV_LORA + LANES, HEADS * LANES),
                  wspec(KV_LORA, HEADS * HD)],
        out_specs=[wide, wide, pl.BlockSpec((HEADS * HD, TOK), lambda i: (0, i)),
                   pl.BlockSpec((TOK, KV_LORA), lambda i: (i, 0))],
        out_shape=[jax.ShapeDtypeStruct((m_rows, HEADS * LANES), BF16)] * 2
        + [jax.ShapeDtypeStruct((HEADS * HD, m_rows), BF16), jax.ShapeDtypeStruct((m_rows, KV_LORA), F32)],
        compiler_params=_cparams(("parallel",)),
        name="prep_b",
    )(y, cos, sin, cos, sin, gq, gkv, wq, wk, wv)


def _cache_b_kernel(c_ref, p_ref, wk_ref, wv_ref, k_ref, vt_ref):
    ckv_b = c_ref[...].astype(BF16)
    kin = jnp.concatenate([ckv_b, p_ref[...].astype(BF16)], axis=1)
    k_ref[...] = _dot(kin, wk_ref[0]).astype(BF16)
    vt_ref[...] = _dot(ckv_b, wv_ref[0]).T.astype(BF16)


def _cache_b(ckv, kpe, wk, wv, layer_of_block):
    rows = ckv.shape[0]
    return pl.pallas_call(
        _cache_b_kernel,
        grid=(rows // TOK,),
        in_specs=[pl.BlockSpec((TOK, KV_LORA), lambda i: (i, 0)),
                  pl.BlockSpec((TOK, LANES), lambda i: (i, 0)),
                  pl.BlockSpec((1, KV_LORA + LANES, HEADS * LANES), lambda i: (layer_of_block(i), 0, 0)),
                  pl.BlockSpec((1, KV_LORA, HEADS * HD), lambda i: (layer_of_block(i), 0, 0))],
        out_specs=[pl.BlockSpec((TOK, HEADS * LANES), lambda i: (i, 0)),
                   pl.BlockSpec((HEADS * HD, TOK), lambda i: (0, i))],
        out_shape=[jax.ShapeDtypeStruct((rows, HEADS * LANES), BF16),
                   jax.ShapeDtypeStruct((HEADS * HD, rows), BF16)],
        compiler_params=_cparams(("parallel",)),
        name="cache_b",
    )(ckv, kpe, wk, wv)


ATT_KEY_TILE = 256
ATT_LOOKAHEAD = 5


def _attn_kernel(*refs, groups, has_cache):
    if has_cache:
        q_ref, kc_ref, vc_ref, kn_ref, vn_ref, o_ref = refs
        segments = [(kc_ref, vc_ref), (kn_ref, vn_ref)]
    else:
        q_ref, kn_ref, vn_ref, o_ref = refs
        segments = [(kn_ref, vn_ref)]
    tiles = [(k_ref, v_ref, t * ATT_KEY_TILE) for k_ref, v_ref in segments
             for t in range(k_ref.shape[0] // ATT_KEY_TILE)]
    q = [q_ref[:, qg * LANES:(qg + 1) * LANES] for qg, _, _ in groups]

    def scores(h, i):
        k_ref, _, t0 = tiles[i]
        kg = groups[h][1]
        return _dot_nt(k_ref[t0:t0 + ATT_KEY_TILE, kg * LANES:(kg + 1) * LANES], q[h])

    heads = range(len(groups))
    jobs = [(i, h) for i in range(len(tiles)) for h in heads]
    pending = {j: scores(h, i) for j, (i, h) in enumerate(jobs[:ATT_LOOKAHEAD])}
    ones_rows = (lax.broadcasted_iota(jnp.int32, (16, ATT_KEY_TILE), 0) == 0).astype(BF16)
    m, acc = {}, {}
    for j, (i, h) in enumerate(jobs):
        if j + ATT_LOOKAHEAD < len(jobs):
            i_n, h_n = jobs[j + ATT_LOOKAHEAD]
            pending[j + ATT_LOOKAHEAD] = scores(h_n, i_n)
        s = pending.pop(j)
        _, v_ref, t0 = tiles[i]
        vrow = groups[h][2]
        v_t = jnp.concatenate([v_ref[vrow:vrow + HD, t0:t0 + ATT_KEY_TILE], ones_rows], axis=0)
        m_tile = jnp.max(s, axis=0, keepdims=True)
        if i == 0:
            m[h] = m_tile
            acc[h] = _dot(v_t, jnp.exp2(s - m_tile).astype(BF16))
        else:
            m_new = jnp.maximum(m[h], m_tile)
            alpha = jnp.exp2(m[h] - m_new)
            acc[h] = acc[h] * alpha + _dot(v_t, jnp.exp2(s - m_new).astype(BF16))
            m[h] = m_new
    outs = [acc[h][:HD] / acc[h][HD:HD + 1] for h in heads]
    o_ref[...] = jnp.concatenate(outs, axis=0).T.astype(BF16)


def _attention(q, kn, vn, cache, groups, row0, n_seq, seq, tq, cache_block):
    qw, kw, vr = q.shape[1], kn.shape[1], vn.shape[0]
    nq = seq // tq
    qb0, sb0 = row0 // tq, row0 // seq
    in_specs = [pl.BlockSpec((tq, qw), lambda b, i: (qb0 + b * nq + i, 0))]
    args = [q]
    if cache is not None:
        past = cache_block[0]
        in_specs += [pl.BlockSpec((past, kw), lambda b, i: (cache_block[1](b), 0)),
                     pl.BlockSpec((vr, past), lambda b, i: (0, cache_block[1](b)))]
        args += list(cache)
    in_specs += [pl.BlockSpec((seq, kw), lambda b, i: (sb0 + b, 0)),
                 pl.BlockSpec((vr, seq), lambda b, i: (0, sb0 + b))]
    args += [kn, vn]
    return pl.pallas_call(
        functools.partial(_attn_kernel, groups=groups, has_cache=cache is not None),
        grid=(n_seq, nq),
        in_specs=in_specs,
        out_specs=pl.BlockSpec((tq, 512), lambda b, i: (b * nq + i, 0)),
        out_shape=jax.ShapeDtypeStruct((n_seq * seq, 512), BF16),
        compiler_params=_cparams(("parallel", "parallel")),
        name="attention",
    )(*args)


GROUPS_A = tuple((h // 2, 2 * (h // 4) + (h % 2), (h // 4) * HD) for h in range(HEADS))
GROUPS_B = tuple((h, h, h * HD) for h in range(HEADS))


def _prep_c_kernel(x_ref, prev_ref, next_ref, ab_ref, w_ref, alog_ref, dtb_ref, ones_ref, exp_ref, tri_ref,
                   q_ref, k_ref, v_ref, gf_ref, gb_ref, bf_ref, bb_ref, *, blocks_per_seq):
    j = pl.program_id(0) % blocks_per_seq
    is_first, is_last = j == 0, j == blocks_per_seq - 1
    x = x_ref[...]
    rows = x.shape[0]
    rid = lax.broadcasted_iota(jnp.int32, x.shape, 0)
    halo_p = jnp.where(is_first, 0.0, prev_ref[7:8, :])
    halo_n = jnp.where(is_last, 0.0, next_ref[0:1, :])
    x_prev = jnp.where(rid == 0, halo_p, pltpu.roll(x, 1, 0))
    x_next = jnp.where(rid == rows - 1, halo_n, pltpu.roll(x, rows - 1, 0))
    w = w_ref[...]
    s = _silu(x_prev * w[0:1] + x * w[1:2] + x_next * w[2:3])
    ones = ones_ref[...]
    hw = HEADS * HD
    q, k = s[:, :hw], s[:, hw:2 * hw]
    q_ref[...] = q * lax.rsqrt(_group_sums(q * q, ones) + EPS) * (HD ** -0.5)
    k_ref[...] = k * lax.rsqrt(_group_sums(k * k, ones) + EPS)
    v_ref[...] = s[:, 2 * hw:]
    ab = ab_ref[...]
    t = ab + dtb_ref[...]
    softplus = jnp.maximum(t, 0.0) + jnp.log1p(jnp.exp(-jnp.abs(t)))
    g = -jnp.exp(alog_ref[...]) * softplus
    lane = lax.broadcasted_iota(jnp.int32, ab.shape, 1)
    parts = _split3(g)
    cum_f = sum(_dot(tri_ref[0], part) for part in parts)
    cum_b = sum(_dot(tri_ref[1], part) for part in parts)
    both = jnp.where(lane < 40, cum_f, jnp.where(lane < 48, cum_b, jax.nn.sigmoid(ab)))
    e = _dot3(both, exp_ref[...])
    gf_ref[...] = e[:, 0 * hw:1 * hw]
    gb_ref[...] = e[:, 1 * hw:2 * hw]
    bf_ref[...] = e[:, 2 * hw:3 * hw]
    bb_ref[...] = e[:, 3 * hw:4 * hw]


def _expand_matrix():
    e = np.zeros((LANES, 4 * HEADS * HD), np.float32)
    for kind in range(2):
        for d in range(2):
            for h in range(HEADS):
                src = 32 + 16 * kind + 8 * d + h
                dst = (2 * kind + d) * HEADS * HD + h * HD
                e[src, dst:dst + HD] = 1.0
    return jnp.asarray(e, BF16)


def _chunk_tri():
    r = np.arange(TOK)
    same = (r[:, None] // CHUNK) == (r[None, :] // CHUNK)
    low = same & (r[:, None] >= r[None, :])
    up = same & (r[:, None] <= r[None, :])
    return jnp.asarray(np.stack([low, up]).astype(np.float32), BF16)


def _prep_c(y, w_conv, alog, dtb, row0, n_rows, seq):
    nb8 = TOK // 8
    b0 = row0 // TOK
    last8 = (row0 + n_rows) // 8 - 1
    qkv_blk = C_QKV // CONV_CH
    out = pl.BlockSpec((TOK, HEADS * HD), lambda i: (i, 0))
    vec = pl.BlockSpec((1, LANES), lambda i: (0, 0))
    return pl.pallas_call(
        functools.partial(_prep_c_kernel, blocks_per_seq=seq // TOK),
        grid=(n_rows // TOK,),
        in_specs=[pl.BlockSpec((TOK, CONV_CH), lambda i: (b0 + i, qkv_blk)),
                  pl.BlockSpec((8, CONV_CH), lambda i: (jnp.maximum((b0 + i) * nb8 - 1, 0), qkv_blk)),
                  pl.BlockSpec((8, CONV_CH), lambda i: (jnp.minimum((b0 + i + 1) * nb8, last8), qkv_blk)),
                  pl.BlockSpec((TOK, LANES), lambda i: (b0 + i, (C_B + 640) // LANES)),
                  pl.BlockSpec((3, CONV_CH), lambda i: (0, 0)),
                  vec, vec,
                  pl.BlockSpec((256, 256), lambda i: (0, 0)),
                  pl.BlockSpec((LANES, 4 * HEADS * HD), lambda i: (0, 0)),
                  pl.BlockSpec((2, TOK, TOK), lambda i: (0, 0, 0))],
        out_specs=[out] * 7,
        out_shape=[jax.ShapeDtypeStruct((n_rows, HEADS * HD), F32)] * 7,
        compiler_params=_cparams(("parallel",)),
        name="prep_c",
    )(y, y, y, y, w_conv, alog, dtb, _ones_blockdiag(256, HD), _expand_matrix(), _chunk_tri())


def _gdn_kernel(*refs, has_s0, n_chunks, n_par):
    ins = refs[:10]
    pos = 10
    if has_s0:
        s0_refs = refs[pos:pos + 2]
        pos += 2
    o_refs = refs[pos:pos + 2]
    s_out_refs = refs[pos + 2:pos + 4]
    s_scr = refs[pos + 4]
    c = pl.program_id(1)

    @pl.when(c == 0)
    def _():
        for d in range(2):
            s_scr[d] = s0_refs[d][...] if has_s0 else jnp.zeros(s_scr.shape[1:], F32)

    n2 = 2 * CHUNK
    ri = lax.broadcasted_iota(jnp.int32, (n2, n2), 0)
    ci = lax.broadcasted_iota(jnp.int32, (n2, n2), 1)
    same = (ri // CHUNK) == (ci // CHUNK)
    left = lax.broadcasted_iota(jnp.int32, (CHUNK, n2), 1) < CHUNK
    eye = (ri == ci).astype(F32)

    def level_mask(lvl, lower):
        bit = 1 << lvl
        hi, lo = (ri, ci) if lower else (ci, ri)
        return ((ri >> (lvl + 1)) == (ci >> (lvl + 1))) & ((hi & bit) != 0) & ((lo & bit) == 0)

    levels = [[level_mask(lvl, d == 0) for lvl in range(6)] for d in range(2)]

    def block_diag(x):
        return jnp.concatenate([jnp.where(left, x, 0.0), jnp.where(left, 0.0, x)], axis=0)

    def anti_diag(x):
        xr = pltpu.roll(x, CHUNK, 1)
        return jnp.concatenate([jnp.where(left, 0.0, xr), jnp.where(left, xr, 0.0)], axis=0)

    chains = [(d, n, p) for n in range(n_par) for d in range(2) for p in range(HEADS // 2)]
    incl = [same & ((ri >= ci) if d == 0 else (ri <= ci)) for d in range(2)]
    strict = [same & ((ri > ci) if d == 0 else (ri < ci)) for d in range(2)]

    def load(which):
        return [ins[5 * d + which][n, 0, :, p * n2:(p + 1) * n2] for d, n, p in chains]

    q, k, v, gc, beta = load(0), load(1), load(2), load(3), load(4)
    gtot = [gi[CHUNK - 1:CHUNK, :] if d == 0 else gi[0:1, :] for (d, _, _), gi in zip(chains, gc)]
    eg = [jnp.exp(gi) for gi in gc]
    kb = [ki * bi for ki, bi in zip(k, beta)]
    decay = []
    for (d, _, _), gi in zip(chains, gc):
        gcr = pltpu.roll(gi, CHUNK, 1)
        gcol = jnp.concatenate([jnp.where(left, gi, gcr), jnp.where(left, gcr, gi)], axis=0)
        diff = gcol - gcol.T
        decay.append(jnp.where(incl[d], jnp.exp(jnp.where(incl[d], diff, 0.0)), 0.0))
    k_bd = [block_diag(ki).astype(BF16) for ki in k]
    kk = [_dot_nt(block_diag(kbi).astype(BF16), kbd) for kbi, kbd in zip(kb, k_bd)]
    qk = [_dot_nt(block_diag(qi).astype(BF16), kbd) for qi, kbd in zip(q, k_bd)]
    a_mat = [jnp.where(strict[d], kki * di, 0.0) for (d, _, _), kki, di in zip(chains, kk, decay)]
    qk = [(qki * di).astype(BF16) for qki, di in zip(qk, decay)]
    x = []
    for vi, bi, kbi, egi in zip(v, beta, kb, eg):
        vb = vi * bi
        kbe_r = pltpu.roll(kbi * egi, CHUNK, 1)
        x.append(jnp.concatenate([jnp.where(left, vb, kbe_r), jnp.where(left, kbe_r, vb)], axis=0))
    corr = [-jnp.where(levels[d][0], ai, 0.0) for (d, _, _), ai in zip(chains, a_mat)]
    for lvl in range(1, 6):
        d_b = [(eye + ci_).astype(BF16) for ci_ in corr]
        t = [_dot(jnp.where(levels[d][lvl], ai, 0.0).astype(BF16), di).astype(BF16)
             for (d, _, _), ai, di in zip(chains, a_mat, d_b)]
        corr = [ci_ - _dot(di, ti) for ci_, di, ti in zip(corr, d_b, t)]
    x = [xi + _dot(ci_.astype(BF16), xi.astype(BF16)) for xi, ci_ in zip(x, corr)]
    s = [s_scr[d, n, p] for d, n, p in chains]
    s_b = [si.astype(BF16) for si in s]
    ws = [_dot(jnp.where(same, 0.0, xi).astype(BF16), sbi) for xi, sbi in zip(x, s_b)]
    v_new = [(jnp.where(same, xi, 0.0) - wi).astype(BF16) for xi, wi in zip(x, ws)]
    o_s = [_dot(anti_diag(qi * egi).astype(BF16), sbi) for qi, egi, sbi in zip(q, eg, s_b)]
    o_v = [_dot(qki, vni) for qki, vni in zip(qk, v_new)]
    kd = [anti_diag(ki * jnp.exp(gt - gi)).astype(BF16) for ki, gt, gi in zip(k, gtot, gc)]
    s_upd = [_dot_tn(kdi, vni) for kdi, vni in zip(kd, v_new)]
    for i, (d, n, p) in enumerate(chains):
        o = o_s[i] + o_v[i]
        o_refs[d][n, 0, :, p * n2:(p + 1) * n2] = o[:CHUNK] + o[CHUNK:]
        s_scr[d, n, p] = s[i] * jnp.exp(gtot[i]) + s_upd[i]

    @pl.when(c == n_chunks - 1)
    def _():
        for d in range(2):
            s_out_refs[d][...] = s_scr[d]


GDN_SEQS_PER_STEP = 2


def _gdn_scan(qn, kn, v, gcf, gcb, bf, bb, s0, n_seq, seq):
    nc = seq // CHUNK
    hw = HEADS * HD
    n2 = 2 * CHUNK
    n_par = GDN_SEQS_PER_STEP if n_seq % GDN_SEQS_PER_STEP == 0 else 1
    view = lambda a: a.reshape(n_seq, nc, CHUNK, hw)
    fwd = pl.BlockSpec((n_par, 1, CHUNK, hw), lambda b, c: (b, c, 0, 0))
    bwd = pl.BlockSpec((n_par, 1, CHUNK, hw), lambda b, c: (b, nc - 1 - c, 0, 0))
    state = pl.BlockSpec((n_par, HEADS // 2, n2, n2), lambda b, c: (b, 0, 0, 0))
    in_specs = [fwd] * 5 + [bwd] * 5
    args = [view(a) for a in (qn, kn, v, gcf, bf, qn, kn, v, gcb, bb)]
    if s0 is not None:
        in_specs += [state, state]
        args += list(s0)
    st_shape = jax.ShapeDtypeStruct((n_seq, HEADS // 2, n2, n2), F32)
    o_shape = jax.ShapeDtypeStruct((n_seq, nc, CHUNK, hw), F32)
    o_f, o_b, s_f, s_b = pl.pallas_call(
        functools.partial(_gdn_kernel, has_s0=s0 is not None, n_chunks=nc, n_par=n_par),
        grid=(n_seq // n_par, nc),
        in_specs=in_specs,
        out_specs=[fwd, bwd, state, state],
        out_shape=[o_shape, o_shape, st_shape, st_shape],
        scratch_shapes=[pltpu.VMEM((2, n_par, HEADS // 2, n2, n2), F32)],
        compiler_params=_cparams(("parallel", "arbitrary")),
        name="gdn_scan",
    )(*args)
    return o_f.reshape(n_seq * seq, hw), o_b.reshape(n_seq * seq, hw), s_f, s_b


def _state_to_pairs(s):
    b = s.shape[0]
    s = s.reshape(b, HEADS // 2, 2, HD, HD)
    z = jnp.zeros_like(s[:, :, 0])
    top = jnp.concatenate([z, s[:, :, 1]], axis=-1)
    bot = jnp.concatenate([s[:, :, 0], z], axis=-1)
    return jnp.concatenate([top, bot], axis=-2)


def _pairs_to_state(s):
    b = s.shape[0]
    even = s[:, :, HD:, :HD]
    odd = s[:, :, :HD, HD:]
    return jnp.stack([even, odd], axis=2).reshape(b, HEADS, HD, HD)


def _merge_kernel(x_ref, mod_ref, *refs, ctx_blocks):
    pairs, (z_ref, gates_ref, gain_ref, ones_ref, wa_ref, wb_ref, wc_ref, wo_ref, o_ref) = refs[:8], refs[8:]
    is_ctx = pl.program_id(0) < ctx_blocks
    ya, yb, o_f, o_b = [jnp.where(is_ctx, pairs[2 * t][...], pairs[2 * t + 1][...]) for t in range(4)]
    o = o_f + o_b
    ms = _group_sums(o * o, ones_ref[...]) * (1.0 / HD)
    yc = o * lax.rsqrt(ms + EPS) * gain_ref[...] * _silu(z_ref[...])
    gates = gates_ref[...]
    merged = (jax.nn.sigmoid(gates[:, :D_MODEL]) * _dot(ya, wa_ref[0])
              + jax.nn.sigmoid(gates[:, D_MODEL:2 * D_MODEL]) * _dot(yb, wb_ref[0])
              + jax.nn.sigmoid(gates[:, 2 * D_MODEL:]) * _dot(yc.astype(BF16), wc_ref[0]))
    mix = _dot(merged.astype(BF16), wo_ref[0])
    o_ref[...] = x_ref[...] + mod_ref[0, 0][2:3] * mix


def _merge(x, mod, layer, mixers, y, gain, wa, wb, wc, wo, mod_row, ctx_rows):
    m_rows = x.shape[0]
    tm = TOK
    nb_ctx = ctx_rows // tm
    nb_lat = (m_rows - ctx_rows) // tm
    ctx_half = pl.BlockSpec((tm, 512), lambda i: (jnp.minimum(i, nb_ctx - 1), 0))
    lat_half = pl.BlockSpec((tm, 512), lambda i: (jnp.clip(i - nb_ctx, 0, nb_lat - 1), 0))
    full = pl.BlockSpec((tm, D_MODEL), lambda i: (i, 0))

    def wspec(k):
        return pl.BlockSpec((1, k, D_MODEL), lambda i: (layer, 0, 0))

    return pl.pallas_call(
        functools.partial(_merge_kernel, ctx_blocks=nb_ctx),
        grid=(m_rows // tm,),
        in_specs=[full,
                  pl.BlockSpec((1, 1, N_MOD, D_MODEL), lambda i: (layer, mod_row(i, tm), 0, 0))]
        + [ctx_half, lat_half] * 4
        + [pl.BlockSpec((tm, 512), lambda i: (i, C_Z // 512)),
           pl.BlockSpec((tm, 3 * D_MODEL), lambda i: (i, 0)),
           pl.BlockSpec((1, 512), lambda i: (0, 0)),
           pl.BlockSpec((256, 256), lambda i: (0, 0)),
           wspec(512), wspec(512), wspec(512), wspec(D_MODEL)],
        out_specs=full,
        out_shape=jax.ShapeDtypeStruct((m_rows, D_MODEL), F32),
        compiler_params=_cparams(("parallel",)),
        name="merge_out",
    )(x, mod, *mixers, y, y, gain, _ones_blockdiag(256, HD), wa, wb, wc, wo)


def _ffn_kernel(x_ref, mod_ref, g_ref, wg_ref, wu_ref, wd_ref, fin_ref, o_ref, *, final):
    x = x_ref[...]
    m = mod_ref[0, 0]
    h = (_rms_rows(x, g_ref[...]) * (1.0 + m[4:5]) + m[3:4]).astype(BF16)
    act = (_silu(_dot(h, wg_ref[0])) * _dot(h, wu_ref[0])).astype(BF16)
    out = x + m[5:6] * _dot(act, wd_ref[0])
    if final:
        out = _rms_rows(out, fin_ref[...])
    o_ref[...] = out


def _ffn(x, mod, layer, gain, wg, wu, wd, fin, mod_row, final):
    m_rows = x.shape[0]
    tm = TOK
    full = pl.BlockSpec((tm, D_MODEL), lambda i: (i, 0))
    vec = pl.BlockSpec((1, D_MODEL), lambda i: (0, 0))
    once = pl.Buffered(1)
    return pl.pallas_call(
        functools.partial(_ffn_kernel, final=final),
        grid=(m_rows // tm,),
        in_specs=[full,
                  pl.BlockSpec((1, 1, N_MOD, D_MODEL), lambda i: (layer, mod_row(i, tm), 0, 0)),
                  vec,
                  pl.BlockSpec((1, D_MODEL, D_FF), lambda i: (layer, 0, 0), pipeline_mode=once),
                  pl.BlockSpec((1, D_MODEL, D_FF), lambda i: (layer, 0, 0), pipeline_mode=once),
                  pl.BlockSpec((1, D_FF, D_MODEL), lambda i: (layer, 0, 0), pipeline_mode=once),
                  vec],
        out_specs=full,
        out_shape=jax.ShapeDtypeStruct((m_rows, D_MODEL), F32),
        compiler_params=_cparams(("parallel",)),
        name="ffn",
    )(x, mod, gain, wg, wu, wd, fin)


def _layout_w_in(w_in):
    def cols(a, b):
        return w_in[:, :, a:b]

    pad = jnp.zeros(w_in.shape[:2] + (64,), w_in.dtype)
    parts = [cols(3520, 6592), cols(1440, 2976), cols(768, 1152), cols(1152, 1408), cols(1408, 1440),
             cols(3488, 3504), cols(3504, 3520), pad, cols(512, 640), cols(640, 768), cols(2976, 3488),
             cols(0, 512)]
    return jnp.concatenate(parts, axis=-1).astype(BF16)


def _layout_w_uq(w_uq):
    n_layers = w_uq.shape[0]
    w = w_uq.reshape(n_layers, Q_LORA, HEADS, NOPE_B + ROPE_B)
    w = jnp.pad(w, ((0, 0), (0, 0), (0, 0), (0, LANES - NOPE_B - ROPE_B)))
    return w.reshape(n_layers, Q_LORA, HEADS * LANES).astype(BF16)


def _layout_w_ukv(w_ukv):
    n_layers = w_ukv.shape[0]
    w = w_ukv.reshape(n_layers, KV_LORA, HEADS, 2, HD)
    zero = jnp.zeros((n_layers, KV_LORA, HEADS, HD), w.dtype)
    wk_top = jnp.concatenate([w[:, :, :, 0], zero], axis=-1).reshape(n_layers, KV_LORA, HEADS * LANES)
    place = np.zeros((LANES, HEADS, LANES), np.float32)
    for d in range(ROPE_B):
        place[d, :, NOPE_B + d] = 1.0
    place = jnp.broadcast_to(jnp.asarray(place.reshape(LANES, HEADS * LANES)), (n_layers, LANES, HEADS * LANES))
    wk = jnp.concatenate([wk_top, place], axis=1).astype(BF16)
    wv = w[:, :, :, 1].reshape(n_layers, KV_LORA, HEADS * HD).astype(BF16)
    return wk, wv


def kernel(x_prompt, x_sample, cache_ka, cache_va, cache_ckv, cache_kpe, state_fwd, state_bwd, c, c_ctx,
           w_mod, b_mod, norm1, norm2, w_in, a_qnorm, a_knorm, b_qnorm, b_kvnorm, w_uq, w_ukv, c_conv,
           c_alog, c_dt_bias, c_onorm, w_pa, w_pb, w_pc, w_out, w_gate, w_up, w_down, final_norm):
    n_ctx, s_ctx, _ = x_prompt.shape
    n_lat, s_lat, _ = x_sample.shape
    n_layers = w_mod.shape[0]
    past = cache_ka.shape[2]
    ctx_rows = n_ctx * s_ctx
    lat_rows = n_lat * s_lat
    assert s_ctx == TOK and s_lat % TOK == 0 and ctx_rows % 512 == 0 and s_lat % 512 == 0
    assert past % TOK == 0 and n_lat + 1 <= 8 and ctx_rows % s_lat == 0

    def mod_row(i, tm):
        return jnp.where(i < ctx_rows // tm, 0, 1 + (i - ctx_rows // tm) // (s_lat // tm))

    def tab_row(i):
        nb = ctx_rows // TOK
        return jnp.where(i < nb, 0, 1 + (i - nb) % (s_lat // TOK))

    tm_in = 1024 if ctx_rows % 1024 == 0 and s_lat % 1024 == 0 else 512

    x = jnp.concatenate([x_prompt.reshape(ctx_rows, D_MODEL), x_sample.reshape(lat_rows, D_MODEL)], axis=0)
    cvec = jnp.zeros((8, D_MODEL), F32).at[0].set(c_ctx).at[1:1 + n_lat].set(c)
    mod = _modulation(cvec, w_mod, b_mod).reshape(n_layers, 8, N_MOD, D_MODEL)
    cos, sin = _rope_tables(s_lat)

    w_in_p = _layout_w_in(w_in)
    w_uq_p = _layout_w_uq(w_uq)
    w_k_p, w_v_p = _layout_w_ukv(w_ukv)
    bf = lambda w: w.astype(BF16)
    w_pa_b, w_pb_b, w_pc_b, w_out_b = bf(w_pa), bf(w_pb), bf(w_pc), bf(w_out)
    w_gate_b, w_up_b, w_down_b = bf(w_gate), bf(w_up), bf(w_down)

    cache_rows = n_lat * n_layers * past
    ck4, cv4 = _cache_a(cache_ka.reshape(cache_rows, LANES), cache_va.reshape(cache_rows, LANES))
    kpe_pad = jnp.pad(cache_kpe.reshape(cache_rows, ROPE_B), ((0, 0), (0, LANES - ROPE_B)))
    ckb, cvb = _cache_b(cache_ckv.reshape(cache_rows, KV_LORA), kpe_pad, w_k_p, w_v_p,
                        lambda i: (i // (past // TOK)) % n_layers)

    lane_vec = lambda v16: jnp.zeros((1, LANES), F32).at[0, 32:48].set(v16.reshape(16))
    new = []
    for l in range(n_layers):
        y = _in_proj(x, mod, l, norm1[l][None], w_in_p, mod_row, tm_in)

        qa, k4, v4, kn_a = _prep_a(y, cos, sin, jnp.tile(a_qnorm[l], HEADS)[None],
                                   jnp.tile(a_knorm[l], KV_HEADS_A)[None], tab_row)
        qb, kb, vb, ckvn = _prep_b(y, cos, sin, b_qnorm[l][None], b_kvnorm[l][None],
                                   w_uq_p, w_k_p, w_v_p, l, tab_row)
        conv_args = (c_conv[l, :, 0, :], lane_vec(c_alog[l]), lane_vec(c_dt_bias[l]))
        gdn_ctx = _prep_c(y, *conv_args, 0, ctx_rows, s_ctx)
        gdn_lat = _prep_c(y, *conv_args, ctx_rows, lat_rows, s_lat)

        cache_blk = (past, lambda b, l=l: b * n_layers + l)
        ya_c = _attention(qa, k4, v4, None, GROUPS_A, 0, n_ctx, s_ctx, s_ctx, None)
        ya_l = _attention(qa, k4, v4, (ck4, cv4), GROUPS_A, ctx_rows, n_lat, s_lat, 256, cache_blk)
        yb_c = _attention(qb, kb, vb, None, GROUPS_B, 0, n_ctx, s_ctx, s_ctx, None)
        yb_l = _attention(qb, kb, vb, (ckb, cvb), GROUPS_B, ctx_rows, n_lat, s_lat, 256, cache_blk)

        of_c, ob_c, sf_c, sb_c = _gdn_scan(*gdn_ctx, None, n_ctx, s_ctx)
        s0 = (_state_to_pairs(state_fwd[:, l]), _state_to_pairs(state_bwd[:, l]))
        of_l, ob_l, _, _ = _gdn_scan(*gdn_lat, s0, n_lat, s_lat)

        x = _merge(x, mod, l, (ya_c, ya_l, yb_c, yb_l, of_c, of_l, ob_c, ob_l), y,
                   jnp.tile(c_onorm[l], HEADS)[None], w_pa_b, w_pb_b, w_pc_b, w_out_b, mod_row, ctx_rows)
        x = _ffn(x, mod, l, norm2[l][None], w_gate_b, w_up_b, w_down_b, final_norm[None], mod_row,
                 final=(l == n_layers - 1))

        new.append((kn_a[:ctx_rows].reshape(n_ctx, s_ctx, KV_HEADS_A, HD),
                    y[:ctx_rows, C_VA:C_VA + LANES].reshape(n_ctx, s_ctx, KV_HEADS_A, HD),
                    ckvn[:ctx_rows].reshape(n_ctx, s_ctx, KV_LORA),
                    y[:ctx_rows, C_B + 640:C_B + 640 + ROPE_B].reshape(n_ctx, s_ctx, ROPE_B),
                    _pairs_to_state(sf_c), _pairs_to_state(sb_c)))

    y_prompt = x[:ctx_rows].reshape(n_ctx, s_ctx, D_MODEL)
    y_sample = x[ctx_rows:].reshape(n_lat, s_lat, D_MODEL)
    stacked = [jnp.stack([n[i] for n in new], axis=1) for i in range(6)]
    return (y_prompt, y_sample) + tuple(stacked)
```

```python
import functools

import numpy as np
import jax
import jax.numpy as jnp
from jax import lax
from jax.experimental import pallas as pl
from jax.experimental.pallas import tpu as pltpu

F32 = jnp.float32
BF16 = jnp.bfloat16

D_MODEL = 1024
GRID_W = 64
ROPE_THETA = 10000.0
EPS = 1e-6
N_MOD = 6
HEADS = 8
KV_HEADS_A = 2
HD = 64
Q_LORA = 384
KV_LORA = 256
NOPE_B = 64
ROPE_B = 32
CHUNK = 64
CONV_CH = 3 * HEADS * HD
D_FF = 2816

C_QKV = 0
C_B = 1536
C_KA = 2304
C_VA = 2432
C_QA = 2560
N_IN = 3072
N_GZ = 3 * D_MODEL + HEADS * HD

LOG2E = 1.4426950408889634
LANES = 128
TOK = 256
ROW_CHUNK = 128
VMEM_LIMIT = 56 * 1024 * 1024


def _cparams(sem):
    return pltpu.CompilerParams(dimension_semantics=sem, vmem_limit_bytes=VMEM_LIMIT)


def _dot(a, b):
    return jnp.dot(a, b, preferred_element_type=F32)


def _dot_nt(a, b):
    return lax.dot_general(a, b, (((1,), (1,)), ((), ())), preferred_element_type=F32)


def _dot_tn(a, b):
    return lax.dot_general(a, b, (((0,), (0,)), ((), ())), preferred_element_type=F32)


def _split3(x):
    hi = x.astype(BF16)
    r = x - hi.astype(F32)
    mid = r.astype(BF16)
    lo = (r - mid.astype(F32)).astype(BF16)
    return hi, mid, lo


def _dot3(x, w):
    hi, mid, lo = _split3(x)
    return _dot(hi, w) + _dot(mid, w) + _dot(lo, w)


def _group_sums(x, ones):
    w = min(x.shape[1], ones.shape[0])
    blk = ones[:w, :w]
    slabs = []
    for s in range(x.shape[1] // w):
        xs = x[:, s * w:(s + 1) * w]
        hi = xs.astype(BF16)
        lo = (xs - hi.astype(F32)).astype(BF16)
        slabs.append(_dot(hi, blk) + _dot(lo, blk))
    return slabs[0] if len(slabs) == 1 else jnp.concatenate(slabs, axis=1)


def _silu(x):
    return x * jax.nn.sigmoid(x)


def _rms_rows(x, g):
    return x * lax.rsqrt(jnp.mean(x * x, axis=-1, keepdims=True) + EPS) * g


def _swap_pairs(x, half):
    lane = lax.broadcasted_iota(jnp.int32, x.shape, 1)
    n = x.shape[1]
    return jnp.where((lane & half) == 0, pltpu.roll(x, n - half, 1), pltpu.roll(x, half, 1))


def _rope_cols(x, cos, sin, half):
    outs = []
    for s in range(x.shape[1] // LANES):
        xs = x[:, s * LANES:(s + 1) * LANES]
        outs.append(xs * cos + _swap_pairs(xs, half) * sin)
    return outs[0] if len(outs) == 1 else jnp.concatenate(outs, axis=1)


def _mod_kernel(c_ref, w_ref, b_ref, o_ref):
    s = _silu(c_ref[...]).astype(BF16)
    o_ref[0] = _dot(s, w_ref[0].astype(BF16)) + b_ref[0]


def _modulation(cvec, w_mod, b_mod):
    n_layers = w_mod.shape[0]
    n_out = w_mod.shape[2]
    tn = 1536
    return pl.pallas_call(
        _mod_kernel,
        grid=(n_layers, n_out // tn),
        in_specs=[pl.BlockSpec((8, D_MODEL), lambda l, j: (0, 0)),
                  pl.BlockSpec((1, D_MODEL, tn), lambda l, j: (l, 0, j)),
                  pl.BlockSpec((1, 1, tn), lambda l, j: (l, 0, j))],
        out_specs=pl.BlockSpec((1, 8, tn), lambda l, j: (l, 0, j)),
        out_shape=jax.ShapeDtypeStruct((n_layers, 8, n_out), F32),
        compiler_params=_cparams(("parallel", "parallel")),
        name="modulation",
    )(cvec, w_mod, b_mod.reshape(n_layers, 1, n_out))


def _in_proj_kernel(x_ref, mod_ref, g_ref, w_ref, o_ref):
    m = mod_ref[0, 0]
    for r in range(0, x_ref.shape[0], ROW_CHUNK):
        y = _rms_rows(x_ref[r:r + ROW_CHUNK, :], g_ref[...])
        h = (y * (1.0 + m[1:2]) + m[0:1]).astype(BF16)
        o_ref[r:r + ROW_CHUNK, :] = _dot(h, w_ref[0])


def _in_proj(x, mod, layer, gain, w, mod_row, tm):
    m_rows = x.shape[0]
    return pl.pallas_call(
        _in_proj_kernel,
        grid=(m_rows // tm,),
        in_specs=[pl.BlockSpec((tm, D_MODEL), lambda i: (i, 0)),
                  pl.BlockSpec((1, 1, N_MOD, D_MODEL), lambda i: (layer, mod_row(i, tm), 0, 0)),
                  pl.BlockSpec((1, D_MODEL), lambda i: (0, 0)),
                  pl.BlockSpec((1, D_MODEL, N_IN), lambda i: (layer, 0, 0))],
        out_specs=pl.BlockSpec((tm, N_IN), lambda i: (i, 0)),
        out_shape=jax.ShapeDtypeStruct((m_rows, N_IN), F32),
        compiler_params=_cparams(("parallel",)),
        name="in_proj",
    )(x, mod, gain, w)


def _rope_table_kernel(ang_ref, sign_ref, cos_ref, sin_ref):
    a = ang_ref[...]
    cos_ref[...] = jnp.cos(a)
    sin_ref[...] = jnp.sin(a) * sign_ref[...]


def _rope_tables(n_lat):
    t = np.arange(n_lat)
    row = jnp.asarray((t // GRID_W).astype(np.float32))[:, None]
    col = jnp.asarray((t % GRID_W).astype(np.float32))[:, None]
    lane = np.arange(LANES)

    def table(nf, lo, width):
        inv_freq = ROPE_THETA ** (-jnp.arange(nf, dtype=F32) / nf)
        d = lane - lo
        active = (d >= 0) & (d < width)
        dd = np.where(active, d, 0)
        f_idx = dd % nf
        use_col = dd >= 2 * nf
        freq = jnp.where(jnp.asarray(active), inv_freq[f_idx], 0.0)[None, :]
        ang = jnp.where(jnp.asarray(use_col)[None, :], col, row) * freq
        sign = np.where((dd % (2 * nf)) < nf, -1.0, 1.0).astype(np.float32)
        return ang, sign

    period_a = HD
    lane_a = lane % period_a
    nf_a = HD // 4
    inv_a = ROPE_THETA ** (-jnp.arange(nf_a, dtype=F32) / nf_a)
    ang_a = jnp.where(jnp.asarray(lane_a >= 2 * nf_a)[None, :], col, row) * inv_a[lane_a % nf_a][None, :]
    sign_a = np.where((lane_a % (2 * nf_a)) < nf_a, -1.0, 1.0).astype(np.float32)
    ang_q, sign_q = table(ROPE_B // 4, NOPE_B, ROPE_B)
    ang_k, sign_k = table(ROPE_B // 4, 0, ROPE_B)
    ang = jnp.stack([ang_a, ang_q, ang_k])
    ang = jnp.concatenate([jnp.zeros((3, TOK, LANES), F32), ang], axis=1)
    sign = jnp.asarray(np.stack([sign_a, sign_q, sign_k]))[:, None, :]
    rows = TOK + n_lat
    spec = pl.BlockSpec((1, TOK, LANES), lambda a, i: (a, i, 0))
    return pl.pallas_call(
        _rope_table_kernel,
        grid=(3, rows // TOK),
        in_specs=[spec, pl.BlockSpec((1, 1, LANES), lambda a, i: (a, 0, 0))],
        out_specs=[spec, spec],
        out_shape=[jax.ShapeDtypeStruct((3, rows, LANES), F32)] * 2,
        compiler_params=_cparams(("parallel", "parallel")),
        name="rope_tables",
    )(ang, sign)


def _prep_a_kernel(qa_ref, kv_ref, cos_ref, sin_ref, gq_ref, gk_ref, ones_ref, place_ref,
                   q_ref, k4_ref, vt_ref, kn_ref):
    cos = cos_ref[0]
    sin = sin_ref[0]
    ones = ones_ref[...]
    q = qa_ref[...]
    qn = q * lax.rsqrt(_group_sums(q * q, ones) * (1.0 / HD) + EPS) * gq_ref[...]
    q_ref[...] = (_rope_cols(qn, cos, sin, HD // 4) * (HD ** -0.5 * LOG2E)).astype(BF16)
    k = kv_ref[:, :LANES]
    kn = k * lax.rsqrt(_group_sums(k * k, ones) * (1.0 / HD) + EPS) * gk_ref[...]
    kn_ref[...] = kn
    k4_ref[...] = _dot(_rope_cols(kn, cos, sin, HD // 4).astype(BF16), place_ref[...]).astype(BF16)
    vt_ref[...] = kv_ref[:, LANES:].T.astype(BF16)


def _place4():
    p = np.zeros((LANES, 4 * LANES), np.float32)
    for kv in range(KV_HEADS_A):
        for side in range(2):
            for d in range(HD):
                p[kv * HD + d, (2 * kv + side) * LANES + side * HD + d] = 1.0
    return jnp.asarray(p, BF16)


def _ones_blockdiag(n, group):
    idx = np.arange(n) // group
    return jnp.asarray((idx[:, None] == idx[None, :]).astype(np.float32), BF16)


def _prep_a(y, cos, sin, gq, gk, tab_row):
    m_rows = y.shape[0]
    wide = pl.BlockSpec((TOK, 4 * LANES), lambda i: (i, 0))
    tab = pl.BlockSpec((1, TOK, LANES), lambda i: (0, tab_row(i), 0))
    return pl.pallas_call(
        _prep_a_kernel,
        grid=(m_rows // TOK,),
        in_specs=[pl.BlockSpec((TOK, 512), lambda i: (i, C_QA // 512)),
                  pl.BlockSpec((TOK, 256), lambda i: (i, C_KA // 256)),
                  tab, tab,
                  pl.BlockSpec((1, 512), lambda i: (0, 0)),
                  pl.BlockSpec((1, LANES), lambda i: (0, 0)),
                  pl.BlockSpec((256, 256), lambda i: (0, 0)),
                  pl.BlockSpec((LANES, 512), lambda i: (0, 0))],
        out_specs=[wide, wide, pl.BlockSpec((LANES, TOK), lambda i: (0, i)),
                   pl.BlockSpec((TOK, LANES), lambda i: (i, 0))],
        out_shape=[jax.ShapeDtypeStruct((m_rows, 512), BF16)] * 2
        + [jax.ShapeDtypeStruct((LANES, m_rows), BF16), jax.ShapeDtypeStruct((m_rows, LANES), F32)],
        compiler_params=_cparams(("parallel",)),
        name="prep_a",
    )(y, y, cos, sin, gq, gk, _ones_blockdiag(256, HD), _place4())


def _cache_a_kernel(k_ref, v_ref, place_ref, k4_ref, vt_ref):
    k4_ref[...] = _dot(k_ref[...].astype(BF16), place_ref[...]).astype(BF16)
    vt_ref[...] = v_ref[...].T.astype(BF16)


def _cache_a(ck, cv):
    rows = ck.shape[0]
    narrow = pl.BlockSpec((TOK, LANES), lambda i: (i, 0))
    return pl.pallas_call(
        _cache_a_kernel,
        grid=(rows // TOK,),
        in_specs=[narrow, narrow, pl.BlockSpec((LANES, 512), lambda i: (0, 0))],
        out_specs=[pl.BlockSpec((TOK, 512), lambda i: (i, 0)), pl.BlockSpec((LANES, TOK), lambda i: (0, i))],
        out_shape=[jax.ShapeDtypeStruct((rows, 512), BF16), jax.ShapeDtypeStruct((LANES, rows), BF16)],
        compiler_params=_cparams(("parallel",)),
        name="cache_a",
    )(ck, cv, _place4())


def _prep_b_kernel(g_ref, cq_ref, sq_ref, ck_ref, sk_ref, gq_ref, gkv_ref, wq_ref, wk_ref, wv_ref,
                   q_ref, k_ref, vt_ref, ckv_ref):
    g = g_ref[...]
    cqn = _rms_rows(g[:, :Q_LORA], gq_ref[...])
    q = _dot(cqn.astype(BF16), wq_ref[0])
    q = _rope_cols(q, cq_ref[0], sq_ref[0], ROPE_B // 4) * ((NOPE_B + ROPE_B) ** -0.5 * LOG2E)
    q_ref[...] = q.astype(BF16)
    ckvn = _rms_rows(g[:, Q_LORA:Q_LORA + KV_LORA], gkv_ref[...])
    ckv_ref[...] = ckvn
    kpe = _rope_cols(g[:, Q_LORA + KV_LORA:], ck_ref[0], sk_ref[0], ROPE_B // 4)
    ckv_b = ckvn.astype(BF16)
    kin = jnp.concatenate([ckv_b, kpe.astype(BF16)], axis=1)
    k_ref[...] = _dot(kin, wk_ref[0]).astype(BF16)
    vt_ref[...] = _dot(ckv_b, wv_ref[0]).T.astype(BF16)


def _prep_b(y, cos, sin, gq, gkv, wq, wk, wv, layer, tab_row):
    m_rows = y.shape[0]
    wide = pl.BlockSpec((TOK, HEADS * LANES), lambda i: (i, 0))

    def tab(a):
        return pl.BlockSpec((1, TOK, LANES), lambda i: (a, tab_row(i), 0))

    def wspec(k, n):
        return pl.BlockSpec((1, k, n), lambda i: (layer, 0, 0))

    return pl.pallas_call(
        _prep_b_kernel,
        grid=(m_rows // TOK,),
        in_specs=[pl.BlockSpec((TOK, 768), lambda i: (i, C_B // 768)),
                  tab(1), tab(1), tab(2), tab(2),
                  pl.BlockSpec((1, Q_LORA), lambda i: (0, 0)),
                  pl.BlockSpec((1, KV_LORA), lambda i: (0, 0)),
                  wspec(Q_LORA, HEADS * LANES), wspec(KV_LORA + LANES, HEADS * LANES),
                  wspec(KV_LORA, HEADS * HD)],
        out_specs=[wide, wide, pl.BlockSpec((HEADS * HD, TOK), lambda i: (0, i)),
                   pl.BlockSpec((TOK, KV_LORA), lambda i: (i, 0))],
        out_shape=[jax.ShapeDtypeStruct((m_rows, HEADS * LANES), BF16)] * 2
        + [jax.ShapeDtypeStruct((HEADS * HD, m_rows), BF16), jax.ShapeDtypeStruct((m_rows, KV_LORA), F32)],
        compiler_params=_cparams(("parallel",)),
        name="prep_b",
    )(y, cos, sin, cos, sin, gq, gkv, wq, wk, wv)


def _cache_b_kernel(c_ref, p_ref, wk_ref, wv_ref, k_ref, vt_ref):
    ckv_b = c_ref[...].astype(BF16)
    kin = jnp.concatenate([ckv_b, p_ref[...].astype(BF16)], axis=1)
    k_ref[...] = _dot(kin, wk_ref[0]).astype(BF16)
    vt_ref[...] = _dot(ckv_b, wv_ref[0]).T.astype(BF16)


def _cache_b(ckv, kpe, wk, wv, layer_of_block):
    rows = ckv.shape[0]
    return pl.pallas_call(
        _cache_b_kernel,
        grid=(rows // TOK,),
        in_specs=[pl.BlockSpec((TOK, KV_LORA), lambda i: (i, 0)),
                  pl.BlockSpec((TOK, LANES), lambda i: (i, 0)),
                  pl.BlockSpec((1, KV_LORA + LANES, HEADS * LANES), lambda i: (layer_of_block(i), 0, 0)),
                  pl.BlockSpec((1, KV_LORA, HEADS * HD), lambda i: (layer_of_block(i), 0, 0))],
        out_specs=[pl.BlockSpec((TOK, HEADS * LANES), lambda i: (i, 0)),
                   pl.BlockSpec((HEADS * HD, TOK), lambda i: (0, i))],
        out_shape=[jax.ShapeDtypeStruct((rows, HEADS * LANES), BF16),
                   jax.ShapeDtypeStruct((HEADS * HD, rows), BF16)],
        compiler_params=_cparams(("parallel",)),
        name="cache_b",
    )(ckv, kpe, wk, wv)


ATT_KEY_TILE = 256
ATT_LOOKAHEAD = 5


def _attn_kernel(*refs, groups, has_cache):
    if has_cache:
        q_ref, kc_ref, vc_ref, kn_ref, vn_ref, o_ref = refs
        segments = [(kc_ref, vc_ref), (kn_ref, vn_ref)]
    else:
        q_ref, kn_ref, vn_ref, o_ref = refs
        segments = [(kn_ref, vn_ref)]
    tiles = [(k_ref, v_ref, t * ATT_KEY_TILE) for k_ref, v_ref in segments
             for t in range(k_ref.shape[0] // ATT_KEY_TILE)]
    q = [q_ref[:, qg * LANES:(qg + 1) * LANES] for qg, _, _ in groups]

    def scores(h, i):
        k_ref, _, t0 = tiles[i]
        kg = groups[h][1]
        return _dot_nt(k_ref[t0:t0 + ATT_KEY_TILE, kg * LANES:(kg + 1) * LANES], q[h])

    heads = range(len(groups))
    jobs = [(i, h) for i in range(len(tiles)) for h in heads]
    pending = {j: scores(h, i) for j, (i, h) in enumerate(jobs[:ATT_LOOKAHEAD])}
    ones_rows = (lax.broadcasted_iota(jnp.int32, (16, ATT_KEY_TILE), 0) == 0).astype(BF16)
    m, acc = {}, {}
    for j, (i, h) in enumerate(jobs):
        if j + ATT_LOOKAHEAD < len(jobs):
            i_n, h_n = jobs[j + ATT_LOOKAHEAD]
            pending[j + ATT_LOOKAHEAD] = scores(h_n, i_n)
        s = pending.pop(j)
        _, v_ref, t0 = tiles[i]
        vrow = groups[h][2]
        v_t = jnp.concatenate([v_ref[vrow:vrow + HD, t0:t0 + ATT_KEY_TILE], ones_rows], axis=0)
        m_tile = jnp.max(s, axis=0, keepdims=True)
        if i == 0:
            m[h] = m_tile
            acc[h] = _dot(v_t, jnp.exp2(s - m_tile).astype(BF16))
        else:
            m_new = jnp.maximum(m[h], m_tile)
            alpha = jnp.exp2(m[h] - m_new)
            acc[h] = acc[h] * alpha + _dot(v_t, jnp.exp2(s - m_new).astype(BF16))
            m[h] = m_new
    outs = [acc[h][:HD] / acc[h][HD:HD + 1] for h in heads]
    o_ref[...] = jnp.concatenate(outs, axis=0).T.astype(BF16)


def _attention(q, kn, vn, cache, groups, row0, n_seq, seq, tq, cache_block):
    qw, kw, vr = q.shape[1], kn.shape[1], vn.shape[0]
    nq = seq // tq
    qb0, sb0 = row0 // tq, row0 // seq
    in_specs = [pl.BlockSpec((tq, qw), lambda b, i: (qb0 + b * nq + i, 0))]
    args = [q]
    if cache is not None:
        past = cache_block[0]
        in_specs += [pl.BlockSpec((past, kw), lambda b, i: (cache_block[1](b), 0)),
                     pl.BlockSpec((vr, past), lambda b, i: (0, cache_block[1](b)))]
        args += list(cache)
    in_specs += [pl.BlockSpec((seq, kw), lambda b, i: (sb0 + b, 0)),
                 pl.BlockSpec((vr, seq), lambda b, i: (0, sb0 + b))]
    args += [kn, vn]
    return pl.pallas_call(
        functools.partial(_attn_kernel, groups=groups, has_cache=cache is not None),
        grid=(n_seq, nq),
        in_specs=in_specs,
        out_specs=pl.BlockSpec((tq, 512), lambda b, i: (b * nq + i, 0)),
        out_shape=jax.ShapeDtypeStruct((n_seq * seq, 512), BF16),
        compiler_params=_cparams(("parallel", "parallel")),
        name="attention",
    )(*args)


GROUPS_A = tuple((h // 2, 2 * (h // 4) + (h % 2), (h // 4) * HD) for h in range(HEADS))
GROUPS_B = tuple((h, h, h * HD) for h in range(HEADS))


def _prep_c_kernel(x_ref, prev_ref, next_ref, ab_ref, w_ref, alog_ref, dtb_ref, ones_ref, exp_ref, tri_ref,
                   q_ref, k_ref, v_ref, gf_ref, gb_ref, bf_ref, bb_ref, *, blocks_per_seq):
    j = pl.program_id(0) % blocks_per_seq
    is_first, is_last = j == 0, j == blocks_per_seq - 1
    x = x_ref[...]
    rows = x.shape[0]
    rid = lax.broadcasted_iota(jnp.int32, x.shape, 0)
    halo_p = jnp.where(is_first, 0.0, prev_ref[7:8, :])
    halo_n = jnp.where(is_last, 0.0, next_ref[0:1, :])
    x_prev = jnp.where(rid == 0, halo_p, pltpu.roll(x, 1, 0))
    x_next = jnp.where(rid == rows - 1, halo_n, pltpu.roll(x, rows - 1, 0))
    w = w_ref[...]
    s = _silu(x_prev * w[0:1] + x * w[1:2] + x_next * w[2:3])
    ones = ones_ref[...]
    hw = HEADS * HD
    q, k = s[:, :hw], s[:, hw:2 * hw]
    q_ref[...] = q * lax.rsqrt(_group_sums(q * q, ones) + EPS) * (HD ** -0.5)
    k_ref[...] = k * lax.rsqrt(_group_sums(k * k, ones) + EPS)
    v_ref[...] = s[:, 2 * hw:]
    ab = ab_ref[...]
    t = ab + dtb_ref[...]
    softplus = jnp.maximum(t, 0.0) + jnp.log1p(jnp.exp(-jnp.abs(t)))
    g = -jnp.exp(alog_ref[...]) * softplus
    lane = lax.broadcasted_iota(jnp.int32, ab.shape, 1)
    parts = _split3(g)
    cum_f = sum(_dot(tri_ref[0], part) for part in parts)
    cum_b = sum(_dot(tri_ref[1], part) for part in parts)
    both = jnp.where(lane < 40, cum_f, jnp.where(lane < 48, cum_b, jax.nn.sigmoid(ab)))
    e = _dot3(both, exp_ref[...])
    gf_ref[...] = e[:, 0 * hw:1 * hw]
    gb_ref[...] = e[:, 1 * hw:2 * hw]
    bf_ref[...] = e[:, 2 * hw:3 * hw]
    bb_ref[...] = e[:, 3 * hw:4 * hw]


def _expand_matrix():
    e = np.zeros((LANES, 4 * HEADS * HD), np.float32)
    for kind in range(2):
        for d in range(2):
            for h in range(HEADS):
                src = 32 + 16 * kind + 8 * d + h
                dst = (2 * kind + d) * HEADS * HD + h * HD
                e[src, dst:dst + HD] = 1.0
    return jnp.asarray(e, BF16)


def _chunk_tri():
    r = np.arange(TOK)
    same = (r[:, None] // CHUNK) == (r[None, :] // CHUNK)
    low = same & (r[:, None] >= r[None, :])
    up = same & (r[:, None] <= r[None, :])
    return jnp.asarray(np.stack([low, up]).astype(np.float32), BF16)


def _prep_c(y, w_conv, alog, dtb, row0, n_rows, seq):
    nb8 = TOK // 8
    b0 = row0 // TOK
    last8 = (row0 + n_rows) // 8 - 1
    qkv_blk = C_QKV // CONV_CH
    out = pl.BlockSpec((TOK, HEADS * HD), lambda i: (i, 0))
    vec = pl.BlockSpec((1, LANES), lambda i: (0, 0))
    return pl.pallas_call(
        functools.partial(_prep_c_kernel, blocks_per_seq=seq // TOK),
        grid=(n_rows // TOK,),
        in_specs=[pl.BlockSpec((TOK, CONV_CH), lambda i: (b0 + i, qkv_blk)),
                  pl.BlockSpec((8, CONV_CH), lambda i: (jnp.maximum((b0 + i) * nb8 - 1, 0), qkv_blk)),
                  pl.BlockSpec((8, CONV_CH), lambda i: (jnp.minimum((b0 + i + 1) * nb8, last8), qkv_blk)),
                  pl.BlockSpec((TOK, LANES), lambda i: (b0 + i, (C_B + 640) // LANES)),
                  pl.BlockSpec((3, CONV_CH), lambda i: (0, 0)),
                  vec, vec,
                  pl.BlockSpec((256, 256), lambda i: (0, 0)),
                  pl.BlockSpec((LANES, 4 * HEADS * HD), lambda i: (0, 0)),
                  pl.BlockSpec((2, TOK, TOK), lambda i: (0, 0, 0))],
        out_specs=[out] * 7,
        out_shape=[jax.ShapeDtypeStruct((n_rows, HEADS * HD), F32)] * 7,
        compiler_params=_cparams(("parallel",)),
        name="prep_c",
    )(y, y, y, y, w_conv, alog, dtb, _ones_blockdiag(256, HD), _expand_matrix(), _chunk_tri())


def _gdn_kernel(*refs, has_s0, n_chunks, n_par):
    ins = refs[:10]
    pos = 10
    if has_s0:
        s0_refs = refs[pos:pos + 2]
        pos += 2
    o_refs = refs[pos:pos + 2]
    s_out_refs = refs[pos + 2:pos + 4]
    s_scr = refs[pos + 4]
    c = pl.program_id(1)

    @pl.when(c == 0)
    def _():
        for d in range(2):
            s_scr[d] = s0_refs[d][...] if has_s0 else jnp.zeros(s_scr.shape[1:], F32)

    n2 = 2 * CHUNK
    ri = lax.broadcasted_iota(jnp.int32, (n2, n2), 0)
    ci = lax.broadcasted_iota(jnp.int32, (n2, n2), 1)
    same = (ri // CHUNK) == (ci // CHUNK)
    left = lax.broadcasted_iota(jnp.int32, (CHUNK, n2), 1) < CHUNK
    eye = (ri == ci).astype(F32)

    def level_mask(lvl, lower):
        bit = 1 << lvl
        hi, lo = (ri, ci) if lower else (ci, ri)
        return ((ri >> (lvl + 1)) == (ci >> (lvl + 1))) & ((hi & bit) != 0) & ((lo & bit) == 0)

    levels = [[level_mask(lvl, d == 0) for lvl in range(6)] for d in range(2)]

    def block_diag(x):
        return jnp.concatenate([jnp.where(left, x, 0.0), jnp.where(left, 0.0, x)], axis=0)

    def anti_diag(x):
        xr = pltpu.roll(x, CHUNK, 1)
        return jnp.concatenate([jnp.where(left, 0.0, xr), jnp.where(left, xr, 0.0)], axis=0)

    chains = [(d, n, p) for n in range(n_par) for d in range(2) for p in range(HEADS // 2)]
    incl = [same & ((ri >= ci) if d == 0 else (ri <= ci)) for d in range(2)]
    strict = [same & ((ri > ci) if d == 0 else (ri < ci)) for d in range(2)]

    def load(which):
        return [ins[5 * d + which][n, 0, :, p * n2:(p + 1) * n2] for d, n, p in chains]

    q, k, v, gc, beta = load(0), load(1), load(2), load(3), load(4)
    gtot = [gi[CHUNK - 1:CHUNK, :] if d == 0 else gi[0:1, :] for (d, _, _), gi in zip(chains, gc)]
    eg = [jnp.exp(gi) for gi in gc]
    kb = [ki * bi for ki, bi in zip(k, beta)]
    decay = []
    for (d, _, _), gi in zip(chains, gc):
        gcr = pltpu.roll(gi, CHUNK, 1)
        gcol = jnp.concatenate([jnp.where(left, gi, gcr), jnp.where(left, gcr, gi)], axis=0)
        diff = gcol - gcol.T
        decay.append(jnp.where(incl[d], jnp.exp(jnp.where(incl[d], diff, 0.0)), 0.0))
    k_bd = [block_diag(ki).astype(BF16) for ki in k]
    kk = [_dot_nt(block_diag(kbi).astype(BF16), kbd) for kbi, kbd in zip(kb, k_bd)]
    qk = [_dot_nt(block_diag(qi).astype(BF16), kbd) for qi, kbd in zip(q, k_bd)]
    a_mat = [jnp.where(strict[d], kki * di, 0.0) for (d, _, _), kki, di in zip(chains, kk, decay)]
    qk = [(qki * di).astype(BF16) for qki, di in zip(qk, decay)]
    x = []
    for vi, bi, kbi, egi in zip(v, beta, kb, eg):
        vb = vi * bi
        kbe_r = pltpu.roll(kbi * egi, CHUNK, 1)
        x.append(jnp.concatenate([jnp.where(left, vb, kbe_r), jnp.where(left, kbe_r, vb)], axis=0))
    corr = [-jnp.where(levels[d][0], ai, 0.0) for (d, _, _), ai in zip(chains, a_mat)]
    for lvl in range(1, 6):
        d_b = [(eye + ci_).astype(BF16) for ci_ in corr]
        t = [_dot(jnp.where(levels[d][lvl], ai, 0.0).astype(BF16), di).astype(BF16)
             for (d, _, _), ai, di in zip(chains, a_mat, d_b)]
        corr = [ci_ - _dot(di, ti) for ci_, di, ti in zip(corr, d_b, t)]
    x = [xi + _dot(ci_.astype(BF16), xi.astype(BF16)) for xi, ci_ in zip(x, corr)]
    s = [s_scr[d, n, p] for d, n, p in chains]
    s_b = [si.astype(BF16) for si in s]
    ws = [_dot(jnp.where(same, 0.0, xi).astype(BF16), sbi) for xi, sbi in zip(x, s_b)]
    v_new = [(jnp.where(same, xi, 0.0) - wi).astype(BF16) for xi, wi in zip(x, ws)]
    o_s = [_dot(anti_diag(qi * egi).astype(BF16), sbi) for qi, egi, sbi in zip(q, eg, s_b)]
    o_v = [_dot(qki, vni) for qki, vni in zip(qk, v_new)]
    kd = [anti_diag(ki * jnp.exp(gt - gi)).astype(BF16) for ki, gt, gi in zip(k, gtot, gc)]
    s_upd = [_dot_tn(kdi, vni) for kdi, vni in zip(kd, v_new)]
    for i, (d, n, p) in enumerate(chains):
        o = o_s[i] + o_v[i]
        o_refs[d][n, 0, :, p * n2:(p + 1) * n2] = o[:CHUNK] + o[CHUNK:]
        s_scr[d, n, p] = s[i] * jnp.exp(gtot[i]) + s_upd[i]

    @pl.when(c == n_chunks - 1)
    def _():
        for d in range(2):
            s_out_refs[d][...] = s_scr[d]


GDN_SEQS_PER_STEP = 2


def _gdn_scan(qn, kn, v, gcf, gcb, bf, bb, s0, n_seq, seq):
    nc = seq // CHUNK
    hw = HEADS * HD
    n2 = 2 * CHUNK
    n_par = GDN_SEQS_PER_STEP if n_seq % GDN_SEQS_PER_STEP == 0 else 1
    view = lambda a: a.reshape(n_seq, nc, CHUNK, hw)
    fwd = pl.BlockSpec((n_par, 1, CHUNK, hw), lambda b, c: (b, c, 0, 0))
    bwd = pl.BlockSpec((n_par, 1, CHUNK, hw), lambda b, c: (b, nc - 1 - c, 0, 0))
    state = pl.BlockSpec((n_par, HEADS // 2, n2, n2), lambda b, c: (b, 0, 0, 0))
    in_specs = [fwd] * 5 + [bwd] * 5
    args = [view(a) for a in (qn, kn, v, gcf, bf, qn, kn, v, gcb, bb)]
    if s0 is not None:
        in_specs += [state, state]
        args += list(s0)
    st_shape = jax.ShapeDtypeStruct((n_seq, HEADS // 2, n2, n2), F32)
    o_shape = jax.ShapeDtypeStruct((n_seq, nc, CHUNK, hw), F32)
    o_f, o_b, s_f, s_b = pl.pallas_call(
        functools.partial(_gdn_kernel, has_s0=s0 is not None, n_chunks=nc, n_par=n_par),
        grid=(n_seq // n_par, nc),
        in_specs=in_specs,
        out_specs=[fwd, bwd, state, state],
        out_shape=[o_shape, o_shape, st_shape, st_shape],
        scratch_shapes=[pltpu.VMEM((2, n_par, HEADS // 2, n2, n2), F32)],
        compiler_params=_cparams(("parallel", "arbitrary")),
        name="gdn_scan",
    )(*args)
    return o_f.reshape(n_seq * seq, hw), o_b.reshape(n_seq * seq, hw), s_f, s_b


def _state_to_pairs(s):
    b = s.shape[0]
    s = s.reshape(b, HEADS // 2, 2, HD, HD)
    z = jnp.zeros_like(s[:, :, 0])
    top = jnp.concatenate([z, s[:, :, 1]], axis=-1)
    bot = jnp.concatenate([s[:, :, 0], z], axis=-1)
    return jnp.concatenate([top, bot], axis=-2)


def _pairs_to_state(s):
    b = s.shape[0]
    even = s[:, :, HD:, :HD]
    odd = s[:, :, :HD, HD:]
    return jnp.stack([even, odd], axis=2).reshape(b, HEADS, HD, HD)


def _merge_kernel(x_ref, mod_ref, *refs, ctx_blocks):
    pairs, (n1_ref, wgz_ref, gain_ref, ones_ref, wa_ref, wb_ref, wc_ref, wo_ref, o_ref) = refs[:8], refs[8:]
    is_ctx = pl.program_id(0) < ctx_blocks
    ya, yb, o_f, o_b = [jnp.where(is_ctx, pairs[2 * t][...], pairs[2 * t + 1][...]) for t in range(4)]
    m = mod_ref[0, 0]
    chunks = [slice(r, r + ROW_CHUNK) for r in range(0, x_ref.shape[0], ROW_CHUNK)]
    pa = [_dot(ya[c], wa_ref[0]) for c in chunks]
    pb = [_dot(yb[c], wb_ref[0]) for c in chunks]
    x = [x_ref[c, :] for c in chunks]
    h = [(_rms_rows(xc, n1_ref[...]) * (1.0 + m[1:2]) + m[0:1]).astype(BF16) for xc in x]
    gz = [_dot(hc, wgz_ref[0]) for hc in h]
    yc = []
    for c, gzc in zip(chunks, gz):
        o = o_f[c] + o_b[c]
        ms = _group_sums(o * o, ones_ref[...]) * (1.0 / HD)
        yc.append((o * lax.rsqrt(ms + EPS) * gain_ref[...] * _silu(gzc[:, 3 * D_MODEL:])).astype(BF16))
    pc = [_dot(ycc, wc_ref[0]) for ycc in yc]
    merged = [(jax.nn.sigmoid(g[:, :D_MODEL]) * a + jax.nn.sigmoid(g[:, D_MODEL:2 * D_MODEL]) * b
               + jax.nn.sigmoid(g[:, 2 * D_MODEL:3 * D_MODEL]) * cc).astype(BF16)
              for g, a, b, cc in zip(gz, pa, pb, pc)]
    mix = [_dot(mc, wo_ref[0]) for mc in merged]
    for c, xc, mixc in zip(chunks, x, mix):
        o_ref[c, :] = xc + m[2:3] * mixc


def _merge(x, mod, layer, mixers, norm_gain, w_gz, gain, wa, wb, wc, wo, mod_row, ctx_rows):
    m_rows = x.shape[0]
    tm = TOK
    nb_ctx = ctx_rows // tm
    nb_lat = (m_rows - ctx_rows) // tm
    ctx_half = pl.BlockSpec((tm, 512), lambda i: (jnp.minimum(i, nb_ctx - 1), 0))
    lat_half = pl.BlockSpec((tm, 512), lambda i: (jnp.clip(i - nb_ctx, 0, nb_lat - 1), 0))
    full = pl.BlockSpec((tm, D_MODEL), lambda i: (i, 0))

    def wspec(k):
        return pl.BlockSpec((1, k, D_MODEL), lambda i: (layer, 0, 0))

    return pl.pallas_call(
        functools.partial(_merge_kernel, ctx_blocks=nb_ctx),
        grid=(m_rows // tm,),
        in_specs=[full,
                  pl.BlockSpec((1, 1, N_MOD, D_MODEL), lambda i: (layer, mod_row(i, tm), 0, 0))]
        + [ctx_half, lat_half] * 4
        + [pl.BlockSpec((1, D_MODEL), lambda i: (0, 0)),
           pl.BlockSpec((1, D_MODEL, N_GZ), lambda i: (layer, 0, 0)),
           pl.BlockSpec((1, 512), lambda i: (0, 0)),
           pl.BlockSpec((256, 256), lambda i: (0, 0)),
           wspec(512), wspec(512), wspec(512), wspec(D_MODEL)],
        out_specs=full,
        out_shape=jax.ShapeDtypeStruct((m_rows, D_MODEL), F32),
        compiler_params=_cparams(("parallel",)),
        name="merge_out",
    )(x, mod, *mixers, norm_gain, w_gz, gain, _ones_blockdiag(256, HD), wa, wb, wc, wo)


def _ffn_kernel(x_ref, mod_ref, g_ref, wg_ref, wu_ref, wd_ref, fin_ref, o_ref, *, final):
    m = mod_ref[0, 0]
    chunks = [slice(r, r + ROW_CHUNK) for r in range(0, x_ref.shape[0], ROW_CHUNK)]
    x = [x_ref[c, :] for c in chunks]
    h = [(_rms_rows(xc, g_ref[...]) * (1.0 + m[4:5]) + m[3:4]).astype(BF16) for xc in x]
    gate = [_dot(hc, wg_ref[0]) for hc in h]
    up = [_dot(hc, wu_ref[0]) for hc in h]
    act = [(_silu(gc) * uc).astype(BF16) for gc, uc in zip(gate, up)]
    down = [_dot(ac, wd_ref[0]) for ac in act]
    for c, xc, dc in zip(chunks, x, down):
        out = xc + m[5:6] * dc
        if final:
            out = _rms_rows(out, fin_ref[...])
        o_ref[c, :] = out


def _ffn(x, mod, layer, gain, wg, wu, wd, fin, mod_row, final):
    m_rows = x.shape[0]
    tm = TOK
    full = pl.BlockSpec((tm, D_MODEL), lambda i: (i, 0))
    vec = pl.BlockSpec((1, D_MODEL), lambda i: (0, 0))
    once = pl.Buffered(1)
    return pl.pallas_call(
        functools.partial(_ffn_kernel, final=final),
        grid=(m_rows // tm,),
        in_specs=[full,
                  pl.BlockSpec((1, 1, N_MOD, D_MODEL), lambda i: (layer, mod_row(i, tm), 0, 0)),
                  vec,
                  pl.BlockSpec((1, D_MODEL, D_FF), lambda i: (layer, 0, 0), pipeline_mode=once),
                  pl.BlockSpec((1, D_MODEL, D_FF), lambda i: (layer, 0, 0), pipeline_mode=once),
                  pl.BlockSpec((1, D_FF, D_MODEL), lambda i: (layer, 0, 0), pipeline_mode=once),
                  vec],
        out_specs=full,
        out_shape=jax.ShapeDtypeStruct((m_rows, D_MODEL), F32),
        compiler_params=_cparams(("parallel",)),
        name="ffn",
    )(x, mod, gain, wg, wu, wd, fin)


def _layout_w_in(w_in):
    def cols(a, b):
        return w_in[:, :, a:b]

    pad = jnp.zeros(w_in.shape[:2] + (64,), w_in.dtype)
    parts = [cols(1440, 2976), cols(768, 1440), cols(3488, 3520), pad, cols(512, 768), cols(0, 512)]
    w_tok = jnp.concatenate(parts, axis=-1).astype(BF16)
    w_gz = jnp.concatenate([cols(3520, 6592), cols(2976, 3488)], axis=-1).astype(BF16)
    return w_tok, w_gz


def _layout_w_uq(w_uq):
    n_layers = w_uq.shape[0]
    w = w_uq.reshape(n_layers, Q_LORA, HEADS, NOPE_B + ROPE_B)
    w = jnp.pad(w, ((0, 0), (0, 0), (0, 0), (0, LANES - NOPE_B - ROPE_B)))
    return w.reshape(n_layers, Q_LORA, HEADS * LANES).astype(BF16)


def _layout_w_ukv(w_ukv):
    n_layers = w_ukv.shape[0]
    w = w_ukv.reshape(n_layers, KV_LORA, HEADS, 2, HD)
    zero = jnp.zeros((n_layers, KV_LORA, HEADS, HD), w.dtype)
    wk_top = jnp.concatenate([w[:, :, :, 0], zero], axis=-1).reshape(n_layers, KV_LORA, HEADS * LANES)
    place = np.zeros((LANES, HEADS, LANES), np.float32)
    for d in range(ROPE_B):
        place[d, :, NOPE_B + d] = 1.0
    place = jnp.broadcast_to(jnp.asarray(place.reshape(LANES, HEADS * LANES)), (n_layers, LANES, HEADS * LANES))
    wk = jnp.concatenate([wk_top, place], axis=1).astype(BF16)
    wv = w[:, :, :, 1].reshape(n_layers, KV_LORA, HEADS * HD).astype(BF16)
    return wk, wv


def kernel(x_prompt, x_sample, cache_ka, cache_va, cache_ckv, cache_kpe, state_fwd, state_bwd, c, c_ctx,
           w_mod, b_mod, norm1, norm2, w_in, a_qnorm, a_knorm, b_qnorm, b_kvnorm, w_uq, w_ukv, c_conv,
           c_alog, c_dt_bias, c_onorm, w_pa, w_pb, w_pc, w_out, w_gate, w_up, w_down, final_norm):
    n_ctx, s_ctx, _ = x_prompt.shape
    n_lat, s_lat, _ = x_sample.shape
    n_layers = w_mod.shape[0]
    past = cache_ka.shape[2]
    ctx_rows = n_ctx * s_ctx
    lat_rows = n_lat * s_lat
    assert s_ctx == TOK and s_lat % TOK == 0 and ctx_rows % 512 == 0 and s_lat % 512 == 0
    assert past % TOK == 0 and n_lat + 1 <= 8 and ctx_rows % s_lat == 0

    def mod_row(i, tm):
        return jnp.where(i < ctx_rows // tm, 0, 1 + (i - ctx_rows // tm) // (s_lat // tm))

    def tab_row(i):
        nb = ctx_rows // TOK
        return jnp.where(i < nb, 0, 1 + (i - nb) % (s_lat // TOK))

    tm_in = 512

    x = jnp.concatenate([x_prompt.reshape(ctx_rows, D_MODEL), x_sample.reshape(lat_rows, D_MODEL)], axis=0)
    cvec = jnp.zeros((8, D_MODEL), F32).at[0].set(c_ctx).at[1:1 + n_lat].set(c)
    mod = _modulation(cvec, w_mod, b_mod).reshape(n_layers, 8, N_MOD, D_MODEL)
    cos, sin = _rope_tables(s_lat)

    w_in_p, w_gz = _layout_w_in(w_in)
    w_uq_p = _layout_w_uq(w_uq)
    w_k_p, w_v_p = _layout_w_ukv(w_ukv)
    bf = lambda w: w.astype(BF16)
    w_pa_b, w_pb_b, w_pc_b, w_out_b = bf(w_pa), bf(w_pb), bf(w_pc), bf(w_out)
    w_gate_b, w_up_b, w_down_b = bf(w_gate), bf(w_up), bf(w_down)

    cache_rows = n_lat * n_layers * past
    ck4, cv4 = _cache_a(cache_ka.reshape(cache_rows, LANES), cache_va.reshape(cache_rows, LANES))
    kpe_pad = jnp.pad(cache_kpe.reshape(cache_rows, ROPE_B), ((0, 0), (0, LANES - ROPE_B)))
    ckb, cvb = _cache_b(cache_ckv.reshape(cache_rows, KV_LORA), kpe_pad, w_k_p, w_v_p,
                        lambda i: (i // (past // TOK)) % n_layers)

    lane_vec = lambda v16: jnp.zeros((1, LANES), F32).at[0, 32:48].set(v16.reshape(16))
    new = []
    for l in range(n_layers):
        y = _in_proj(x, mod, l, norm1[l][None], w_in_p, mod_row, tm_in)

        qa, k4, v4, kn_a = _prep_a(y, cos, sin, jnp.tile(a_qnorm[l], HEADS)[None],
                                   jnp.tile(a_knorm[l], KV_HEADS_A)[None], tab_row)
        qb, kb, vb, ckvn = _prep_b(y, cos, sin, b_qnorm[l][None], b_kvnorm[l][None],
                                   w_uq_p, w_k_p, w_v_p, l, tab_row)
        conv_args = (c_conv[l, :, 0, :], lane_vec(c_alog[l]), lane_vec(c_dt_bias[l]))
        gdn_ctx = _prep_c(y, *conv_args, 0, ctx_rows, s_ctx)
        gdn_lat = _prep_c(y, *conv_args, ctx_rows, lat_rows, s_lat)

        cache_blk = (past, lambda b, l=l: b * n_layers + l)
        ya_c = _attention(qa, k4, v4, None, GROUPS_A, 0, n_ctx, s_ctx, s_ctx, None)
        ya_l = _attention(qa, k4, v4, (ck4, cv4), GROUPS_A, ctx_rows, n_lat, s_lat, 256, cache_blk)
        yb_c = _attention(qb, kb, vb, None, GROUPS_B, 0, n_ctx, s_ctx, s_ctx, None)
        yb_l = _attention(qb, kb, vb, (ckb, cvb), GROUPS_B, ctx_rows, n_lat, s_lat, 256, cache_blk)

        of_c, ob_c, sf_c, sb_c = _gdn_scan(*gdn_ctx, None, n_ctx, s_ctx)
        s0 = (_state_to_pairs(state_fwd[:, l]), _state_to_pairs(state_bwd[:, l]))
        of_l, ob_l, _, _ = _gdn_scan(*gdn_lat, s0, n_lat, s_lat)

        x = _merge(x, mod, l, (ya_c, ya_l, yb_c, yb_l, of_c, of_l, ob_c, ob_l), norm1[l][None], w_gz,
                   jnp.tile(c_onorm[l], HEADS)[None], w_pa_b, w_pb_b, w_pc_b, w_out_b, mod_row, ctx_rows)
        x = _ffn(x, mod, l, norm2[l][None], w_gate_b, w_up_b, w_down_b, final_norm[None], mod_row,
                 final=(l == n_layers - 1))

        new.append((kn_a[:ctx_rows].reshape(n_ctx, s_ctx, KV_HEADS_A, HD),
                    y[:ctx_rows, C_VA:C_VA + LANES].reshape(n_ctx, s_ctx, KV_HEADS_A, HD),
                    ckvn[:ctx_rows].reshape(n_ctx, s_ctx, KV_LORA),
                    y[:ctx_rows, C_B + 640:C_B + 640 + ROPE_B].reshape(n_ctx, s_ctx, ROPE_B),
                    _pairs_to_state(sf_c), _pairs_to_state(sb_c)))

    y_prompt = x[:ctx_rows].reshape(n_ctx, s_ctx, D_MODEL)
    y_sample = x[ctx_rows:].reshape(n_lat, s_lat, D_MODEL)
    stacked = [jnp.stack([n[i] for n in new], axis=1) for i in range(6)]
    return (y_prompt, y_sample) + tuple(stacked)
```

```python
import functools

import numpy as np
import jax
import jax.numpy as jnp
from jax import lax
from jax.experimental import pallas as pl
from jax.experimental.pallas import tpu as pltpu

F32 = jnp.float32
BF16 = jnp.bfloat16

D_MODEL = 1024
GRID_W = 64
GRID_W_LOG2 = 6
assert 1 << GRID_W_LOG2 == GRID_W
ROPE_THETA = 10000.0
EPS = 1e-6
N_MOD = 6
HEADS = 8
KV_HEADS_A = 2
HD = 64
Q_LORA = 384
KV_LORA = 256
NOPE_B = 64
ROPE_B = 32
CHUNK = 64
CONV_CH = 3 * HEADS * HD
D_FF = 2816

C_QKV = 0
C_B = 1536
C_KA = 2304
C_VA = 2432
C_QA = 2560
N_IN = 3072
N_GZ = 3 * D_MODEL + HEADS * HD

LOG2E = 1.4426950408889634
LANES = 128
TOK = 256
TOK_AB = 512
TM_DENSE = 512
ROW_CHUNK = 128
VMEM_LIMIT = 56 * 1024 * 1024


def _cparams(sem):
    return pltpu.CompilerParams(dimension_semantics=sem, vmem_limit_bytes=VMEM_LIMIT)


def _dot(a, b):
    return jnp.dot(a, b, preferred_element_type=F32)


def _dot_nt(a, b):
    return lax.dot_general(a, b, (((1,), (1,)), ((), ())), preferred_element_type=F32)


def _dot_tn(a, b):
    return lax.dot_general(a, b, (((0,), (0,)), ((), ())), preferred_element_type=F32)


def _split3(x):
    hi = x.astype(BF16)
    r = x - hi.astype(F32)
    mid = r.astype(BF16)
    lo = (r - mid.astype(F32)).astype(BF16)
    return hi, mid, lo


def _dot3(x, w):
    hi, mid, lo = _split3(x)
    return _dot(hi, w) + _dot(mid, w) + _dot(lo, w)


def _group_sums(x, ones):
    w = min(x.shape[1], ones.shape[0])
    blk = ones[:w, :w]
    slabs = []
    for s in range(x.shape[1] // w):
        xs = x[:, s * w:(s + 1) * w]
        hi = xs.astype(BF16)
        lo = (xs - hi.astype(F32)).astype(BF16)
        slabs.append(_dot(hi, blk) + _dot(lo, blk))
    return slabs[0] if len(slabs) == 1 else jnp.concatenate(slabs, axis=1)


def _silu(x):
    return x * jax.nn.sigmoid(x)


def _rms_rows(x, g):
    return x * lax.rsqrt(jnp.mean(x * x, axis=-1, keepdims=True) + EPS) * g


def _swap_pairs(x, half):
    lane = lax.broadcasted_iota(jnp.int32, x.shape, 1)
    n = x.shape[1]
    return jnp.where((lane & half) == 0, pltpu.roll(x, n - half, 1), pltpu.roll(x, half, 1))


def _rope_cols(x, cos, sin, half):
    outs = []
    for s in range(x.shape[1] // LANES):
        xs = x[:, s * LANES:(s + 1) * LANES]
        outs.append(xs * cos + _swap_pairs(xs, half) * sin)
    return outs[0] if len(outs) == 1 else jnp.concatenate(outs, axis=1)


def _mod_kernel(c_ref, w_ref, b_ref, o_ref):
    s = _silu(c_ref[...]).astype(BF16)
    o_ref[0] = _dot(s, w_ref[0].astype(BF16)) + b_ref[0]


def _modulation(cvec, w_mod, b_mod):
    n_layers = w_mod.shape[0]
    n_out = w_mod.shape[2]
    tn = 1536
    return pl.pallas_call(
        _mod_kernel,
        grid=(n_layers, n_out // tn),
        in_specs=[pl.BlockSpec((8, D_MODEL), lambda l, j: (0, 0)),
                  pl.BlockSpec((1, D_MODEL, tn), lambda l, j: (l, 0, j)),
                  pl.BlockSpec((1, 1, tn), lambda l, j: (l, 0, j))],
        out_specs=pl.BlockSpec((1, 8, tn), lambda l, j: (l, 0, j)),
        out_shape=jax.ShapeDtypeStruct((n_layers, 8, n_out), F32),
        compiler_params=_cparams(("parallel", "parallel")),
        name="modulation",
    )(cvec, w_mod, b_mod.reshape(n_layers, 1, n_out))


def _in_proj_kernel(x_ref, mod_ref, g_ref, w_ref, o_ref):
    m = mod_ref[0, 0]
    for r in range(0, x_ref.shape[0], ROW_CHUNK):
        y = _rms_rows(x_ref[r:r + ROW_CHUNK, :], g_ref[...])
        h = (y * (1.0 + m[1:2]) + m[0:1]).astype(BF16)
        o_ref[r:r + ROW_CHUNK, :] = _dot(h, w_ref[0])


def _in_proj(x, mod, layer, gain, w, mod_row, tm):
    m_rows = x.shape[0]
    return pl.pallas_call(
        _in_proj_kernel,
        grid=(m_rows // tm,),
        in_specs=[pl.BlockSpec((tm, D_MODEL), lambda i: (i, 0)),
                  pl.BlockSpec((1, 1, N_MOD, D_MODEL), lambda i: (layer, mod_row(i, tm), 0, 0)),
                  pl.BlockSpec((1, D_MODEL), lambda i: (0, 0)),
                  pl.BlockSpec((1, D_MODEL, N_IN), lambda i: (layer, 0, 0))],
        out_specs=pl.BlockSpec((tm, N_IN), lambda i: (i, 0)),
        out_shape=jax.ShapeDtypeStruct((m_rows, N_IN), F32),
        compiler_params=_cparams(("parallel",)),
        name="in_proj",
    )(x, mod, gain, w)


def _rope_table_kernel(freq_ref, col_ref, sign_ref, cos_ref, sin_ref):
    shape = cos_ref.shape
    pos = pl.program_id(1) * TOK_AB + lax.broadcasted_iota(jnp.int32, shape, 1) - TOK_AB
    grid_row = (pos >> GRID_W_LOG2).astype(F32)
    grid_col = (pos & (GRID_W - 1)).astype(F32)
    a = jnp.where(col_ref[...] > 0.5, grid_col, grid_row) * freq_ref[...]
    a = jnp.where(pos >= 0, a, 0.0)
    cos_ref[...] = jnp.cos(a)
    sin_ref[...] = jnp.sin(a) * sign_ref[...]


def _rope_tables(n_lat):
    lane = np.arange(LANES)

    def lane_pattern(nf, d, active):
        inv_freq = np.float32(ROPE_THETA) ** (-np.arange(nf, dtype=np.float32) / np.float32(nf))
        dd = np.where(active, d, 0)
        freq = np.where(active, inv_freq[dd % nf], 0.0).astype(np.float32)
        use_col = (dd >= 2 * nf).astype(np.float32)
        sign = np.where((dd % (2 * nf)) < nf, -1.0, 1.0).astype(np.float32)
        return freq, use_col, sign

    d_q = lane - NOPE_B
    patterns = [lane_pattern(HD // 4, lane % HD, np.ones(LANES, bool)),
                lane_pattern(ROPE_B // 4, d_q, (d_q >= 0) & (d_q < ROPE_B)),
                lane_pattern(ROPE_B // 4, lane, lane < ROPE_B)]
    freq, use_col, sign = [jnp.asarray(np.stack([p[t] for p in patterns]))[:, None, :] for t in range(3)]
    rows = TOK_AB + n_lat
    spec = pl.BlockSpec((1, TOK_AB, LANES), lambda a, i: (a, i, 0))
    vec = pl.BlockSpec((1, 1, LANES), lambda a, i: (a, 0, 0))
    return pl.pallas_call(
        _rope_table_kernel,
        grid=(3, rows // TOK_AB),
        in_specs=[vec, vec, vec],
        out_specs=[spec, spec],
        out_shape=[jax.ShapeDtypeStruct((3, rows, LANES), F32)] * 2,
        compiler_params=_cparams(("parallel", "parallel")),
        name="rope_tables",
    )(freq, use_col, sign)


def _prep_a_kernel(qa_ref, kv_ref, cos_ref, sin_ref, gq_ref, gk_ref, ones_ref, place_ref,
                   q_ref, k4_ref, vt_ref, kn_ref):
    cos = cos_ref[0]
    sin = sin_ref[0]
    ones = ones_ref[...]
    q = qa_ref[...]
    qn = q * lax.rsqrt(_group_sums(q * q, ones) * (1.0 / HD) + EPS) * gq_ref[...]
    q_ref[...] = (_rope_cols(qn, cos, sin, HD // 4) * (HD ** -0.5 * LOG2E)).astype(BF16)
    k = kv_ref[:, :LANES]
    kn = k * lax.rsqrt(_group_sums(k * k, ones) * (1.0 / HD) + EPS) * gk_ref[...]
    kn_ref[...] = kn
    k4_ref[...] = _dot(_rope_cols(kn, cos, sin, HD // 4).astype(BF16), place_ref[...]).astype(BF16)
    vt_ref[...] = kv_ref[:, LANES:].T.astype(BF16)


def _place4():
    p = np.zeros((LANES, 4 * LANES), np.float32)
    for kv in range(KV_HEADS_A):
        for side in range(2):
            for d in range(HD):
                p[kv * HD + d, (2 * kv + side) * LANES + side * HD + d] = 1.0
    return jnp.asarray(p, BF16)


def _ones_blockdiag(n, group):
    idx = np.arange(n) // group
    return jnp.asarray((idx[:, None] == idx[None, :]).astype(np.float32), BF16)


def _prep_a(y, cos, sin, gq, gk, tab_row):
    m_rows = y.shape[0]
    wide = pl.BlockSpec((TOK_AB, 4 * LANES), lambda i: (i, 0))
    tab = pl.BlockSpec((1, TOK_AB, LANES), lambda i: (0, tab_row(i), 0))
    return pl.pallas_call(
        _prep_a_kernel,
        grid=(m_rows // TOK_AB,),
        in_specs=[pl.BlockSpec((TOK_AB, 512), lambda i: (i, C_QA // 512)),
                  pl.BlockSpec((TOK_AB, 256), lambda i: (i, C_KA // 256)),
                  tab, tab,
                  pl.BlockSpec((1, 512), lambda i: (0, 0)),
                  pl.BlockSpec((1, LANES), lambda i: (0, 0)),
                  pl.BlockSpec((256, 256), lambda i: (0, 0)),
                  pl.BlockSpec((LANES, 512), lambda i: (0, 0))],
        out_specs=[wide, wide, pl.BlockSpec((LANES, TOK_AB), lambda i: (0, i)),
                   pl.BlockSpec((TOK_AB, LANES), lambda i: (i, 0))],
        out_shape=[jax.ShapeDtypeStruct((m_rows, 512), BF16)] * 2
        + [jax.ShapeDtypeStruct((LANES, m_rows), BF16), jax.ShapeDtypeStruct((m_rows, LANES), F32)],
        compiler_params=_cparams(("parallel",)),
        name="prep_a",
    )(y, y, cos, sin, gq, gk, _ones_blockdiag(256, HD), _place4())


def _cache_a_kernel(k_ref, v_ref, place_ref, k4_ref, vt_ref):
    k4_ref[...] = _dot(k_ref[...].astype(BF16), place_ref[...]).astype(BF16)
    vt_ref[...] = v_ref[...].T.astype(BF16)


def _cache_a(ck, cv):
    rows = ck.shape[0]
    narrow = pl.BlockSpec((TOK, LANES), lambda i: (i, 0))
    return pl.pallas_call(
        _cache_a_kernel,
        grid=(rows // TOK,),
        in_specs=[narrow, narrow, pl.BlockSpec((LANES, 512), lambda i: (0, 0))],
        out_specs=[pl.BlockSpec((TOK, 512), lambda i: (i, 0)), pl.BlockSpec((LANES, TOK), lambda i: (0, i))],
        out_shape=[jax.ShapeDtypeStruct((rows, 512), BF16), jax.ShapeDtypeStruct((LANES, rows), BF16)],
        compiler_params=_cparams(("parallel",)),
        name="cache_a",
    )(ck, cv, _place4())


def _prep_b_kernel(g_ref, cq_ref, sq_ref, ck_ref, sk_ref, gq_ref, gkv_ref, wq_ref, wk_ref, wv_ref,
                   q_ref, k_ref, vt_ref, ckv_ref):
    g = g_ref[...]
    cqn = _rms_rows(g[:, :Q_LORA], gq_ref[...])
    q = _dot(cqn.astype(BF16), wq_ref[0])
    q = _rope_cols(q, cq_ref[0], sq_ref[0], ROPE_B // 4) * ((NOPE_B + ROPE_B) ** -0.5 * LOG2E)
    q_ref[...] = q.astype(BF16)
    ckvn = _rms_rows(g[:, Q_LORA:Q_LORA + KV_LORA], gkv_ref[...])
    ckv_ref[...] = ckvn
    kpe = _rope_cols(g[:, Q_LORA + KV_LORA:], ck_ref[0], sk_ref[0], ROPE_B // 4)
    ckv_b = ckvn.astype(BF16)
    kin = jnp.concatenate([ckv_b, kpe.astype(BF16)], axis=1)
    k_ref[...] = _dot(kin, wk_ref[0]).astype(BF16)
    vt_ref[...] = _dot(ckv_b, wv_ref[0]).T.astype(BF16)


def _prep_b(y, cos, sin, gq, gkv, wq, wk, wv, layer, tab_row):
    m_rows = y.shape[0]
    wide = pl.BlockSpec((TOK_AB, HEADS * LANES), lambda i: (i, 0))

    def tab(a):
        return pl.BlockSpec((1, TOK_AB, LANES), lambda i: (a, tab_row(i), 0))

    def wspec(k, n):
        return pl.BlockSpec((1, k, n), lambda i: (layer, 0, 0))

    return pl.pallas_call(
        _prep_b_kernel,
        grid=(m_rows // TOK_AB,),
        in_specs=[pl.BlockSpec((TOK_AB, 768), lambda i: (i, C_B // 768)),
                  tab(1), tab(1), tab(2), tab(2),
                  pl.BlockSpec((1, Q_LORA), lambda i: (0, 0)),
                  pl.BlockSpec((1, KV_LORA), lambda i: (0, 0)),
                  wspec(Q_LORA, HEADS * LANES), wspec(KV_LORA + LANES, HEADS * LANES),
                  wspec(KV_LORA, HEADS * HD)],
        out_specs=[wide, wide, pl.BlockSpec((HEADS * HD, TOK_AB), lambda i: (0, i)),
                   pl.BlockSpec((TOK_AB, KV_LORA), lambda i: (i, 0))],
        out_shape=[jax.ShapeDtypeStruct((m_rows, HEADS * LANES), BF16)] * 2
        + [jax.ShapeDtypeStruct((HEADS * HD, m_rows), BF16), jax.ShapeDtypeStruct((m_rows, KV_LORA), F32)],
        compiler_params=_cparams(("parallel",)),
        name="prep_b",
    )(y, cos, sin, cos, sin, gq, gkv, wq, wk, wv)


def _cache_b_kernel(c_ref, p_ref, wk_ref, wv_ref, k_ref, vt_ref):
    ckv_b = c_ref[...].astype(BF16)
    kin = jnp.concatenate([ckv_b, p_ref[...].astype(BF16)], axis=1)
    k_ref[...] = _dot(kin, wk_ref[0]).astype(BF16)
    vt_ref[...] = _dot(ckv_b, wv_ref[0]).T.astype(BF16)


def _cache_b(ckv, kpe, wk, wv, layer_of_block):
    rows = ckv.shape[0]
    return pl.pallas_call(
        _cache_b_kernel,
        grid=(rows // TOK,),
        in_specs=[pl.BlockSpec((TOK, KV_LORA), lambda i: (i, 0)),
                  pl.BlockSpec((TOK, LANES), lambda i: (i, 0)),
                  pl.BlockSpec((1, KV_LORA + LANES, HEADS * LANES), lambda i: (layer_of_block(i), 0, 0)),
                  pl.BlockSpec((1, KV_LORA, HEADS * HD), lambda i: (layer_of_block(i), 0, 0))],
        out_specs=[pl.BlockSpec((TOK, HEADS * LANES), lambda i: (i, 0)),
                   pl.BlockSpec((HEADS * HD, TOK), lambda i: (0, i))],
        out_shape=[jax.ShapeDtypeStruct((rows, HEADS * LANES), BF16),
                   jax.ShapeDtypeStruct((HEADS * HD, rows), BF16)],
        compiler_params=_cparams(("parallel",)),
        name="cache_b",
    )(ckv, kpe, wk, wv)


ATT_KEY_TILE = 256
ATT_LOOKAHEAD = 5


def _attn_kernel(*refs, groups, has_cache):
    if has_cache:
        q_ref, kc_ref, vc_ref, kn_ref, vn_ref, o_ref = refs
        segments = [(kc_ref, vc_ref), (kn_ref, vn_ref)]
    else:
        q_ref, kn_ref, vn_ref, o_ref = refs
        segments = [(kn_ref, vn_ref)]
    tiles = [(k_ref, v_ref, t * ATT_KEY_TILE) for k_ref, v_ref in segments
             for t in range(k_ref.shape[0] // ATT_KEY_TILE)]
    q = [q_ref[:, qg * LANES:(qg + 1) * LANES] for qg, _, _ in groups]

    def scores(h, i):
        k_ref, _, t0 = tiles[i]
        kg = groups[h][1]
        return _dot_nt(k_ref[t0:t0 + ATT_KEY_TILE, kg * LANES:(kg + 1) * LANES], q[h])

    heads = range(len(groups))
    jobs = [(i, h) for i in range(len(tiles)) for h in heads]
    pending = {j: scores(h, i) for j, (i, h) in enumerate(jobs[:ATT_LOOKAHEAD])}
    ones_rows = (lax.broadcasted_iota(jnp.int32, (16, ATT_KEY_TILE), 0) == 0).astype(BF16)
    m, acc = {}, {}
    for j, (i, h) in enumerate(jobs):
        if j + ATT_LOOKAHEAD < len(jobs):
            i_n, h_n = jobs[j + ATT_LOOKAHEAD]
            pending[j + ATT_LOOKAHEAD] = scores(h_n, i_n)
        s = pending.pop(j)
        _, v_ref, t0 = tiles[i]
        vrow = groups[h][2]
        v_t = jnp.concatenate([v_ref[vrow:vrow + HD, t0:t0 + ATT_KEY_TILE], ones_rows], axis=0)
        m_tile = jnp.max(s, axis=0, keepdims=True)
        if i == 0:
            m[h] = m_tile
            acc[h] = _dot(v_t, jnp.exp2(s - m_tile).astype(BF16))
        else:
            m_new = jnp.maximum(m[h], m_tile)
            alpha = jnp.exp2(m[h] - m_new)
            acc[h] = acc[h] * alpha + _dot(v_t, jnp.exp2(s - m_new).astype(BF16))
            m[h] = m_new
    outs = [acc[h][:HD] / acc[h][HD:HD + 1] for h in heads]
    o_ref[...] = jnp.concatenate(outs, axis=0).T.astype(BF16)


def _attention(q, kn, vn, cache, groups, row0, n_seq, seq, tq, cache_block):
    qw, kw, vr = q.shape[1], kn.shape[1], vn.shape[0]
    nq = seq // tq
    qb0, sb0 = row0 // tq, row0 // seq
    in_specs = [pl.BlockSpec((tq, qw), lambda b, i: (qb0 + b * nq + i, 0))]
    args = [q]
    if cache is not None:
        past = cache_block[0]
        in_specs += [pl.BlockSpec((past, kw), lambda b, i: (cache_block[1](b), 0)),
                     pl.BlockSpec((vr, past), lambda b, i: (0, cache_block[1](b)))]
        args += list(cache)
    in_specs += [pl.BlockSpec((seq, kw), lambda b, i: (sb0 + b, 0)),
                 pl.BlockSpec((vr, seq), lambda b, i: (0, sb0 + b))]
    args += [kn, vn]
    return pl.pallas_call(
        functools.partial(_attn_kernel, groups=groups, has_cache=cache is not None),
        grid=(n_seq, nq),
        in_specs=in_specs,
        out_specs=pl.BlockSpec((tq, 512), lambda b, i: (b * nq + i, 0)),
        out_shape=jax.ShapeDtypeStruct((n_seq * seq, 512), BF16),
        compiler_params=_cparams(("parallel", "parallel")),
        name="attention",
    )(*args)


GROUPS_A = tuple((h // 2, 2 * (h // 4) + (h % 2), (h // 4) * HD) for h in range(HEADS))
GROUPS_B = tuple((h, h, h * HD) for h in range(HEADS))


def _prep_c_kernel(x_ref, prev_ref, next_ref, ab_ref, w_ref, alog_ref, dtb_ref, ones_ref, exp_ref, tri_ref,
                   q_ref, k_ref, v_ref, gf_ref, gb_ref, bf_ref, bb_ref, *, blocks_per_seq):
    j = pl.program_id(0) % blocks_per_seq
    is_first, is_last = j == 0, j == blocks_per_seq - 1
    x = x_ref[...]
    rows = x.shape[0]
    rid = lax.broadcasted_iota(jnp.int32, x.shape, 0)
    halo_p = jnp.where(is_first, 0.0, prev_ref[7:8, :])
    halo_n = jnp.where(is_last, 0.0, next_ref[0:1, :])
    x_prev = jnp.where(rid == 0, halo_p, pltpu.roll(x, 1, 0))
    x_next = jnp.where(rid == rows - 1, halo_n, pltpu.roll(x, rows - 1, 0))
    w = w_ref[...]
    s = _silu(x_prev * w[0:1] + x * w[1:2] + x_next * w[2:3])
    ones = ones_ref[...]
    hw = HEADS * HD
    q, k = s[:, :hw], s[:, hw:2 * hw]
    q_ref[...] = q * lax.rsqrt(_group_sums(q * q, ones) + EPS) * (HD ** -0.5)
    k_ref[...] = k * lax.rsqrt(_group_sums(k * k, ones) + EPS)
    v_ref[...] = s[:, 2 * hw:]
    ab = ab_ref[...]
    t = ab + dtb_ref[...]
    softplus = jnp.maximum(t, 0.0) + jnp.log1p(jnp.exp(-jnp.abs(t)))
    g = -jnp.exp(alog_ref[...]) * softplus
    lane = lax.broadcasted_iota(jnp.int32, ab.shape, 1)
    parts = _split3(g)
    cum_f = sum(_dot(tri_ref[0], part) for part in parts)
    cum_b = sum(_dot(tri_ref[1], part) for part in parts)
    both = jnp.where(lane < 40, cum_f, jnp.where(lane < 48, cum_b, jax.nn.sigmoid(ab)))
    e = _dot3(both, exp_ref[...])
    gf_ref[...] = e[:, 0 * hw:1 * hw]
    gb_ref[...] = e[:, 1 * hw:2 * hw]
    bf_ref[...] = e[:, 2 * hw:3 * hw]
    bb_ref[...] = e[:, 3 * hw:4 * hw]


def _expand_matrix():
    e = np.zeros((LANES, 4 * HEADS * HD), np.float32)
    for kind in range(2):
        for d in range(2):
            for h in range(HEADS):
                src = 32 + 16 * kind + 8 * d + h
                dst = (2 * kind + d) * HEADS * HD + h * HD
                e[src, dst:dst + HD] = 1.0
    return jnp.asarray(e, BF16)


def _chunk_tri():
    r = np.arange(TOK)
    same = (r[:, None] // CHUNK) == (r[None, :] // CHUNK)
    low = same & (r[:, None] >= r[None, :])
    up = same & (r[:, None] <= r[None, :])
    return jnp.asarray(np.stack([low, up]).astype(np.float32), BF16)


def _prep_c(y, w_conv, alog, dtb, row0, n_rows, seq):
    nb8 = TOK // 8
    b0 = row0 // TOK
    last8 = (row0 + n_rows) // 8 - 1
    qkv_blk = C_QKV // CONV_CH
    out = pl.BlockSpec((TOK, HEADS * HD), lambda i: (i, 0))
    vec = pl.BlockSpec((1, LANES), lambda i: (0, 0))
    return pl.pallas_call(
        functools.partial(_prep_c_kernel, blocks_per_seq=seq // TOK),
        grid=(n_rows // TOK,),
        in_specs=[pl.BlockSpec((TOK, CONV_CH), lambda i: (b0 + i, qkv_blk)),
                  pl.BlockSpec((8, CONV_CH), lambda i: (jnp.maximum((b0 + i) * nb8 - 1, 0), qkv_blk)),
                  pl.BlockSpec((8, CONV_CH), lambda i: (jnp.minimum((b0 + i + 1) * nb8, last8), qkv_blk)),
                  pl.BlockSpec((TOK, LANES), lambda i: (b0 + i, (C_B + 640) // LANES)),
                  pl.BlockSpec((3, CONV_CH), lambda i: (0, 0)),
                  vec, vec,
                  pl.BlockSpec((256, 256), lambda i: (0, 0)),
                  pl.BlockSpec((LANES, 4 * HEADS * HD), lambda i: (0, 0)),
                  pl.BlockSpec((2, TOK, TOK), lambda i: (0, 0, 0))],
        out_specs=[out] * 7,
        out_shape=[jax.ShapeDtypeStruct((n_rows, HEADS * HD), F32)] * 7,
        compiler_params=_cparams(("parallel",)),
        name="prep_c",
    )(y, y, y, y, w_conv, alog, dtb, _ones_blockdiag(256, HD), _expand_matrix(), _chunk_tri())


def _gdn_kernel(*refs, has_s0, n_chunks, n_par):
    ins = refs[:10]
    pos = 10
    if has_s0:
        s0_refs = refs[pos:pos + 2]
        pos += 2
    o_refs = refs[pos:pos + 2]
    s_out_refs = refs[pos + 2:pos + 4]
    s_scr = refs[pos + 4]
    c = pl.program_id(1)

    @pl.when(c == 0)
    def _():
        for d in range(2):
            s_scr[d] = s0_refs[d][...] if has_s0 else jnp.zeros(s_scr.shape[1:], F32)

    n2 = 2 * CHUNK
    ri = lax.broadcasted_iota(jnp.int32, (n2, n2), 0)
    ci = lax.broadcasted_iota(jnp.int32, (n2, n2), 1)
    same = (ri // CHUNK) == (ci // CHUNK)
    left = lax.broadcasted_iota(jnp.int32, (CHUNK, n2), 1) < CHUNK
    eye = (ri == ci).astype(F32)

    def level_mask(lvl, lower):
        bit = 1 << lvl
        hi, lo = (ri, ci) if lower else (ci, ri)
        return ((ri >> (lvl + 1)) == (ci >> (lvl + 1))) & ((hi & bit) != 0) & ((lo & bit) == 0)

    levels = [[level_mask(lvl, d == 0) for lvl in range(6)] for d in range(2)]

    def block_diag(x):
        return jnp.concatenate([jnp.where(left, x, 0.0), jnp.where(left, 0.0, x)], axis=0)

    def anti_diag(x):
        xr = pltpu.roll(x, CHUNK, 1)
        return jnp.concatenate([jnp.where(left, 0.0, xr), jnp.where(left, xr, 0.0)], axis=0)

    chains = [(d, n, p) for n in range(n_par) for d in range(2) for p in range(HEADS // 2)]
    incl = [same & ((ri >= ci) if d == 0 else (ri <= ci)) for d in range(2)]
    strict = [same & ((ri > ci) if d == 0 else (ri < ci)) for d in range(2)]

    def load(which):
        return [ins[5 * d + which][n, 0, :, p * n2:(p + 1) * n2] for d, n, p in chains]

    q, k, v, gc, beta = load(0), load(1), load(2), load(3), load(4)
    gtot = [gi[CHUNK - 1:CHUNK, :] if d == 0 else gi[0:1, :] for (d, _, _), gi in zip(chains, gc)]
    eg = [jnp.exp(gi) for gi in gc]
    kb = [ki * bi for ki, bi in zip(k, beta)]
    decay = []
    for (d, _, _), gi in zip(chains, gc):
        gcr = pltpu.roll(gi, CHUNK, 1)
        gcol = jnp.concatenate([jnp.where(left, gi, gcr), jnp.where(left, gcr, gi)], axis=0)
        diff = gcol - gcol.T
        decay.append(jnp.where(incl[d], jnp.exp(jnp.where(incl[d], diff, 0.0)), 0.0))
    k_bd = [block_diag(ki).astype(BF16) for ki in k]
    kk = [_dot_nt(block_diag(kbi).astype(BF16), kbd) for kbi, kbd in zip(kb, k_bd)]
    qk = [_dot_nt(block_diag(qi).astype(BF16), kbd) for qi, kbd in zip(q, k_bd)]
    a_mat = [jnp.where(strict[d], kki * di, 0.0) for (d, _, _), kki, di in zip(chains, kk, decay)]
    qk = [(qki * di).astype(BF16) for qki, di in zip(qk, decay)]
    x = []
    for vi, bi, kbi, egi in zip(v, beta, kb, eg):
        vb = vi * bi
        kbe_r = pltpu.roll(kbi * egi, CHUNK, 1)
        x.append(jnp.concatenate([jnp.where(left, vb, kbe_r), jnp.where(left, kbe_r, vb)], axis=0))
    corr = [-jnp.where(levels[d][0], ai, 0.0) for (d, _, _), ai in zip(chains, a_mat)]
    for lvl in range(1, 6):
        d_b = [(eye + ci_).astype(BF16) for ci_ in corr]
        t = [_dot(jnp.where(levels[d][lvl], ai, 0.0).astype(BF16), di).astype(BF16)
             for (d, _, _), ai, di in zip(chains, a_mat, d_b)]
        corr = [ci_ - _dot(di, ti) for ci_, di, ti in zip(corr, d_b, t)]
    x = [xi + _dot(ci_.astype(BF16), xi.astype(BF16)) for xi, ci_ in zip(x, corr)]
    s = [s_scr[d, n, p] for d, n, p in chains]
    s_b = [si.astype(BF16) for si in s]
    ws = [_dot(jnp.where(same, 0.0, xi).astype(BF16), sbi) for xi, sbi in zip(x, s_b)]
    v_new = [(jnp.where(same, xi, 0.0) - wi).astype(BF16) for xi, wi in zip(x, ws)]
    o_s = [_dot(anti_diag(qi * egi).astype(BF16), sbi) for qi, egi, sbi in zip(q, eg, s_b)]
    o_v = [_dot(qki, vni) for qki, vni in zip(qk, v_new)]
    kd = [anti_diag(ki * jnp.exp(gt - gi)).astype(BF16) for ki, gt, gi in zip(k, gtot, gc)]
    s_upd = [_dot_tn(kdi, vni) for kdi, vni in zip(kd, v_new)]
    for i, (d, n, p) in enumerate(chains):
        o = o_s[i] + o_v[i]
        o_refs[d][n, 0, :, p * n2:(p + 1) * n2] = o[:CHUNK] + o[CHUNK:]
        s_scr[d, n, p] = s[i] * jnp.exp(gtot[i]) + s_upd[i]

    @pl.when(c == n_chunks - 1)
    def _():
        for d in range(2):
            s_out_refs[d][...] = s_scr[d]


GDN_SEQS_PER_STEP = 4


def _gdn_scan(qn, kn, v, gcf, gcb, bf, bb, s0, n_seq, seq):
    nc = seq // CHUNK
    hw = HEADS * HD
    n2 = 2 * CHUNK
    n_par = max(p for p in range(1, GDN_SEQS_PER_STEP + 1) if n_seq % p == 0)
    view = lambda a: a.reshape(n_seq, nc, CHUNK, hw)
    fwd = pl.BlockSpec((n_par, 1, CHUNK, hw), lambda b, c: (b, c, 0, 0))
    bwd = pl.BlockSpec((n_par, 1, CHUNK, hw), lambda b, c: (b, nc - 1 - c, 0, 0))
    state = pl.BlockSpec((n_par, HEADS // 2, n2, n2), lambda b, c: (b, 0, 0, 0))
    in_specs = [fwd] * 5 + [bwd] * 5
    args = [view(a) for a in (qn, kn, v, gcf, bf, qn, kn, v, gcb, bb)]
    if s0 is not None:
        in_specs += [state, state]
        args += list(s0)
    st_shape = jax.ShapeDtypeStruct((n_seq, HEADS // 2, n2, n2), F32)
    o_shape = jax.ShapeDtypeStruct((n_seq, nc, CHUNK, hw), F32)
    o_f, o_b, s_f, s_b = pl.pallas_call(
        functools.partial(_gdn_kernel, has_s0=s0 is not None, n_chunks=nc, n_par=n_par),
        grid=(n_seq // n_par, nc),
        in_specs=in_specs,
        out_specs=[fwd, bwd, state, state],
        out_shape=[o_shape, o_shape, st_shape, st_shape],
        scratch_shapes=[pltpu.VMEM((2, n_par, HEADS // 2, n2, n2), F32)],
        compiler_params=_cparams(("parallel", "arbitrary")),
        name="gdn_scan",
    )(*args)
    return o_f.reshape(n_seq * seq, hw), o_b.reshape(n_seq * seq, hw), s_f, s_b


def _state_to_pairs(s):
    b = s.shape[0]
    s = s.reshape(b, HEADS // 2, 2, HD, HD)
    z = jnp.zeros_like(s[:, :, 0])
    top = jnp.concatenate([z, s[:, :, 1]], axis=-1)
    bot = jnp.concatenate([s[:, :, 0], z], axis=-1)
    return jnp.concatenate([top, bot], axis=-2)


def _pairs_to_state(s):
    b = s.shape[0]
    even = s[:, :, HD:, :HD]
    odd = s[:, :, :HD, HD:]
    return jnp.stack([even, odd], axis=2).reshape(b, HEADS, HD, HD)


def _merge_kernel(x_ref, mod_ref, *refs, ctx_blocks):
    pairs, (n1_ref, wgz_ref, gain_ref, ones_ref, wa_ref, wb_ref, wc_ref, wo_ref, o_ref) = refs[:8], refs[8:]
    is_ctx = pl.program_id(0) < ctx_blocks
    ya, yb, o_f, o_b = [jnp.where(is_ctx, pairs[2 * t][...], pairs[2 * t + 1][...]) for t in range(4)]
    m = mod_ref[0, 0]
    chunks = [slice(r, r + ROW_CHUNK) for r in range(0, x_ref.shape[0], ROW_CHUNK)]
    pa = [_dot(ya[c], wa_ref[0]) for c in chunks]
    pb = [_dot(yb[c], wb_ref[0]) for c in chunks]
    x = [x_ref[c, :] for c in chunks]
    h = [(_rms_rows(xc, n1_ref[...]) * (1.0 + m[1:2]) + m[0:1]).astype(BF16) for xc in x]
    gz = [_dot(hc, wgz_ref[0]) for hc in h]
    yc = []
    for c, gzc in zip(chunks, gz):
        o = o_f[c] + o_b[c]
        ms = _group_sums(o * o, ones_ref[...]) * (1.0 / HD)
        yc.append((o * lax.rsqrt(ms + EPS) * gain_ref[...] * _silu(gzc[:, 3 * D_MODEL:])).astype(BF16))
    pc = [_dot(ycc, wc_ref[0]) for ycc in yc]
    merged = [(jax.nn.sigmoid(g[:, :D_MODEL]) * a + jax.nn.sigmoid(g[:, D_MODEL:2 * D_MODEL]) * b
               + jax.nn.sigmoid(g[:, 2 * D_MODEL:3 * D_MODEL]) * cc).astype(BF16)
              for g, a, b, cc in zip(gz, pa, pb, pc)]
    mix = [_dot(mc, wo_ref[0]) for mc in merged]
    for c, xc, mixc in zip(chunks, x, mix):
        o_ref[c, :] = xc + m[2:3] * mixc


def _merge(x, mod, layer, mixers, norm_gain, w_gz, gain, wa, wb, wc, wo, mod_row, ctx_rows):
    m_rows = x.shape[0]
    tm = TM_DENSE
    nb_ctx = ctx_rows // tm
    nb_lat = (m_rows - ctx_rows) // tm
    ctx_half = pl.BlockSpec((tm, 512), lambda i: (jnp.minimum(i, nb_ctx - 1), 0))
    lat_half = pl.BlockSpec((tm, 512), lambda i: (jnp.clip(i - nb_ctx, 0, nb_lat - 1), 0))
    full = pl.BlockSpec((tm, D_MODEL), lambda i: (i, 0))

    def wspec(k):
        return pl.BlockSpec((1, k, D_MODEL), lambda i: (layer, 0, 0))

    return pl.pallas_call(
        functools.partial(_merge_kernel, ctx_blocks=nb_ctx),
        grid=(m_rows // tm,),
        in_specs=[full,
                  pl.BlockSpec((1, 1, N_MOD, D_MODEL), lambda i: (layer, mod_row(i, tm), 0, 0))]
        + [ctx_half, lat_half] * 4
        + [pl.BlockSpec((1, D_MODEL), lambda i: (0, 0)),
           pl.BlockSpec((1, D_MODEL, N_GZ), lambda i: (layer, 0, 0)),
           pl.BlockSpec((1, 512), lambda i: (0, 0)),
           pl.BlockSpec((256, 256), lambda i: (0, 0)),
           wspec(512), wspec(512), wspec(512), wspec(D_MODEL)],
        out_specs=full,
        out_shape=jax.ShapeDtypeStruct((m_rows, D_MODEL), F32),
        compiler_params=_cparams(("parallel",)),
        name="merge_out",
    )(x, mod, *mixers, norm_gain, w_gz, gain, _ones_blockdiag(256, HD), wa, wb, wc, wo)


def _ffn_kernel(x_ref, mod_ref, g_ref, wg_ref, wu_ref, wd_ref, fin_ref, o_ref, *, final):
    m = mod_ref[0, 0]
    chunks = [slice(r, r + ROW_CHUNK) for r in range(0, x_ref.shape[0], ROW_CHUNK)]
    x = [x_ref[c, :] for c in chunks]
    h = [(_rms_rows(xc, g_ref[...]) * (1.0 + m[4:5]) + m[3:4]).astype(BF16) for xc in x]
    gate = [_dot(hc, wg_ref[0]) for hc in h]
    up = [_dot(hc, wu_ref[0]) for hc in h]
    act = [(_silu(gc) * uc).astype(BF16) for gc, uc in zip(gate, up)]
    down = [_dot(ac, wd_ref[0]) for ac in act]
    for c, xc, dc in zip(chunks, x, down):
        out = xc + m[5:6] * dc
        if final:
            out = _rms_rows(out, fin_ref[...])
        o_ref[c, :] = out


def _ffn(x, mod, layer, gain, wg, wu, wd, fin, mod_row, final):
    m_rows = x.shape[0]
    tm = TM_DENSE
    full = pl.BlockSpec((tm, D_MODEL), lambda i: (i, 0))
    vec = pl.BlockSpec((1, D_MODEL), lambda i: (0, 0))
    once = pl.Buffered(1)
    return pl.pallas_call(
        functools.partial(_ffn_kernel, final=final),
        grid=(m_rows // tm,),
        in_specs=[full,
                  pl.BlockSpec((1, 1, N_MOD, D_MODEL), lambda i: (layer, mod_row(i, tm), 0, 0)),
                  vec,
                  pl.BlockSpec((1, D_MODEL, D_FF), lambda i: (layer, 0, 0), pipeline_mode=once),
                  pl.BlockSpec((1, D_MODEL, D_FF), lambda i: (layer, 0, 0), pipeline_mode=once),
                  pl.BlockSpec((1, D_FF, D_MODEL), lambda i: (layer, 0, 0), pipeline_mode=once),
                  vec],
        out_specs=full,
        out_shape=jax.ShapeDtypeStruct((m_rows, D_MODEL), F32),
        compiler_params=_cparams(("parallel",)),
        name="ffn",
    )(x, mod, gain, wg, wu, wd, fin)


def _layout_w_in(w_in):
    def cols(a, b):
        return w_in[:, :, a:b]

    pad = jnp.zeros(w_in.shape[:2] + (64,), w_in.dtype)
    parts = [cols(1440, 2976), cols(768, 1440), cols(3488, 3520), pad, cols(512, 768), cols(0, 512)]
    w_tok = jnp.concatenate(parts, axis=-1).astype(BF16)
    w_gz = jnp.concatenate([cols(3520, 6592), cols(2976, 3488)], axis=-1).astype(BF16)
    return w_tok, w_gz


def _layout_w_uq(w_uq):
    n_layers = w_uq.shape[0]
    w = w_uq.reshape(n_layers, Q_LORA, HEADS, NOPE_B + ROPE_B)
    w = jnp.pad(w, ((0, 0), (0, 0), (0, 0), (0, LANES - NOPE_B - ROPE_B)))
    return w.reshape(n_layers, Q_LORA, HEADS * LANES).astype(BF16)


def _layout_w_ukv(w_ukv):
    n_layers = w_ukv.shape[0]
    w = w_ukv.reshape(n_layers, KV_LORA, HEADS, 2, HD)
    zero = jnp.zeros((n_layers, KV_LORA, HEADS, HD), w.dtype)
    wk_top = jnp.concatenate([w[:, :, :, 0], zero], axis=-1).reshape(n_layers, KV_LORA, HEADS * LANES)
    place = np.zeros((LANES, HEADS, LANES), np.float32)
    for d in range(ROPE_B):
        place[d, :, NOPE_B + d] = 1.0
    place = jnp.broadcast_to(jnp.asarray(place.reshape(LANES, HEADS * LANES)), (n_layers, LANES, HEADS * LANES))
    wk = jnp.concatenate([wk_top, place], axis=1).astype(BF16)
    wv = w[:, :, :, 1].reshape(n_layers, KV_LORA, HEADS * HD).astype(BF16)
    return wk, wv


def kernel(x_prompt, x_sample, cache_ka, cache_va, cache_ckv, cache_kpe, state_fwd, state_bwd, c, c_ctx,
           w_mod, b_mod, norm1, norm2, w_in, a_qnorm, a_knorm, b_qnorm, b_kvnorm, w_uq, w_ukv, c_conv,
           c_alog, c_dt_bias, c_onorm, w_pa, w_pb, w_pc, w_out, w_gate, w_up, w_down, final_norm):
    n_ctx, s_ctx, _ = x_prompt.shape
    n_lat, s_lat, _ = x_sample.shape
    n_layers = w_mod.shape[0]
    past = cache_ka.shape[2]
    ctx_rows = n_ctx * s_ctx
    lat_rows = n_lat * s_lat
    assert s_ctx == TOK and s_lat % TOK == 0 and ctx_rows % 512 == 0 and s_lat % 512 == 0
    assert past % TOK == 0 and n_lat + 1 <= 8 and ctx_rows % s_lat == 0

    def mod_row(i, tm):
        return jnp.where(i < ctx_rows // tm, 0, 1 + (i - ctx_rows // tm) // (s_lat // tm))

    def tab_row(i):
        nb = ctx_rows // TOK_AB
        return jnp.where(i < nb, 0, 1 + (i - nb) % (s_lat // TOK_AB))

    tm_in = TM_DENSE

    x = jnp.concatenate([x_prompt.reshape(ctx_rows, D_MODEL), x_sample.reshape(lat_rows, D_MODEL)], axis=0)
    cvec = jnp.zeros((8, D_MODEL), F32).at[0].set(c_ctx).at[1:1 + n_lat].set(c)
    mod = _modulation(cvec, w_mod, b_mod).reshape(n_layers, 8, N_MOD, D_MODEL)
    cos, sin = _rope_tables(s_lat)

    w_in_p, w_gz = _layout_w_in(w_in)
    w_uq_p = _layout_w_uq(w_uq)
    w_k_p, w_v_p = _layout_w_ukv(w_ukv)
    bf = lambda w: w.astype(BF16)
    w_pa_b, w_pb_b, w_pc_b, w_out_b = bf(w_pa), bf(w_pb), bf(w_pc), bf(w_out)
    w_gate_b, w_up_b, w_down_b = bf(w_gate), bf(w_up), bf(w_down)

    cache_rows = n_lat * n_layers * past
    ck4, cv4 = _cache_a(cache_ka.reshape(cache_rows, LANES), cache_va.reshape(cache_rows, LANES))
    kpe_pad = jnp.pad(cache_kpe.reshape(cache_rows, ROPE_B), ((0, 0), (0, LANES - ROPE_B)))
    ckb, cvb = _cache_b(cache_ckv.reshape(cache_rows, KV_LORA), kpe_pad, w_k_p, w_v_p,
                        lambda i: (i // (past // TOK)) % n_layers)

    lane_vec = lambda v16: jnp.zeros((1, LANES), F32).at[0, 32:48].set(v16.reshape(16))
    new = []
    for l in range(n_layers):
        y = _in_proj(x, mod, l, norm1[l][None], w_in_p, mod_row, tm_in)

        qa, k4, v4, kn_a = _prep_a(y, cos, sin, jnp.tile(a_qnorm[l], HEADS)[None],
                                   jnp.tile(a_knorm[l], KV_HEADS_A)[None], tab_row)
        qb, kb, vb, ckvn = _prep_b(y, cos, sin, b_qnorm[l][None], b_kvnorm[l][None],
                                   w_uq_p, w_k_p, w_v_p, l, tab_row)
        conv_args = (c_conv[l, :, 0, :], lane_vec(c_alog[l]), lane_vec(c_dt_bias[l]))
        gdn_ctx = _prep_c(y, *conv_args, 0, ctx_rows, s_ctx)
        gdn_lat = _prep_c(y, *conv_args, ctx_rows, lat_rows, s_lat)

        cache_blk = (past, lambda b, l=l: b * n_layers + l)
        ya_c = _attention(qa, k4, v4, None, GROUPS_A, 0, n_ctx, s_ctx, s_ctx, None)
        ya_l = _attention(qa, k4, v4, (ck4, cv4), GROUPS_A, ctx_rows, n_lat, s_lat, 256, cache_blk)
        yb_c = _attention(qb, kb, vb, None, GROUPS_B, 0, n_ctx, s_ctx, s_ctx, None)
        yb_l = _attention(qb, kb, vb, (ckb, cvb), GROUPS_B, ctx_rows, n_lat, s_lat, 256, cache_blk)

        of_c, ob_c, sf_c, sb_c = _gdn_scan(*gdn_ctx, None, n_ctx, s_ctx)
        s0 = (_state_to_pairs(state_fwd[:, l]), _state_to_pairs(state_bwd[:, l]))
        of_l, ob_l, _, _ = _gdn_scan(*gdn_lat, s0, n_lat, s_lat)

        x = _merge(x, mod, l, (ya_c, ya_l, yb_c, yb_l, of_c, of_l, ob_c, ob_l), norm1[l][None], w_gz,
                   jnp.tile(c_onorm[l], HEADS)[None], w_pa_b, w_pb_b, w_pc_b, w_out_b, mod_row, ctx_rows)
        x = _ffn(x, mod, l, norm2[l][None], w_gate_b, w_up_b, w_down_b, final_norm[None], mod_row,
                 final=(l == n_layers - 1))

        new.append((kn_a[:ctx_rows].reshape(n_ctx, s_ctx, KV_HEADS_A, HD),
                    y[:ctx_rows, C_VA:C_VA + LANES].reshape(n_ctx, s_ctx, KV_HEADS_A, HD),
                    ckvn[:ctx_rows].reshape(n_ctx, s_ctx, KV_LORA),
                    y[:ctx_rows, C_B + 640:C_B + 640 + ROPE_B].reshape(n_ctx, s_ctx, ROPE_B),
                    _pairs_to_state(sf_c), _pairs_to_state(sb_c)))

    y_prompt = x[:ctx_rows].reshape(n_ctx, s_ctx, D_MODEL)
    y_sample = x[ctx_rows:].reshape(n_lat, s_lat, D_MODEL)
    stacked = [jnp.stack([n[i] for n in new], axis=1) for i in range(6)]
    return (y_prompt, y_sample) + tuple(stacked)
```

```python
import functools

import numpy as np
import jax
import jax.numpy as jnp
from jax import lax
from jax.experimental import pallas as pl
from jax.experimental.pallas import tpu as pltpu

F32 = jnp.float32
BF16 = jnp.bfloat16

D_MODEL = 1024
GRID_W = 64
GRID_W_LOG2 = 6
assert 1 << GRID_W_LOG2 == GRID_W
ROPE_THETA = 10000.0
EPS = 1e-6
N_MOD = 6
HEADS = 8
KV_HEADS_A = 2
HD = 64
Q_LORA = 384
KV_LORA = 256
NOPE_B = 64
ROPE_B = 32
CHUNK = 64
CONV_CH = 3 * HEADS * HD
D_FF = 2816

C_QKV = 0
C_B = 1536
C_KA = 2304
C_VA = 2432
C_QA = 2560
N_IN = 3072
N_GZ = 3 * D_MODEL + HEADS * HD

LOG2E = 1.4426950408889634
LANES = 128
TOK = 256
TOK_AB = 512
TM_DENSE = 512
ROW_CHUNK = 128
VMEM_LIMIT = 56 * 1024 * 1024


def _cparams(sem):
    return pltpu.CompilerParams(dimension_semantics=sem, vmem_limit_bytes=VMEM_LIMIT)


def _dot(a, b):
    return jnp.dot(a, b, preferred_element_type=F32)


def _dot_nt(a, b):
    return lax.dot_general(a, b, (((1,), (1,)), ((), ())), preferred_element_type=F32)


def _dot_tn(a, b):
    return lax.dot_general(a, b, (((0,), (0,)), ((), ())), preferred_element_type=F32)


def _split3(x):
    hi = x.astype(BF16)
    r = x - hi.astype(F32)
    mid = r.astype(BF16)
    lo = (r - mid.astype(F32)).astype(BF16)
    return hi, mid, lo


def _dot3(x, w):
    hi, mid, lo = _split3(x)
    return _dot(hi, w) + _dot(mid, w) + _dot(lo, w)


def _group_sums(x, ones):
    w = min(x.shape[1], ones.shape[0])
    blk = ones[:w, :w]
    slabs = []
    for s in range(x.shape[1] // w):
        xs = x[:, s * w:(s + 1) * w]
        hi = xs.astype(BF16)
        lo = (xs - hi.astype(F32)).astype(BF16)
        slabs.append(_dot(hi, blk) + _dot(lo, blk))
    return slabs[0] if len(slabs) == 1 else jnp.concatenate(slabs, axis=1)


def _silu(x):
    return x * jax.nn.sigmoid(x)


def _rms_rows(x, g):
    return x * lax.rsqrt(jnp.mean(x * x, axis=-1, keepdims=True) + EPS) * g


def _swap_pairs(x, half):
    lane = lax.broadcasted_iota(jnp.int32, x.shape, 1)
    n = x.shape[1]
    return jnp.where((lane & half) == 0, pltpu.roll(x, n - half, 1), pltpu.roll(x, half, 1))


def _rope_cols(x, cos, sin, half):
    outs = []
    for s in range(x.shape[1] // LANES):
        xs = x[:, s * LANES:(s + 1) * LANES]
        outs.append(xs * cos + _swap_pairs(xs, half) * sin)
    return outs[0] if len(outs) == 1 else jnp.concatenate(outs, axis=1)


def _mod_kernel(c_ref, w_ref, b_ref, o_ref):
    s = _silu(c_ref[...]).astype(BF16)
    o_ref[0] = _dot(s, w_ref[0].astype(BF16)) + b_ref[0]


def _modulation(cvec, w_mod, b_mod):
    n_layers = w_mod.shape[0]
    n_out = w_mod.shape[2]
    tn = 1536
    return pl.pallas_call(
        _mod_kernel,
        grid=(n_layers, n_out // tn),
        in_specs=[pl.BlockSpec((8, D_MODEL), lambda l, j: (0, 0)),
                  pl.BlockSpec((1, D_MODEL, tn), lambda l, j: (l, 0, j)),
                  pl.BlockSpec((1, 1, tn), lambda l, j: (l, 0, j))],
        out_specs=pl.BlockSpec((1, 8, tn), lambda l, j: (l, 0, j)),
        out_shape=jax.ShapeDtypeStruct((n_layers, 8, n_out), F32),
        compiler_params=_cparams(("parallel", "parallel")),
        name="modulation",
    )(cvec, w_mod, b_mod.reshape(n_layers, 1, n_out))


def _in_proj_kernel(x_ref, mod_ref, g_ref, w_ref, o_ref):
    m = mod_ref[0, 0]
    for r in range(0, x_ref.shape[0], ROW_CHUNK):
        y = _rms_rows(x_ref[r:r + ROW_CHUNK, :], g_ref[...])
        h = (y * (1.0 + m[1:2]) + m[0:1]).astype(BF16)
        o_ref[r:r + ROW_CHUNK, :] = _dot(h, w_ref[0])


def _token_kernel(x_ref, mod_ref, g_ref, w_ref,
                  cos_a, sin_a, cos_q, sin_q, cos_k, sin_k, gq_a, gk_a, ones_ref, place_ref,
                  gq_b, gkv_b, wq_ref, wk_ref, wv_ref,
                  yc_ref, va_ref, qa_ref, k4_ref, vta_ref, kna_ref, qb_ref, kb_ref, vtb_ref, ckv_ref, y_scr):
    _in_proj_kernel(x_ref, mod_ref, g_ref, w_ref, y_scr)
    yc_ref[:, :CONV_CH] = y_scr[:, C_QKV:C_QKV + CONV_CH]
    yc_ref[:, CONV_CH:] = y_scr[:, C_B + 640:C_B + 640 + LANES]
    va_ref[...] = y_scr[:, C_VA:C_VA + LANES]
    _prep_a_kernel(y_scr.at[:, C_QA:C_QA + HEADS * HD], y_scr.at[:, C_KA:C_KA + 2 * LANES],
                   cos_a, sin_a, gq_a, gk_a, ones_ref, place_ref, qa_ref, k4_ref, vta_ref, kna_ref)
    _prep_b_kernel(y_scr.at[:, C_B:C_B + 768], cos_q, sin_q, cos_k, sin_k, gq_b, gkv_b,
                   wq_ref, wk_ref, wv_ref, qb_ref, kb_ref, vtb_ref, ckv_ref)


def _token_stage(x, mod, layer, norm_gain, w, cos, sin, gq_a, gk_a, gq_b, gkv_b, wq, wk, wv, mod_row, tab_row):
    m_rows = x.shape[0]
    tm = TOK_AB
    rows = lambda w_: pl.BlockSpec((tm, w_), lambda i: (i, 0))
    cols = lambda h_: pl.BlockSpec((h_, tm), lambda i: (0, i))
    vec = lambda w_: pl.BlockSpec((1, w_), lambda i: (0, 0))
    tab = lambda a: pl.BlockSpec((1, tm, LANES), lambda i: (a, tab_row(i), 0))
    lw = lambda k, n: pl.BlockSpec((1, k, n), lambda i: (layer, 0, 0))
    f32 = lambda *shape: jax.ShapeDtypeStruct(shape, F32)
    bf16 = lambda *shape: jax.ShapeDtypeStruct(shape, BF16)
    return pl.pallas_call(
        _token_kernel,
        grid=(m_rows // tm,),
        in_specs=[rows(D_MODEL),
                  pl.BlockSpec((1, 1, N_MOD, D_MODEL), lambda i: (layer, mod_row(i, tm), 0, 0)),
                  vec(D_MODEL), lw(D_MODEL, N_IN),
                  tab(0), tab(0), tab(1), tab(1), tab(2), tab(2),
                  vec(HEADS * HD), vec(LANES),
                  pl.BlockSpec((256, 256), lambda i: (0, 0)), pl.BlockSpec((LANES, 4 * LANES), lambda i: (0, 0)),
                  vec(Q_LORA), vec(KV_LORA),
                  lw(Q_LORA, HEADS * LANES), lw(KV_LORA + LANES, HEADS * LANES), lw(KV_LORA, HEADS * HD)],
        out_specs=[rows(CONV_CH + LANES), rows(LANES),
                   rows(4 * LANES), rows(4 * LANES), cols(LANES), rows(LANES),
                   rows(HEADS * LANES), rows(HEADS * LANES), cols(HEADS * HD), rows(KV_LORA)],
        out_shape=[f32(m_rows, CONV_CH + LANES), f32(m_rows, LANES),
                   bf16(m_rows, 4 * LANES), bf16(m_rows, 4 * LANES), bf16(LANES, m_rows), f32(m_rows, LANES),
                   bf16(m_rows, HEADS * LANES), bf16(m_rows, HEADS * LANES), bf16(HEADS * HD, m_rows),
                   f32(m_rows, KV_LORA)],
        scratch_shapes=[pltpu.VMEM((tm, N_IN), F32)],
        compiler_params=_cparams(("parallel",)),
        name="token_stage",
    )(x, mod, norm_gain, w, cos, sin, cos, sin, cos, sin, gq_a, gk_a, _ones_blockdiag(256, HD), _place4(),
      gq_b, gkv_b, wq, wk, wv)


def _rope_table_kernel(freq_ref, col_ref, sign_ref, cos_ref, sin_ref):
    shape = cos_ref.shape
    pos = pl.program_id(1) * TOK_AB + lax.broadcasted_iota(jnp.int32, shape, 1) - TOK_AB
    grid_row = (pos >> GRID_W_LOG2).astype(F32)
    grid_col = (pos & (GRID_W - 1)).astype(F32)
    a = jnp.where(col_ref[...] > 0.5, grid_col, grid_row) * freq_ref[...]
    a = jnp.where(pos >= 0, a, 0.0)
    cos_ref[...] = jnp.cos(a)
    sin_ref[...] = jnp.sin(a) * sign_ref[...]


def _rope_tables(n_lat):
    lane = np.arange(LANES)

    def lane_pattern(nf, d, active):
        inv_freq = np.float32(ROPE_THETA) ** (-np.arange(nf, dtype=np.float32) / np.float32(nf))
        dd = np.where(active, d, 0)
        freq = np.where(active, inv_freq[dd % nf], 0.0).astype(np.float32)
        use_col = (dd >= 2 * nf).astype(np.float32)
        sign = np.where((dd % (2 * nf)) < nf, -1.0, 1.0).astype(np.float32)
        return freq, use_col, sign

    d_q = lane - NOPE_B
    patterns = [lane_pattern(HD // 4, lane % HD, np.ones(LANES, bool)),
                lane_pattern(ROPE_B // 4, d_q, (d_q >= 0) & (d_q < ROPE_B)),
                lane_pattern(ROPE_B // 4, lane, lane < ROPE_B)]
    freq, use_col, sign = [jnp.asarray(np.stack([p[t] for p in patterns]))[:, None, :] for t in range(3)]
    rows = TOK_AB + n_lat
    spec = pl.BlockSpec((1, TOK_AB, LANES), lambda a, i: (a, i, 0))
    vec = pl.BlockSpec((1, 1, LANES), lambda a, i: (a, 0, 0))
    return pl.pallas_call(
        _rope_table_kernel,
        grid=(3, rows // TOK_AB),
        in_specs=[vec, vec, vec],
        out_specs=[spec, spec],
        out_shape=[jax.ShapeDtypeStruct((3, rows, LANES), F32)] * 2,
        compiler_params=_cparams(("parallel", "parallel")),
        name="rope_tables",
    )(freq, use_col, sign)


def _prep_a_kernel(qa_ref, kv_ref, cos_ref, sin_ref, gq_ref, gk_ref, ones_ref, place_ref,
                   q_ref, k4_ref, vt_ref, kn_ref):
    cos = cos_ref[0]
    sin = sin_ref[0]
    ones = ones_ref[...]
    q = qa_ref[...]
    qn = q * lax.rsqrt(_group_sums(q * q, ones) * (1.0 / HD) + EPS) * gq_ref[...]
    q_ref[...] = (_rope_cols(qn, cos, sin, HD // 4) * (HD ** -0.5 * LOG2E)).astype(BF16)
    k = kv_ref[:, :LANES]
    kn = k * lax.rsqrt(_group_sums(k * k, ones) * (1.0 / HD) + EPS) * gk_ref[...]
    kn_ref[...] = kn
    k4_ref[...] = _dot(_rope_cols(kn, cos, sin, HD // 4).astype(BF16), place_ref[...]).astype(BF16)
    vt_ref[...] = kv_ref[:, LANES:].T.astype(BF16)


def _place4():
    p = np.zeros((LANES, 4 * LANES), np.float32)
    for kv in range(KV_HEADS_A):
        for side in range(2):
            for d in range(HD):
                p[kv * HD + d, (2 * kv + side) * LANES + side * HD + d] = 1.0
    return jnp.asarray(p, BF16)


def _ones_blockdiag(n, group):
    idx = np.arange(n) // group
    return jnp.asarray((idx[:, None] == idx[None, :]).astype(np.float32), BF16)


def _cache_a_kernel(k_ref, v_ref, place_ref, k4_ref, vt_ref):
    k4_ref[...] = _dot(k_ref[...].astype(BF16), place_ref[...]).astype(BF16)
    vt_ref[...] = v_ref[...].T.astype(BF16)


def _cache_a(ck, cv):
    rows = ck.shape[0]
    narrow = pl.BlockSpec((TOK, LANES), lambda i: (i, 0))
    return pl.pallas_call(
        _cache_a_kernel,
        grid=(rows // TOK,),
        in_specs=[narrow, narrow, pl.BlockSpec((LANES, 512), lambda i: (0, 0))],
        out_specs=[pl.BlockSpec((TOK, 512), lambda i: (i, 0)), pl.BlockSpec((LANES, TOK), lambda i: (0, i))],
        out_shape=[jax.ShapeDtypeStruct((rows, 512), BF16), jax.ShapeDtypeStruct((LANES, rows), BF16)],
        compiler_params=_cparams(("parallel",)),
        name="cache_a",
    )(ck, cv, _place4())


def _prep_b_kernel(g_ref, cq_ref, sq_ref, ck_ref, sk_ref, gq_ref, gkv_ref, wq_ref, wk_ref, wv_ref,
                   q_ref, k_ref, vt_ref, ckv_ref):
    g = g_ref[...]
    cqn = _rms_rows(g[:, :Q_LORA], gq_ref[...])
    q = _dot(cqn.astype(BF16), wq_ref[0])
    q = _rope_cols(q, cq_ref[0], sq_ref[0], ROPE_B // 4) * ((NOPE_B + ROPE_B) ** -0.5 * LOG2E)
    q_ref[...] = q.astype(BF16)
    ckvn = _rms_rows(g[:, Q_LORA:Q_LORA + KV_LORA], gkv_ref[...])
    ckv_ref[...] = ckvn
    kpe = _rope_cols(g[:, Q_LORA + KV_LORA:], ck_ref[0], sk_ref[0], ROPE_B // 4)
    ckv_b = ckvn.astype(BF16)
    kin = jnp.concatenate([ckv_b, kpe.astype(BF16)], axis=1)
    k_ref[...] = _dot(kin, wk_ref[0]).astype(BF16)
    vt_ref[...] = _dot(ckv_b, wv_ref[0]).T.astype(BF16)


def _cache_b_kernel(c_ref, p_ref, wk_ref, wv_ref, k_ref, vt_ref):
    ckv_b = c_ref[...].astype(BF16)
    kin = jnp.concatenate([ckv_b, p_ref[...].astype(BF16)], axis=1)
    k_ref[...] = _dot(kin, wk_ref[0]).astype(BF16)
    vt_ref[...] = _dot(ckv_b, wv_ref[0]).T.astype(BF16)


def _cache_b(ckv, kpe, wk, wv, layer_of_block):
    rows = ckv.shape[0]
    return pl.pallas_call(
        _cache_b_kernel,
        grid=(rows // TOK,),
        in_specs=[pl.BlockSpec((TOK, KV_LORA), lambda i: (i, 0)),
                  pl.BlockSpec((TOK, LANES), lambda i: (i, 0)),
                  pl.BlockSpec((1, KV_LORA + LANES, HEADS * LANES), lambda i: (layer_of_block(i), 0, 0)),
                  pl.BlockSpec((1, KV_LORA, HEADS * HD), lambda i: (layer_of_block(i), 0, 0))],
        out_specs=[pl.BlockSpec((TOK, HEADS * LANES), lambda i: (i, 0)),
                   pl.BlockSpec((HEADS * HD, TOK), lambda i: (0, i))],
        out_shape=[jax.ShapeDtypeStruct((rows, HEADS * LANES), BF16),
                   jax.ShapeDtypeStruct((HEADS * HD, rows), BF16)],
        compiler_params=_cparams(("parallel",)),
        name="cache_b",
    )(ckv, kpe, wk, wv)


ATT_KEY_TILE = 256
ATT_LOOKAHEAD = 5


def _attn_kernel(*refs, groups, has_cache):
    if has_cache:
        q_ref, kc_ref, vc_ref, kn_ref, vn_ref, o_ref = refs
        segments = [(kc_ref, vc_ref), (kn_ref, vn_ref)]
    else:
        q_ref, kn_ref, vn_ref, o_ref = refs
        segments = [(kn_ref, vn_ref)]
    tiles = [(k_ref, v_ref, t * ATT_KEY_TILE) for k_ref, v_ref in segments
             for t in range(k_ref.shape[0] // ATT_KEY_TILE)]
    q = [q_ref[:, qg * LANES:(qg + 1) * LANES] for qg, _, _ in groups]

    def scores(h, i):
        k_ref, _, t0 = tiles[i]
        kg = groups[h][1]
        return _dot_nt(k_ref[t0:t0 + ATT_KEY_TILE, kg * LANES:(kg + 1) * LANES], q[h])

    heads = range(len(groups))
    jobs = [(i, h) for i in range(len(tiles)) for h in heads]
    pending = {j: scores(h, i) for j, (i, h) in enumerate(jobs[:ATT_LOOKAHEAD])}
    ones_rows = (lax.broadcasted_iota(jnp.int32, (16, ATT_KEY_TILE), 0) == 0).astype(BF16)
    m, acc = {}, {}
    for j, (i, h) in enumerate(jobs):
        if j + ATT_LOOKAHEAD < len(jobs):
            i_n, h_n = jobs[j + ATT_LOOKAHEAD]
            pending[j + ATT_LOOKAHEAD] = scores(h_n, i_n)
        s = pending.pop(j)
        _, v_ref, t0 = tiles[i]
        vrow = groups[h][2]
        v_t = jnp.concatenate([v_ref[vrow:vrow + HD, t0:t0 + ATT_KEY_TILE], ones_rows], axis=0)
        m_tile = jnp.max(s, axis=0, keepdims=True)
        if i == 0:
            m[h] = m_tile
            acc[h] = _dot(v_t, jnp.exp2(s - m_tile).astype(BF16))
        else:
            m_new = jnp.maximum(m[h], m_tile)
            alpha = jnp.exp2(m[h] - m_new)
            acc[h] = acc[h] * alpha + _dot(v_t, jnp.exp2(s - m_new).astype(BF16))
            m[h] = m_new
    outs = [acc[h][:HD] / acc[h][HD:HD + 1] for h in heads]
    o_ref[...] = jnp.concatenate(outs, axis=0).T.astype(BF16)


def _attention(q, kn, vn, cache, groups, row0, n_seq, seq, tq, cache_block):
    qw, kw, vr = q.shape[1], kn.shape[1], vn.shape[0]
    nq = seq // tq
    qb0, sb0 = row0 // tq, row0 // seq
    in_specs = [pl.BlockSpec((tq, qw), lambda b, i: (qb0 + b * nq + i, 0))]
    args = [q]
    if cache is not None:
        past = cache_block[0]
        in_specs += [pl.BlockSpec((past, kw), lambda b, i: (cache_block[1](b), 0)),
                     pl.BlockSpec((vr, past), lambda b, i: (0, cache_block[1](b)))]
        args += list(cache)
    in_specs += [pl.BlockSpec((seq, kw), lambda b, i: (sb0 + b, 0)),
                 pl.BlockSpec((vr, seq), lambda b, i: (0, sb0 + b))]
    args += [kn, vn]
    return pl.pallas_call(
        functools.partial(_attn_kernel, groups=groups, has_cache=cache is not None),
        grid=(n_seq, nq),
        in_specs=in_specs,
        out_specs=pl.BlockSpec((tq, 512), lambda b, i: (b * nq + i, 0)),
        out_shape=jax.ShapeDtypeStruct((n_seq * seq, 512), BF16),
        compiler_params=_cparams(("parallel", "parallel")),
        name="attention",
    )(*args)


GROUPS_A = tuple((h // 2, 2 * (h // 4) + (h % 2), (h // 4) * HD) for h in range(HEADS))
GROUPS_B = tuple((h, h, h * HD) for h in range(HEADS))


def _prep_c_kernel(x_ref, prev_ref, next_ref, ab_ref, w_ref, alog_ref, dtb_ref, ones_ref, exp_ref, tri_ref,
                   q_ref, k_ref, v_ref, gf_ref, gb_ref, bf_ref, bb_ref, *, blocks_per_seq):
    j = pl.program_id(0) % blocks_per_seq
    is_first, is_last = j == 0, j == blocks_per_seq - 1
    x = x_ref[...]
    rows = x.shape[0]
    rid = lax.broadcasted_iota(jnp.int32, x.shape, 0)
    halo_p = jnp.where(is_first, 0.0, prev_ref[7:8, :])
    halo_n = jnp.where(is_last, 0.0, next_ref[0:1, :])
    x_prev = jnp.where(rid == 0, halo_p, pltpu.roll(x, 1, 0))
    x_next = jnp.where(rid == rows - 1, halo_n, pltpu.roll(x, rows - 1, 0))
    w = w_ref[...]
    s = _silu(x_prev * w[0:1] + x * w[1:2] + x_next * w[2:3])
    ones = ones_ref[...]
    hw = HEADS * HD
    q, k = s[:, :hw], s[:, hw:2 * hw]
    q_ref[...] = q * lax.rsqrt(_group_sums(q * q, ones) + EPS) * (HD ** -0.5)
    k_ref[...] = k * lax.rsqrt(_group_sums(k * k, ones) + EPS)
    v_ref[...] = s[:, 2 * hw:]
    ab = ab_ref[...]
    t = ab + dtb_ref[...]
    softplus = jnp.maximum(t, 0.0) + jnp.log1p(jnp.exp(-jnp.abs(t)))
    g = -jnp.exp(alog_ref[...]) * softplus
    lane = lax.broadcasted_iota(jnp.int32, ab.shape, 1)
    parts = _split3(g)
    cum_f = sum(_dot(tri_ref[0], part) for part in parts)
    cum_b = sum(_dot(tri_ref[1], part) for part in parts)
    both = jnp.where(lane < 40, cum_f, jnp.where(lane < 48, cum_b, jax.nn.sigmoid(ab)))
    e = _dot3(both, exp_ref[...])
    gf_ref[...] = e[:, 0 * hw:1 * hw]
    gb_ref[...] = e[:, 1 * hw:2 * hw]
    bf_ref[...] = e[:, 2 * hw:3 * hw]
    bb_ref[...] = e[:, 3 * hw:4 * hw]


def _expand_matrix():
    e = np.zeros((LANES, 4 * HEADS * HD), np.float32)
    for kind in range(2):
        for d in range(2):
            for h in range(HEADS):
                src = 32 + 16 * kind + 8 * d + h
                dst = (2 * kind + d) * HEADS * HD + h * HD
                e[src, dst:dst + HD] = 1.0
    return jnp.asarray(e, BF16)


def _chunk_tri():
    r = np.arange(TOK)
    same = (r[:, None] // CHUNK) == (r[None, :] // CHUNK)
    low = same & (r[:, None] >= r[None, :])
    up = same & (r[:, None] <= r[None, :])
    return jnp.asarray(np.stack([low, up]).astype(np.float32), BF16)


def _prep_c(y, w_conv, alog, dtb, row0, n_rows, seq):
    nb8 = TOK // 8
    b0 = row0 // TOK
    last8 = (row0 + n_rows) // 8 - 1
    qkv_blk = 0
    out = pl.BlockSpec((TOK, HEADS * HD), lambda i: (i, 0))
    vec = pl.BlockSpec((1, LANES), lambda i: (0, 0))
    return pl.pallas_call(
        functools.partial(_prep_c_kernel, blocks_per_seq=seq // TOK),
        grid=(n_rows // TOK,),
        in_specs=[pl.BlockSpec((TOK, CONV_CH), lambda i: (b0 + i, qkv_blk)),
                  pl.BlockSpec((8, CONV_CH), lambda i: (jnp.maximum((b0 + i) * nb8 - 1, 0), qkv_blk)),
                  pl.BlockSpec((8, CONV_CH), lambda i: (jnp.minimum((b0 + i + 1) * nb8, last8), qkv_blk)),
                  pl.BlockSpec((TOK, LANES), lambda i: (b0 + i, CONV_CH // LANES)),
                  pl.BlockSpec((3, CONV_CH), lambda i: (0, 0)),
                  vec, vec,
                  pl.BlockSpec((256, 256), lambda i: (0, 0)),
                  pl.BlockSpec((LANES, 4 * HEADS * HD), lambda i: (0, 0)),
                  pl.BlockSpec((2, TOK, TOK), lambda i: (0, 0, 0))],
        out_specs=[out] * 7,
        out_shape=[jax.ShapeDtypeStruct((n_rows, HEADS * HD), F32)] * 7,
        compiler_params=_cparams(("parallel",)),
        name="prep_c",
    )(y, y, y, y, w_conv, alog, dtb, _ones_blockdiag(256, HD), _expand_matrix(), _chunk_tri())


def _gdn_kernel(*refs, has_s0, n_chunks, n_par):
    ins = refs[:10]
    pos = 10
    if has_s0:
        s0_refs = refs[pos:pos + 2]
        pos += 2
    o_refs = refs[pos:pos + 2]
    s_out_refs = refs[pos + 2:pos + 4]
    s_scr = refs[pos + 4]
    c = pl.program_id(1)

    @pl.when(c == 0)
    def _():
        for d in range(2):
            s_scr[d] = s0_refs[d][...] if has_s0 else jnp.zeros(s_scr.shape[1:], F32)

    n2 = 2 * CHUNK
    ri = lax.broadcasted_iota(jnp.int32, (n2, n2), 0)
    ci = lax.broadcasted_iota(jnp.int32, (n2, n2), 1)
    same = (ri // CHUNK) == (ci // CHUNK)
    left = lax.broadcasted_iota(jnp.int32, (CHUNK, n2), 1) < CHUNK
    eye = (ri == ci).astype(F32)

    def level_mask(lvl, lower):
        bit = 1 << lvl
        hi, lo = (ri, ci) if lower else (ci, ri)
        return ((ri >> (lvl + 1)) == (ci >> (lvl + 1))) & ((hi & bit) != 0) & ((lo & bit) == 0)

    levels = [[level_mask(lvl, d == 0) for lvl in range(6)] for d in range(2)]

    def block_diag(x):
        return jnp.concatenate([jnp.where(left, x, 0.0), jnp.where(left, 0.0, x)], axis=0)

    def anti_diag(x):
        xr = pltpu.roll(x, CHUNK, 1)
        return jnp.concatenate([jnp.where(left, 0.0, xr), jnp.where(left, xr, 0.0)], axis=0)

    chains = [(d, n, p) for n in range(n_par) for d in range(2) for p in range(HEADS // 2)]
    incl = [same & ((ri >= ci) if d == 0 else (ri <= ci)) for d in range(2)]
    strict = [same & ((ri > ci) if d == 0 else (ri < ci)) for d in range(2)]

    def load(which):
        return [ins[5 * d + which][n, 0, :, p * n2:(p + 1) * n2] for d, n, p in chains]

    q, k, v, gc, beta = load(0), load(1), load(2), load(3), load(4)
    gtot = [gi[CHUNK - 1:CHUNK, :] if d == 0 else gi[0:1, :] for (d, _, _), gi in zip(chains, gc)]
    eg = [jnp.exp(gi) for gi in gc]
    kb = [ki * bi for ki, bi in zip(k, beta)]
    decay = []
    for (d, _, _), gi in zip(chains, gc):
        gcr = pltpu.roll(gi, CHUNK, 1)
        gcol = jnp.concatenate([jnp.where(left, gi, gcr), jnp.where(left, gcr, gi)], axis=0)
        diff = gcol - gcol.T
        decay.append(jnp.where(incl[d], jnp.exp(jnp.where(incl[d], diff, 0.0)), 0.0))
    k_bd = [block_diag(ki).astype(BF16) for ki in k]
    kk = [_dot_nt(block_diag(kbi).astype(BF16), kbd) for kbi, kbd in zip(kb, k_bd)]
    qk = [_dot_nt(block_diag(qi).astype(BF16), kbd) for qi, kbd in zip(q, k_bd)]
    a_mat = [jnp.where(strict[d], kki * di, 0.0) for (d, _, _), kki, di in zip(chains, kk, decay)]
    qk = [(qki * di).astype(BF16) for qki, di in zip(qk, decay)]
    x = []
    for vi, bi, kbi, egi in zip(v, beta, kb, eg):
        vb = vi * bi
        kbe_r = pltpu.roll(kbi * egi, CHUNK, 1)
        x.append(jnp.concatenate([jnp.where(left, vb, kbe_r), jnp.where(left, kbe_r, vb)], axis=0))
    corr = [-jnp.where(levels[d][0], ai, 0.0) for (d, _, _), ai in zip(chains, a_mat)]
    for lvl in range(1, 6):
        d_b = [(eye + ci_).astype(BF16) for ci_ in corr]
        t = [_dot(jnp.where(levels[d][lvl], ai, 0.0).astype(BF16), di).astype(BF16)
             for (d, _, _), ai, di in zip(chains, a_mat, d_b)]
        corr = [ci_ - _dot(di, ti) for ci_, di, ti in zip(corr, d_b, t)]
    x = [xi + _dot(ci_.astype(BF16), xi.astype(BF16)) for xi, ci_ in zip(x, corr)]
    s = [s_scr[d, n, p] for d, n, p in chains]
    s_b = [si.astype(BF16) for si in s]
    ws = [_dot(jnp.where(same, 0.0, xi).astype(BF16), sbi) for xi, sbi in zip(x, s_b)]
    v_new = [(jnp.where(same, xi, 0.0) - wi).astype(BF16) for xi, wi in zip(x, ws)]
    o_s = [_dot(anti_diag(qi * egi).astype(BF16), sbi) for qi, egi, sbi in zip(q, eg, s_b)]
    o_v = [_dot(qki, vni) for qki, vni in zip(qk, v_new)]
    kd = [anti_diag(ki * jnp.exp(gt - gi)).astype(BF16) for ki, gt, gi in zip(k, gtot, gc)]
    s_upd = [_dot_tn(kdi, vni) for kdi, vni in zip(kd, v_new)]
    for i, (d, n, p) in enumerate(chains):
        o = o_s[i] + o_v[i]
        o_refs[d][n, 0, :, p * n2:(p + 1) * n2] = o[:CHUNK] + o[CHUNK:]
        s_scr[d, n, p] = s[i] * jnp.exp(gtot[i]) + s_upd[i]

    @pl.when(c == n_chunks - 1)
    def _():
        for d in range(2):
            s_out_refs[d][...] = s_scr[d]


GDN_SEQS_PER_STEP = 2


def _gdn_scan(qn, kn, v, gcf, gcb, bf, bb, s0, n_seq, seq):
    nc = seq // CHUNK
    hw = HEADS * HD
    n2 = 2 * CHUNK
    n_par = max(p for p in range(1, GDN_SEQS_PER_STEP + 1) if n_seq % p == 0)
    view = lambda a: a.reshape(n_seq, nc, CHUNK, hw)
    fwd = pl.BlockSpec((n_par, 1, CHUNK, hw), lambda b, c: (b, c, 0, 0))
    bwd = pl.BlockSpec((n_par, 1, CHUNK, hw), lambda b, c: (b, nc - 1 - c, 0, 0))
    state = pl.BlockSpec((n_par, HEADS // 2, n2, n2), lambda b, c: (b, 0, 0, 0))
    in_specs = [fwd] * 5 + [bwd] * 5
    args = [view(a) for a in (qn, kn, v, gcf, bf, qn, kn, v, gcb, bb)]
    if s0 is not None:
        in_specs += [state, state]
        args += list(s0)
    st_shape = jax.ShapeDtypeStruct((n_seq, HEADS // 2, n2, n2), F32)
    o_shape = jax.ShapeDtypeStruct((n_seq, nc, CHUNK, hw), F32)
    o_f, o_b, s_f, s_b = pl.pallas_call(
        functools.partial(_gdn_kernel, has_s0=s0 is not None, n_chunks=nc, n_par=n_par),
        grid=(n_seq // n_par, nc),
        in_specs=in_specs,
        out_specs=[fwd, bwd, state, state],
        out_shape=[o_shape, o_shape, st_shape, st_shape],
        scratch_shapes=[pltpu.VMEM((2, n_par, HEADS // 2, n2, n2), F32)],
        compiler_params=_cparams(("parallel", "arbitrary")),
        name="gdn_scan",
    )(*args)
    return o_f.reshape(n_seq * seq, hw), o_b.reshape(n_seq * seq, hw), s_f, s_b


def _state_to_pairs(s):
    b = s.shape[0]
    s = s.reshape(b, HEADS // 2, 2, HD, HD)
    z = jnp.zeros_like(s[:, :, 0])
    top = jnp.concatenate([z, s[:, :, 1]], axis=-1)
    bot = jnp.concatenate([s[:, :, 0], z], axis=-1)
    return jnp.concatenate([top, bot], axis=-2)


def _pairs_to_state(s):
    b = s.shape[0]
    even = s[:, :, HD:, :HD]
    odd = s[:, :, :HD, HD:]
    return jnp.stack([even, odd], axis=2).reshape(b, HEADS, HD, HD)


def _merge_kernel(x_ref, mod_ref, *refs, ctx_blocks):
    pairs, (n1_ref, wgz_ref, gain_ref, ones_ref, wa_ref, wb_ref, wc_ref, wo_ref, o_ref) = refs[:8], refs[8:]
    is_ctx = pl.program_id(0) < ctx_blocks
    ya, yb, o_f, o_b = [jnp.where(is_ctx, pairs[2 * t][...], pairs[2 * t + 1][...]) for t in range(4)]
    m = mod_ref[0, 0]
    chunks = [slice(r, r + ROW_CHUNK) for r in range(0, x_ref.shape[0], ROW_CHUNK)]
    pa = [_dot(ya[c], wa_ref[0]) for c in chunks]
    pb = [_dot(yb[c], wb_ref[0]) for c in chunks]
    x = [x_ref[c, :] for c in chunks]
    h = [(_rms_rows(xc, n1_ref[...]) * (1.0 + m[1:2]) + m[0:1]).astype(BF16) for xc in x]
    gz = [_dot(hc, wgz_ref[0]) for hc in h]
    yc = []
    for c, gzc in zip(chunks, gz):
        o = o_f[c] + o_b[c]
        ms = _group_sums(o * o, ones_ref[...]) * (1.0 / HD)
        yc.append((o * lax.rsqrt(ms + EPS) * gain_ref[...] * _silu(gzc[:, 3 * D_MODEL:])).astype(BF16))
    pc = [_dot(ycc, wc_ref[0]) for ycc in yc]
    merged = [(jax.nn.sigmoid(g[:, :D_MODEL]) * a + jax.nn.sigmoid(g[:, D_MODEL:2 * D_MODEL]) * b
               + jax.nn.sigmoid(g[:, 2 * D_MODEL:3 * D_MODEL]) * cc).astype(BF16)
              for g, a, b, cc in zip(gz, pa, pb, pc)]
    mix = [_dot(mc, wo_ref[0]) for mc in merged]
    for c, xc, mixc in zip(chunks, x, mix):
        o_ref[c, :] = xc + m[2:3] * mixc


def _merge(x, mod, layer, mixers, norm_gain, w_gz, gain, wa, wb, wc, wo, mod_row, ctx_rows):
    m_rows = x.shape[0]
    tm = TM_DENSE
    nb_ctx = ctx_rows // tm
    nb_lat = (m_rows - ctx_rows) // tm
    ctx_half = pl.BlockSpec((tm, 512), lambda i: (jnp.minimum(i, nb_ctx - 1), 0))
    lat_half = pl.BlockSpec((tm, 512), lambda i: (jnp.clip(i - nb_ctx, 0, nb_lat - 1), 0))
    full = pl.BlockSpec((tm, D_MODEL), lambda i: (i, 0))

    def wspec(k):
        return pl.BlockSpec((1, k, D_MODEL), lambda i: (layer, 0, 0))

    return pl.pallas_call(
        functools.partial(_merge_kernel, ctx_blocks=nb_ctx),
        grid=(m_rows // tm,),
        in_specs=[full,
                  pl.BlockSpec((1, 1, N_MOD, D_MODEL), lambda i: (layer, mod_row(i, tm), 0, 0))]
        + [ctx_half, lat_half] * 4
        + [pl.BlockSpec((1, D_MODEL), lambda i: (0, 0)),
           pl.BlockSpec((1, D_MODEL, N_GZ), lambda i: (layer, 0, 0)),
           pl.BlockSpec((1, 512), lambda i: (0, 0)),
           pl.BlockSpec((256, 256), lambda i: (0, 0)),
           wspec(512), wspec(512), wspec(512), wspec(D_MODEL)],
        out_specs=full,
        out_shape=jax.ShapeDtypeStruct((m_rows, D_MODEL), F32),
        compiler_params=_cparams(("parallel",)),
        name="merge_out",
    )(x, mod, *mixers, norm_gain, w_gz, gain, _ones_blockdiag(256, HD), wa, wb, wc, wo)


def _ffn_kernel(x_ref, mod_ref, g_ref, wg_ref, wu_ref, wd_ref, fin_ref, o_ref, *, final):
    m = mod_ref[0, 0]
    chunks = [slice(r, r + ROW_CHUNK) for r in range(0, x_ref.shape[0], ROW_CHUNK)]
    x = [x_ref[c, :] for c in chunks]
    h = [(_rms_rows(xc, g_ref[...]) * (1.0 + m[4:5]) + m[3:4]).astype(BF16) for xc in x]
    gate = [_dot(hc, wg_ref[0]) for hc in h]
    up = [_dot(hc, wu_ref[0]) for hc in h]
    act = [(_silu(gc) * uc).astype(BF16) for gc, uc in zip(gate, up)]
    down = [_dot(ac, wd_ref[0]) for ac in act]
    for c, xc, dc in zip(chunks, x, down):
        out = xc + m[5:6] * dc
        if final:
            out = _rms_rows(out, fin_ref[...])
        o_ref[c, :] = out


def _ffn(x, mod, layer, gain, wg, wu, wd, fin, mod_row, final):
    m_rows = x.shape[0]
    tm = TM_DENSE
    full = pl.BlockSpec((tm, D_MODEL), lambda i: (i, 0))
    vec = pl.BlockSpec((1, D_MODEL), lambda i: (0, 0))
    once = pl.Buffered(1)
    return pl.pallas_call(
        functools.partial(_ffn_kernel, final=final),
        grid=(m_rows // tm,),
        in_specs=[full,
                  pl.BlockSpec((1, 1, N_MOD, D_MODEL), lambda i: (layer, mod_row(i, tm), 0, 0)),
                  vec,
                  pl.BlockSpec((1, D_MODEL, D_FF), lambda i: (layer, 0, 0), pipeline_mode=once),
                  pl.BlockSpec((1, D_MODEL, D_FF), lambda i: (layer, 0, 0), pipeline_mode=once),
                  pl.BlockSpec((1, D_FF, D_MODEL), lambda i: (layer, 0, 0), pipeline_mode=once),
                  vec],
        out_specs=full,
        out_shape=jax.ShapeDtypeStruct((m_rows, D_MODEL), F32),
        compiler_params=_cparams(("parallel",)),
        name="ffn",
    )(x, mod, gain, wg, wu, wd, fin)


def _layout_w_in(w_in):
    def cols(a, b):
        return w_in[:, :, a:b]

    pad = jnp.zeros(w_in.shape[:2] + (64,), w_in.dtype)
    parts = [cols(1440, 2976), cols(768, 1440), cols(3488, 3520), pad, cols(512, 768), cols(0, 512)]
    w_tok = jnp.concatenate(parts, axis=-1).astype(BF16)
    w_gz = jnp.concatenate([cols(3520, 6592), cols(2976, 3488)], axis=-1).astype(BF16)
    return w_tok, w_gz


def _layout_w_uq(w_uq):
    n_layers = w_uq.shape[0]
    w = w_uq.reshape(n_layers, Q_LORA, HEADS, NOPE_B + ROPE_B)
    w = jnp.pad(w, ((0, 0), (0, 0), (0, 0), (0, LANES - NOPE_B - ROPE_B)))
    return w.reshape(n_layers, Q_LORA, HEADS * LANES).astype(BF16)


def _layout_w_ukv(w_ukv):
    n_layers = w_ukv.shape[0]
    w = w_ukv.reshape(n_layers, KV_LORA, HEADS, 2, HD)
    zero = jnp.zeros((n_layers, KV_LORA, HEADS, HD), w.dtype)
    wk_top = jnp.concatenate([w[:, :, :, 0], zero], axis=-1).reshape(n_layers, KV_LORA, HEADS * LANES)
    place = np.zeros((LANES, HEADS, LANES), np.float32)
    for d in range(ROPE_B):
        place[d, :, NOPE_B + d] = 1.0
    place = jnp.broadcast_to(jnp.asarray(place.reshape(LANES, HEADS * LANES)), (n_layers, LANES, HEADS * LANES))
    wk = jnp.concatenate([wk_top, place], axis=1).astype(BF16)
    wv = w[:, :, :, 1].reshape(n_layers, KV_LORA, HEADS * HD).astype(BF16)
    return wk, wv


def kernel(x_prompt, x_sample, cache_ka, cache_va, cache_ckv, cache_kpe, state_fwd, state_bwd, c, c_ctx,
           w_mod, b_mod, norm1, norm2, w_in, a_qnorm, a_knorm, b_qnorm, b_kvnorm, w_uq, w_ukv, c_conv,
           c_alog, c_dt_bias, c_onorm, w_pa, w_pb, w_pc, w_out, w_gate, w_up, w_down, final_norm):
    n_ctx, s_ctx, _ = x_prompt.shape
    n_lat, s_lat, _ = x_sample.shape
    n_layers = w_mod.shape[0]
    past = cache_ka.shape[2]
    ctx_rows = n_ctx * s_ctx
    lat_rows = n_lat * s_lat
    assert s_ctx == TOK and s_lat % TOK == 0 and ctx_rows % 512 == 0 and s_lat % 512 == 0
    assert past % TOK == 0 and n_lat + 1 <= 8 and ctx_rows % s_lat == 0

    def mod_row(i, tm):
        return jnp.where(i < ctx_rows // tm, 0, 1 + (i - ctx_rows // tm) // (s_lat // tm))

    def tab_row(i):
        nb = ctx_rows // TOK_AB
        return jnp.where(i < nb, 0, 1 + (i - nb) % (s_lat // TOK_AB))

    x = jnp.concatenate([x_prompt.reshape(ctx_rows, D_MODEL), x_sample.reshape(lat_rows, D_MODEL)], axis=0)
    cvec = jnp.zeros((8, D_MODEL), F32).at[0].set(c_ctx).at[1:1 + n_lat].set(c)
    mod = _modulation(cvec, w_mod, b_mod).reshape(n_layers, 8, N_MOD, D_MODEL)
    cos, sin = _rope_tables(s_lat)

    w_in_p, w_gz = _layout_w_in(w_in)
    w_uq_p = _layout_w_uq(w_uq)
    w_k_p, w_v_p = _layout_w_ukv(w_ukv)
    bf = lambda w: w.astype(BF16)
    w_pa_b, w_pb_b, w_pc_b, w_out_b = bf(w_pa), bf(w_pb), bf(w_pc), bf(w_out)
    w_gate_b, w_up_b, w_down_b = bf(w_gate), bf(w_up), bf(w_down)

    cache_rows = n_lat * n_layers * past
    ck4, cv4 = _cache_a(cache_ka.reshape(cache_rows, LANES), cache_va.reshape(cache_rows, LANES))
    kpe_pad = jnp.pad(cache_kpe.reshape(cache_rows, ROPE_B), ((0, 0), (0, LANES - ROPE_B)))
    ckb, cvb = _cache_b(cache_ckv.reshape(cache_rows, KV_LORA), kpe_pad, w_k_p, w_v_p,
                        lambda i: (i // (past // TOK)) % n_layers)

    lane_vec = lambda v16: jnp.zeros((1, LANES), F32).at[0, 32:48].set(v16.reshape(16))
    new = []
    for l in range(n_layers):
        y, va_raw, qa, k4, v4, kn_a, qb, kb, vb, ckvn = _token_stage(
            x, mod, l, norm1[l][None], w_in_p, cos, sin,
            jnp.tile(a_qnorm[l], HEADS)[None], jnp.tile(a_knorm[l], KV_HEADS_A)[None],
            b_qnorm[l][None], b_kvnorm[l][None], w_uq_p, w_k_p, w_v_p, mod_row, tab_row)
        conv_args = (c_conv[l, :, 0, :], lane_vec(c_alog[l]), lane_vec(c_dt_bias[l]))
        gdn_ctx = _prep_c(y, *conv_args, 0, ctx_rows, s_ctx)
        gdn_lat = _prep_c(y, *conv_args, ctx_rows, lat_rows, s_lat)

        cache_blk = (past, lambda b, l=l: b * n_layers + l)
        ya_c = _attention(qa, k4, v4, None, GROUPS_A, 0, n_ctx, s_ctx, s_ctx, None)
        ya_l = _attention(qa, k4, v4, (ck4, cv4), GROUPS_A, ctx_rows, n_lat, s_lat, 256, cache_blk)
        yb_c = _attention(qb, kb, vb, None, GROUPS_B, 0, n_ctx, s_ctx, s_ctx, None)
        yb_l = _attention(qb, kb, vb, (ckb, cvb), GROUPS_B, ctx_rows, n_lat, s_lat, 256, cache_blk)

        of_c, ob_c, sf_c, sb_c = _gdn_scan(*gdn_ctx, None, n_ctx, s_ctx)
        s0 = (_state_to_pairs(state_fwd[:, l]), _state_to_pairs(state_bwd[:, l]))
        of_l, ob_l, _, _ = _gdn_scan(*gdn_lat, s0, n_lat, s_lat)

        x = _merge(x, mod, l, (ya_c, ya_l, yb_c, yb_l, of_c, of_l, ob_c, ob_l), norm1[l][None], w_gz,
                   jnp.tile(c_onorm[l], HEADS)[None], w_pa_b, w_pb_b, w_pc_b, w_out_b, mod_row, ctx_rows)
        x = _ffn(x, mod, l, norm2[l][None], w_gate_b, w_up_b, w_down_b, final_norm[None], mod_row,
                 final=(l == n_layers - 1))

        new.append((kn_a[:ctx_rows].reshape(n_ctx, s_ctx, KV_HEADS_A, HD),
                    va_raw[:ctx_rows].reshape(n_ctx, s_ctx, KV_HEADS_A, HD),
                    ckvn[:ctx_rows].reshape(n_ctx, s_ctx, KV_LORA),
                    y[:ctx_rows, CONV_CH:CONV_CH + ROPE_B].reshape(n_ctx, s_ctx, ROPE_B),
                    _pairs_to_state(sf_c), _pairs_to_state(sb_c)))

    y_prompt = x[:ctx_rows].reshape(n_ctx, s_ctx, D_MODEL)
    y_sample = x[ctx_rows:].reshape(n_lat, s_lat, D_MODEL)
    stacked = [jnp.stack([n[i] for n in new], axis=1) for i in range(6)]
    return (y_prompt, y_sample) + tuple(stacked)
```

```python
import functools

import numpy as np
import jax
import jax.numpy as jnp
from jax import lax
from jax.experimental import pallas as pl
from jax.experimental.pallas import tpu as pltpu

F32 = jnp.float32
BF16 = jnp.bfloat16

D_MODEL = 1024
GRID_W = 64
GRID_W_LOG2 = 6
assert 1 << GRID_W_LOG2 == GRID_W
ROPE_THETA = 10000.0
EPS = 1e-6
N_MOD = 6
HEADS = 8
KV_HEADS_A = 2
HD = 64
Q_LORA = 384
KV_LORA = 256
NOPE_B = 64
ROPE_B = 32
CHUNK = 64
CONV_CH = 3 * HEADS * HD
D_FF = 2816

C_QKV = 0
C_B = 1536
C_KA = 2304
C_VA = 2432
C_QA = 2560
N_IN = 3072
N_GZ = 3 * D_MODEL + HEADS * HD

LOG2E = 1.4426950408889634
LANES = 128
TOK = 256
TOK_AB = 512
TM_DENSE = 512
ROW_CHUNK = 128
VMEM_LIMIT = 56 * 1024 * 1024


def _cparams(sem):
    return pltpu.CompilerParams(dimension_semantics=sem, vmem_limit_bytes=VMEM_LIMIT)


def _dot(a, b):
    return jnp.dot(a, b, preferred_element_type=F32)


def _dot_nt(a, b):
    return lax.dot_general(a, b, (((1,), (1,)), ((), ())), preferred_element_type=F32)


def _dot_tn(a, b):
    return lax.dot_general(a, b, (((0,), (0,)), ((), ())), preferred_element_type=F32)


def _split3(x):
    hi = x.astype(BF16)
    r = x - hi.astype(F32)
    mid = r.astype(BF16)
    lo = (r - mid.astype(F32)).astype(BF16)
    return hi, mid, lo


def _dot3(x, w):
    hi, mid, lo = _split3(x)
    return _dot(hi, w) + _dot(mid, w) + _dot(lo, w)


def _group_sums(x, ones):
    w = min(x.shape[1], ones.shape[0])
    blk = ones[:w, :w]
    slabs = []
    for s in range(x.shape[1] // w):
        xs = x[:, s * w:(s + 1) * w]
        hi = xs.astype(BF16)
        lo = (xs - hi.astype(F32)).astype(BF16)
        slabs.append(_dot(hi, blk) + _dot(lo, blk))
    return slabs[0] if len(slabs) == 1 else jnp.concatenate(slabs, axis=1)


def _silu(x):
    return x * jax.nn.sigmoid(x)


def _rms_rows(x, g):
    return x * lax.rsqrt(jnp.mean(x * x, axis=-1, keepdims=True) + EPS) * g


def _swap_pairs(x, half):
    lane = lax.broadcasted_iota(jnp.int32, x.shape, 1)
    n = x.shape[1]
    return jnp.where((lane & half) == 0, pltpu.roll(x, n - half, 1), pltpu.roll(x, half, 1))


def _rope_cols(x, cos, sin, half):
    outs = []
    for s in range(x.shape[1] // LANES):
        xs = x[:, s * LANES:(s + 1) * LANES]
        outs.append(xs * cos + _swap_pairs(xs, half) * sin)
    return outs[0] if len(outs) == 1 else jnp.concatenate(outs, axis=1)


def _mod_kernel(c_ref, w_ref, b_ref, o_ref):
    s = _silu(c_ref[...]).astype(BF16)
    o_ref[0] = _dot(s, w_ref[0].astype(BF16)) + b_ref[0]


def _modulation(cvec, w_mod, b_mod):
    n_layers = w_mod.shape[0]
    n_out = w_mod.shape[2]
    tn = 1536
    return pl.pallas_call(
        _mod_kernel,
        grid=(n_layers, n_out // tn),
        in_specs=[pl.BlockSpec((8, D_MODEL), lambda l, j: (0, 0)),
                  pl.BlockSpec((1, D_MODEL, tn), lambda l, j: (l, 0, j)),
                  pl.BlockSpec((1, 1, tn), lambda l, j: (l, 0, j))],
        out_specs=pl.BlockSpec((1, 8, tn), lambda l, j: (l, 0, j)),
        out_shape=jax.ShapeDtypeStruct((n_layers, 8, n_out), F32),
        compiler_params=_cparams(("parallel", "parallel")),
        name="modulation",
    )(cvec, w_mod, b_mod.reshape(n_layers, 1, n_out))


def _x_rows(x_refs, ctx_blocks, rows):
    if len(x_refs) == 1:
        return x_refs[0][rows, :]
    return jnp.where(pl.program_id(0) < ctx_blocks, x_refs[0][rows, :], x_refs[1][rows, :])


def _x_specs(x_parts, tm, ctx_rows):
    if len(x_parts) == 1:
        return [pl.BlockSpec((tm, D_MODEL), lambda i: (i, 0))]
    nb_ctx, nb_lat = ctx_rows // tm, x_parts[1].shape[0] // tm
    return [pl.BlockSpec((tm, D_MODEL), lambda i: (jnp.minimum(i, nb_ctx - 1), 0)),
            pl.BlockSpec((tm, D_MODEL), lambda i: (jnp.clip(i - nb_ctx, 0, nb_lat - 1), 0))]


def _token_kernel(*refs, n_x, ctx_blocks):
    x_refs = refs[:n_x]
    (mod_ref, g_ref, w_ref, cos_a, sin_a, cos_q, sin_q, cos_k, sin_k, gq_a, gk_a, ones_ref, place_ref,
     gq_b, gkv_b, wq_ref, wk_ref, wv_ref,
     yc_ref, va_ref, qa_ref, k4_ref, vta_ref, kna_ref, qb_ref, kb_ref, vtb_ref, ckv_ref, y_scr) = refs[n_x:]
    m = mod_ref[0, 0]
    for r in range(0, y_scr.shape[0], ROW_CHUNK):
        rows = slice(r, r + ROW_CHUNK)
        y = _rms_rows(_x_rows(x_refs, ctx_blocks, rows), g_ref[...])
        h = (y * (1.0 + m[1:2]) + m[0:1]).astype(BF16)
        y_scr[rows, :] = _dot(h, w_ref[0])
    yc_ref[:, :CONV_CH] = y_scr[:, C_QKV:C_QKV + CONV_CH]
    yc_ref[:, CONV_CH:] = y_scr[:, C_B + 640:C_B + 640 + LANES]
    va_ref[...] = y_scr[:, C_VA:C_VA + LANES]
    _prep_a_kernel(y_scr.at[:, C_QA:C_QA + HEADS * HD], y_scr.at[:, C_KA:C_KA + 2 * LANES],
                   cos_a, sin_a, gq_a, gk_a, ones_ref, place_ref, qa_ref, k4_ref, vta_ref, kna_ref)
    _prep_b_kernel(y_scr.at[:, C_B:C_B + 768], cos_q, sin_q, cos_k, sin_k, gq_b, gkv_b,
                   wq_ref, wk_ref, wv_ref, qb_ref, kb_ref, vtb_ref, ckv_ref)


def _token_stage(x_parts, mod, layer, norm_gain, w, cos, sin, gq_a, gk_a, gq_b, gkv_b, wq, wk, wv,
                 mod_row, tab_row, ctx_rows):
    m_rows = sum(p.shape[0] for p in x_parts)
    tm = TOK_AB
    rows = lambda w_: pl.BlockSpec((tm, w_), lambda i: (i, 0))
    cols = lambda h_: pl.BlockSpec((h_, tm), lambda i: (0, i))
    vec = lambda w_: pl.BlockSpec((1, w_), lambda i: (0, 0))
    tab = lambda a: pl.BlockSpec((1, tm, LANES), lambda i: (a, tab_row(i), 0))
    lw = lambda k, n: pl.BlockSpec((1, k, n), lambda i: (layer, 0, 0))
    f32 = lambda *shape: jax.ShapeDtypeStruct(shape, F32)
    bf16 = lambda *shape: jax.ShapeDtypeStruct(shape, BF16)
    return pl.pallas_call(
        functools.partial(_token_kernel, n_x=len(x_parts), ctx_blocks=ctx_rows // tm),
        grid=(m_rows // tm,),
        in_specs=_x_specs(x_parts, tm, ctx_rows)
        + [pl.BlockSpec((1, 1, N_MOD, D_MODEL), lambda i: (layer, mod_row(i, tm), 0, 0)),
                  vec(D_MODEL), lw(D_MODEL, N_IN),
                  tab(0), tab(0), tab(1), tab(1), tab(2), tab(2),
                  vec(HEADS * HD), vec(LANES),
                  pl.BlockSpec((256, 256), lambda i: (0, 0)), pl.BlockSpec((LANES, 4 * LANES), lambda i: (0, 0)),
                  vec(Q_LORA), vec(KV_LORA),
                  lw(Q_LORA, HEADS * LANES), lw(KV_LORA + LANES, HEADS * LANES), lw(KV_LORA, HEADS * HD)],
        out_specs=[rows(CONV_CH + LANES), rows(LANES),
                   rows(4 * LANES), rows(4 * LANES), cols(LANES), rows(LANES),
                   rows(HEADS * LANES), rows(HEADS * LANES), cols(HEADS * HD), rows(KV_LORA)],
        out_shape=[f32(m_rows, CONV_CH + LANES), f32(m_rows, LANES),
                   bf16(m_rows, 4 * LANES), bf16(m_rows, 4 * LANES), bf16(LANES, m_rows), f32(m_rows, LANES),
                   bf16(m_rows, HEADS * LANES), bf16(m_rows, HEADS * LANES), bf16(HEADS * HD, m_rows),
                   f32(m_rows, KV_LORA)],
        scratch_shapes=[pltpu.VMEM((tm, N_IN), F32)],
        compiler_params=_cparams(("parallel",)),
        name="token_stage",
    )(*x_parts, mod, norm_gain, w, cos, sin, cos, sin, cos, sin, gq_a, gk_a, _ones_blockdiag(256, HD),
      _place4(), gq_b, gkv_b, wq, wk, wv)


def _rope_table_kernel(freq_ref, col_ref, sign_ref, cos_ref, sin_ref):
    shape = cos_ref.shape
    pos = pl.program_id(1) * TOK_AB + lax.broadcasted_iota(jnp.int32, shape, 1) - TOK_AB
    grid_row = (pos >> GRID_W_LOG2).astype(F32)
    grid_col = (pos & (GRID_W - 1)).astype(F32)
    a = jnp.where(col_ref[...] > 0.5, grid_col, grid_row) * freq_ref[...]
    a = jnp.where(pos >= 0, a, 0.0)
    cos_ref[...] = jnp.cos(a)
    sin_ref[...] = jnp.sin(a) * sign_ref[...]


def _rope_tables(n_lat):
    lane = np.arange(LANES)

    def lane_pattern(nf, d, active):
        inv_freq = np.float32(ROPE_THETA) ** (-np.arange(nf, dtype=np.float32) / np.float32(nf))
        dd = np.where(active, d, 0)
        freq = np.where(active, inv_freq[dd % nf], 0.0).astype(np.float32)
        use_col = (dd >= 2 * nf).astype(np.float32)
        sign = np.where((dd % (2 * nf)) < nf, -1.0, 1.0).astype(np.float32)
        return freq, use_col, sign

    d_q = lane - NOPE_B
    patterns = [lane_pattern(HD // 4, lane % HD, np.ones(LANES, bool)),
                lane_pattern(ROPE_B // 4, d_q, (d_q >= 0) & (d_q < ROPE_B)),
                lane_pattern(ROPE_B // 4, lane, lane < ROPE_B)]
    freq, use_col, sign = [jnp.asarray(np.stack([p[t] for p in patterns]))[:, None, :] for t in range(3)]
    rows = TOK_AB + n_lat
    spec = pl.BlockSpec((1, TOK_AB, LANES), lambda a, i: (a, i, 0))
    vec = pl.BlockSpec((1, 1, LANES), lambda a, i: (a, 0, 0))
    return pl.pallas_call(
        _rope_table_kernel,
        grid=(3, rows // TOK_AB),
        in_specs=[vec, vec, vec],
        out_specs=[spec, spec],
        out_shape=[jax.ShapeDtypeStruct((3, rows, LANES), F32)] * 2,
        compiler_params=_cparams(("parallel", "parallel")),
        name="rope_tables",
    )(freq, use_col, sign)


def _prep_a_kernel(qa_ref, kv_ref, cos_ref, sin_ref, gq_ref, gk_ref, ones_ref, place_ref,
                   q_ref, k4_ref, vt_ref, kn_ref):
    cos = cos_ref[0]
    sin = sin_ref[0]
    ones = ones_ref[...]
    q = qa_ref[...]
    qn = q * lax.rsqrt(_group_sums(q * q, ones) * (1.0 / HD) + EPS) * gq_ref[...]
    q_ref[...] = (_rope_cols(qn, cos, sin, HD // 4) * (HD ** -0.5 * LOG2E)).astype(BF16)
    k = kv_ref[:, :LANES]
    kn = k * lax.rsqrt(_group_sums(k * k, ones) * (1.0 / HD) + EPS) * gk_ref[...]
    kn_ref[...] = kn
    k4_ref[...] = _dot(_rope_cols(kn, cos, sin, HD // 4).astype(BF16), place_ref[...]).astype(BF16)
    vt_ref[...] = kv_ref[:, LANES:].T.astype(BF16)


def _place4():
    p = np.zeros((LANES, 4 * LANES), np.float32)
    for kv in range(KV_HEADS_A):
        for side in range(2):
            for d in range(HD):
                p[kv * HD + d, (2 * kv + side) * LANES + side * HD + d] = 1.0
    return jnp.asarray(p, BF16)


def _ones_blockdiag(n, group):
    idx = np.arange(n) // group
    return jnp.asarray((idx[:, None] == idx[None, :]).astype(np.float32), BF16)


def _cache_a_kernel(k_ref, v_ref, place_ref, k4_ref, vt_ref):
    k4_ref[...] = _dot(k_ref[...].astype(BF16), place_ref[...]).astype(BF16)
    vt_ref[...] = v_ref[...].T.astype(BF16)


def _cache_a(ck, cv):
    rows = ck.shape[0]
    narrow = pl.BlockSpec((TOK, LANES), lambda i: (i, 0))
    return pl.pallas_call(
        _cache_a_kernel,
        grid=(rows // TOK,),
        in_specs=[narrow, narrow, pl.BlockSpec((LANES, 512), lambda i: (0, 0))],
        out_specs=[pl.BlockSpec((TOK, 512), lambda i: (i, 0)), pl.BlockSpec((LANES, TOK), lambda i: (0, i))],
        out_shape=[jax.ShapeDtypeStruct((rows, 512), BF16), jax.ShapeDtypeStruct((LANES, rows), BF16)],
        compiler_params=_cparams(("parallel",)),
        name="cache_a",
    )(ck, cv, _place4())


def _prep_b_kernel(g_ref, cq_ref, sq_ref, ck_ref, sk_ref, gq_ref, gkv_ref, wq_ref, wk_ref, wv_ref,
                   q_ref, k_ref, vt_ref, ckv_ref):
    g = g_ref[...]
    cqn = _rms_rows(g[:, :Q_LORA], gq_ref[...])
    q = _dot(cqn.astype(BF16), wq_ref[0])
    q = _rope_cols(q, cq_ref[0], sq_ref[0], ROPE_B // 4) * ((NOPE_B + ROPE_B) ** -0.5 * LOG2E)
    q_ref[...] = q.astype(BF16)
    ckvn = _rms_rows(g[:, Q_LORA:Q_LORA + KV_LORA], gkv_ref[...])
    ckv_ref[...] = ckvn
    kpe = _rope_cols(g[:, Q_LORA + KV_LORA:], ck_ref[0], sk_ref[0], ROPE_B // 4)
    ckv_b = ckvn.astype(BF16)
    kin = jnp.concatenate([ckv_b, kpe.astype(BF16)], axis=1)
    k_ref[...] = _dot(kin, wk_ref[0]).astype(BF16)
    vt_ref[...] = _dot(ckv_b, wv_ref[0]).T.astype(BF16)


def _cache_b_kernel(c_ref, p_ref, wk_ref, wv_ref, k_ref, vt_ref):
    ckv_b = c_ref[...].astype(BF16)
    kin = jnp.concatenate([ckv_b, p_ref[...].astype(BF16)], axis=1)
    k_ref[...] = _dot(kin, wk_ref[0]).astype(BF16)
    vt_ref[...] = _dot(ckv_b, wv_ref[0]).T.astype(BF16)


def _cache_b(ckv, kpe, wk, wv, layer_of_block):
    rows = ckv.shape[0]
    return pl.pallas_call(
        _cache_b_kernel,
        grid=(rows // TOK,),
        in_specs=[pl.BlockSpec((TOK, KV_LORA), lambda i: (i, 0)),
                  pl.BlockSpec((TOK, LANES), lambda i: (i, 0)),
                  pl.BlockSpec((1, KV_LORA + LANES, HEADS * LANES), lambda i: (layer_of_block(i), 0, 0)),
                  pl.BlockSpec((1, KV_LORA, HEADS * HD), lambda i: (layer_of_block(i), 0, 0))],
        out_specs=[pl.BlockSpec((TOK, HEADS * LANES), lambda i: (i, 0)),
                   pl.BlockSpec((HEADS * HD, TOK), lambda i: (0, i))],
        out_shape=[jax.ShapeDtypeStruct((rows, HEADS * LANES), BF16),
                   jax.ShapeDtypeStruct((HEADS * HD, rows), BF16)],
        compiler_params=_cparams(("parallel",)),
        name="cache_b",
    )(ckv, kpe, wk, wv)


ATT_KEY_TILE = 256
ATT_LOOKAHEAD = 5


def _attn_kernel(*refs, groups, has_cache):
    if has_cache:
        q_ref, kc_ref, vc_ref, kn_ref, vn_ref, o_ref = refs
        segments = [(kc_ref, vc_ref), (kn_ref, vn_ref)]
    else:
        q_ref, kn_ref, vn_ref, o_ref = refs
        segments = [(kn_ref, vn_ref)]
    tiles = [(k_ref, v_ref, t * ATT_KEY_TILE) for k_ref, v_ref in segments
             for t in range(k_ref.shape[0] // ATT_KEY_TILE)]
    q = [q_ref[:, qg * LANES:(qg + 1) * LANES] for qg, _, _ in groups]

    def scores(h, i):
        k_ref, _, t0 = tiles[i]
        kg = groups[h][1]
        return _dot_nt(k_ref[t0:t0 + ATT_KEY_TILE, kg * LANES:(kg + 1) * LANES], q[h])

    heads = range(len(groups))
    jobs = [(i, h) for i in range(len(tiles)) for h in heads]
    pending = {j: scores(h, i) for j, (i, h) in enumerate(jobs[:ATT_LOOKAHEAD])}
    ones_rows = (lax.broadcasted_iota(jnp.int32, (16, ATT_KEY_TILE), 0) == 0).astype(BF16)
    m, acc = {}, {}
    for j, (i, h) in enumerate(jobs):
        if j + ATT_LOOKAHEAD < len(jobs):
            i_n, h_n = jobs[j + ATT_LOOKAHEAD]
            pending[j + ATT_LOOKAHEAD] = scores(h_n, i_n)
        s = pending.pop(j)
        _, v_ref, t0 = tiles[i]
        vrow = groups[h][2]
        v_t = jnp.concatenate([v_ref[vrow:vrow + HD, t0:t0 + ATT_KEY_TILE], ones_rows], axis=0)
        m_tile = jnp.max(s, axis=0, keepdims=True)
        if i == 0:
            m[h] = m_tile
            acc[h] = _dot(v_t, jnp.exp2(s - m_tile).astype(BF16))
        else:
            m_new = jnp.maximum(m[h], m_tile)
            alpha = jnp.exp2(m[h] - m_new)
            acc[h] = acc[h] * alpha + _dot(v_t, jnp.exp2(s - m_new).astype(BF16))
            m[h] = m_new
    outs = [acc[h][:HD] / acc[h][HD:HD + 1] for h in heads]
    o_ref[...] = jnp.concatenate(outs, axis=0).T.astype(BF16)


def _attention(q, kn, vn, cache, groups, row0, n_seq, seq, tq, cache_block):
    qw, kw, vr = q.shape[1], kn.shape[1], vn.shape[0]
    nq = seq // tq
    qb0, sb0 = row0 // tq, row0 // seq
    in_specs = [pl.BlockSpec((tq, qw), lambda b, i: (qb0 + b * nq + i, 0))]
    args = [q]
    if cache is not None:
        past = cache_block[0]
        in_specs += [pl.BlockSpec((past, kw), lambda b, i: (cache_block[1](b), 0)),
                     pl.BlockSpec((vr, past), lambda b, i: (0, cache_block[1](b)))]
        args += list(cache)
    in_specs += [pl.BlockSpec((seq, kw), lambda b, i: (sb0 + b, 0)),
                 pl.BlockSpec((vr, seq), lambda b, i: (0, sb0 + b))]
    args += [kn, vn]
    return pl.pallas_call(
        functools.partial(_attn_kernel, groups=groups, has_cache=cache is not None),
        grid=(n_seq, nq),
        in_specs=in_specs,
        out_specs=pl.BlockSpec((tq, 512), lambda b, i: (b * nq + i, 0)),
        out_shape=jax.ShapeDtypeStruct((n_seq * seq, 512), BF16),
        compiler_params=_cparams(("parallel", "parallel")),
        name="attention",
    )(*args)


GROUPS_A = tuple((h // 2, 2 * (h // 4) + (h % 2), (h // 4) * HD) for h in range(HEADS))
GROUPS_B = tuple((h, h, h * HD) for h in range(HEADS))


def _prep_c_kernel(x_ref, prev_ref, next_ref, ab_ref, w_ref, alog_ref, dtb_ref, ones_ref, exp_ref, tri_ref,
                   q_ref, k_ref, v_ref, gf_ref, gb_ref, bf_ref, bb_ref, *, blocks_per_seq):
    j = pl.program_id(0) % blocks_per_seq
    is_first, is_last = j == 0, j == blocks_per_seq - 1
    x = x_ref[...]
    rows = x.shape[0]
    rid = lax.broadcasted_iota(jnp.int32, x.shape, 0)
    halo_p = jnp.where(is_first, 0.0, prev_ref[7:8, :])
    halo_n = jnp.where(is_last, 0.0, next_ref[0:1, :])
    x_prev = jnp.where(rid == 0, halo_p, pltpu.roll(x, 1, 0))
    x_next = jnp.where(rid == rows - 1, halo_n, pltpu.roll(x, rows - 1, 0))
    w = w_ref[...]
    s = _silu(x_prev * w[0:1] + x * w[1:2] + x_next * w[2:3])
    ones = ones_ref[...]
    hw = HEADS * HD
    q, k = s[:, :hw], s[:, hw:2 * hw]
    q_ref[...] = q * lax.rsqrt(_group_sums(q * q, ones) + EPS) * (HD ** -0.5)
    k_ref[...] = k * lax.rsqrt(_group_sums(k * k, ones) + EPS)
    v_ref[...] = s[:, 2 * hw:]
    ab = ab_ref[...]
    t = ab + dtb_ref[...]
    softplus = jnp.maximum(t, 0.0) + jnp.log1p(jnp.exp(-jnp.abs(t)))
    g = -jnp.exp(alog_ref[...]) * softplus
    lane = lax.broadcasted_iota(jnp.int32, ab.shape, 1)
    parts = _split3(g)
    cum_f = sum(_dot(tri_ref[0], part) for part in parts)
    cum_b = sum(_dot(tri_ref[1], part) for part in parts)
    both = jnp.where(lane < 40, cum_f, jnp.where(lane < 48, cum_b, jax.nn.sigmoid(ab)))
    e = _dot3(both, exp_ref[...])
    gf_ref[...] = e[:, 0 * hw:1 * hw]
    gb_ref[...] = e[:, 1 * hw:2 * hw]
    bf_ref[...] = e[:, 2 * hw:3 * hw]
    bb_ref[...] = e[:, 3 * hw:4 * hw]


def _expand_matrix():
    e = np.zeros((LANES, 4 * HEADS * HD), np.float32)
    for kind in range(2):
        for d in range(2):
            for h in range(HEADS):
                src = 32 + 16 * kind + 8 * d + h
                dst = (2 * kind + d) * HEADS * HD + h * HD
                e[src, dst:dst + HD] = 1.0
    return jnp.asarray(e, BF16)


def _chunk_tri():
    r = np.arange(TOK)
    same = (r[:, None] // CHUNK) == (r[None, :] // CHUNK)
    low = same & (r[:, None] >= r[None, :])
    up = same & (r[:, None] <= r[None, :])
    return jnp.asarray(np.stack([low, up]).astype(np.float32), BF16)


def _prep_c(y, w_conv, alog, dtb, row0, n_rows, seq):
    nb8 = TOK // 8
    b0 = row0 // TOK
    last8 = (row0 + n_rows) // 8 - 1
    qkv_blk = 0
    out = pl.BlockSpec((TOK, HEADS * HD), lambda i: (i, 0))
    vec = pl.BlockSpec((1, LANES), lambda i: (0, 0))
    return pl.pallas_call(
        functools.partial(_prep_c_kernel, blocks_per_seq=seq // TOK),
        grid=(n_rows // TOK,),
        in_specs=[pl.BlockSpec((TOK, CONV_CH), lambda i: (b0 + i, qkv_blk)),
                  pl.BlockSpec((8, CONV_CH), lambda i: (jnp.maximum((b0 + i) * nb8 - 1, 0), qkv_blk)),
                  pl.BlockSpec((8, CONV_CH), lambda i: (jnp.minimum((b0 + i + 1) * nb8, last8), qkv_blk)),
                  pl.BlockSpec((TOK, LANES), lambda i: (b0 + i, CONV_CH // LANES)),
                  pl.BlockSpec((3, CONV_CH), lambda i: (0, 0)),
                  vec, vec,
                  pl.BlockSpec((256, 256), lambda i: (0, 0)),
                  pl.BlockSpec((LANES, 4 * HEADS * HD), lambda i: (0, 0)),
                  pl.BlockSpec((2, TOK, TOK), lambda i: (0, 0, 0))],
        out_specs=[out] * 7,
        out_shape=[jax.ShapeDtypeStruct((n_rows, HEADS * HD), F32)] * 7,
        compiler_params=_cparams(("parallel",)),
        name="prep_c",
    )(y, y, y, y, w_conv, alog, dtb, _ones_blockdiag(256, HD), _expand_matrix(), _chunk_tri())


def _gdn_kernel(*refs, has_s0, n_chunks, n_par):
    ins = refs[:10]
    pos = 10
    if has_s0:
        s0_refs = refs[pos:pos + 2]
        pos += 2
    o_refs = refs[pos:pos + 2]
    s_out_refs = refs[pos + 2:pos + 4]
    s_scr = refs[pos + 4]
    c = pl.program_id(1)

    @pl.when(c == 0)
    def _():
        for d in range(2):
            s_scr[d] = s0_refs[d][...] if has_s0 else jnp.zeros(s_scr.shape[1:], F32)

    n2 = 2 * CHUNK
    ri = lax.broadcasted_iota(jnp.int32, (n2, n2), 0)
    ci = lax.broadcasted_iota(jnp.int32, (n2, n2), 1)
    same = (ri // CHUNK) == (ci // CHUNK)
    left = lax.broadcasted_iota(jnp.int32, (CHUNK, n2), 1) < CHUNK
    eye = (ri == ci).astype(F32)

    def level_mask(lvl, lower):
        bit = 1 << lvl
        hi, lo = (ri, ci) if lower else (ci, ri)
        return ((ri >> (lvl + 1)) == (ci >> (lvl + 1))) & ((hi & bit) != 0) & ((lo & bit) == 0)

    levels = [[level_mask(lvl, d == 0) for lvl in range(6)] for d in range(2)]

    def block_diag(x):
        return jnp.concatenate([jnp.where(left, x, 0.0), jnp.where(left, 0.0, x)], axis=0)

    def anti_diag(x):
        xr = pltpu.roll(x, CHUNK, 1)
        return jnp.concatenate([jnp.where(left, 0.0, xr), jnp.where(left, xr, 0.0)], axis=0)

    chains = [(d, n, p) for n in range(n_par) for d in range(2) for p in range(HEADS // 2)]
    incl = [same & ((ri >= ci) if d == 0 else (ri <= ci)) for d in range(2)]
    strict = [same & ((ri > ci) if d == 0 else (ri < ci)) for d in range(2)]

    def load(which):
        return [ins[5 * d + which][n, 0, :, p * n2:(p + 1) * n2] for d, n, p in chains]

    q, k, v, gc, beta = load(0), load(1), load(2), load(3), load(4)
    gtot = [gi[CHUNK - 1:CHUNK, :] if d == 0 else gi[0:1, :] for (d, _, _), gi in zip(chains, gc)]
    eg = [jnp.exp(gi) for gi in gc]
    kb = [ki * bi for ki, bi in zip(k, beta)]
    decay = []
    for (d, _, _), gi in zip(chains, gc):
        gcr = pltpu.roll(gi, CHUNK, 1)
        gcol = jnp.concatenate([jnp.where(left, gi, gcr), jnp.where(left, gcr, gi)], axis=0)
        diff = gcol - gcol.T
        decay.append(jnp.where(incl[d], jnp.exp(jnp.where(incl[d], diff, 0.0)), 0.0))
    k_bd = [block_diag(ki).astype(BF16) for ki in k]
    kk = [_dot_nt(block_diag(kbi).astype(BF16), kbd) for kbi, kbd in zip(kb, k_bd)]
    qk = [_dot_nt(block_diag(qi).astype(BF16), kbd) for qi, kbd in zip(q, k_bd)]
    a_mat = [jnp.where(strict[d], kki * di, 0.0) for (d, _, _), kki, di in zip(chains, kk, decay)]
    qk = [(qki * di).astype(BF16) for qki, di in zip(qk, decay)]
    x = []
    for vi, bi, kbi, egi in zip(v, beta, kb, eg):
        vb = vi * bi
        kbe_r = pltpu.roll(kbi * egi, CHUNK, 1)
        x.append(jnp.concatenate([jnp.where(left, vb, kbe_r), jnp.where(left, kbe_r, vb)], axis=0))
    inv = [eye - jnp.where(levels[d][0], ai, 0.0) for (d, _, _), ai in zip(chains, a_mat)]
    a_b = [ai.astype(BF16) for ai in a_mat]
    for lvl in range(1, 6):
        d_b = [di.astype(BF16) for di in inv]
        t = [_dot(ai, di).astype(BF16) for ai, di in zip(a_b, d_b)]
        inv = [di - jnp.where(levels[d][lvl], _dot(dbi, ti), 0.0)
               for (d, _, _), di, dbi, ti in zip(chains, inv, d_b, t)]
    x = [xi + _dot((di - eye).astype(BF16), xi.astype(BF16)) for xi, di in zip(x, inv)]
    s = [s_scr[d, n, p] for d, n, p in chains]
    s_b = [si.astype(BF16) for si in s]
    ws = [_dot(jnp.where(same, 0.0, xi).astype(BF16), sbi) for xi, sbi in zip(x, s_b)]
    v_new = [(jnp.where(same, xi, 0.0) - wi).astype(BF16) for xi, wi in zip(x, ws)]
    o_s = [_dot(anti_diag(qi * egi).astype(BF16), sbi) for qi, egi, sbi in zip(q, eg, s_b)]
    o_v = [_dot(qki, vni) for qki, vni in zip(qk, v_new)]
    kd = [anti_diag(ki * jnp.exp(gt - gi)).astype(BF16) for ki, gt, gi in zip(k, gtot, gc)]
    s_upd = [_dot_tn(kdi, vni) for kdi, vni in zip(kd, v_new)]
    for i, (d, n, p) in enumerate(chains):
        o = o_s[i] + o_v[i]
        o_refs[d][n, 0, :, p * n2:(p + 1) * n2] = o[:CHUNK] + o[CHUNK:]
        s_scr[d, n, p] = s[i] * jnp.exp(gtot[i]) + s_upd[i]

    @pl.when(c == n_chunks - 1)
    def _():
        for d in range(2):
            s_out_refs[d][...] = s_scr[d]


GDN_SEQS_PER_STEP = 2


def _gdn_scan(qn, kn, v, gcf, gcb, bf, bb, s0, n_seq, seq):
    nc = seq // CHUNK
    hw = HEADS * HD
    n2 = 2 * CHUNK
    n_par = max(p for p in range(1, GDN_SEQS_PER_STEP + 1) if n_seq % p == 0)
    view = lambda a: a.reshape(n_seq, nc, CHUNK, hw)
    fwd = pl.BlockSpec((n_par, 1, CHUNK, hw), lambda b, c: (b, c, 0, 0))
    bwd = pl.BlockSpec((n_par, 1, CHUNK, hw), lambda b, c: (b, nc - 1 - c, 0, 0))
    state = pl.BlockSpec((n_par, HEADS // 2, n2, n2), lambda b, c: (b, 0, 0, 0))
    in_specs = [fwd] * 5 + [bwd] * 5
    args = [view(a) for a in (qn, kn, v, gcf, bf, qn, kn, v, gcb, bb)]
    if s0 is not None:
        in_specs += [state, state]
        args += list(s0)
    st_shape = jax.ShapeDtypeStruct((n_seq, HEADS // 2, n2, n2), F32)
    o_shape = jax.ShapeDtypeStruct((n_seq, nc, CHUNK, hw), F32)
    o_f, o_b, s_f, s_b = pl.pallas_call(
        functools.partial(_gdn_kernel, has_s0=s0 is not None, n_chunks=nc, n_par=n_par),
        grid=(n_seq // n_par, nc),
        in_specs=in_specs,
        out_specs=[fwd, bwd, state, state],
        out_shape=[o_shape, o_shape, st_shape, st_shape],
        scratch_shapes=[pltpu.VMEM((2, n_par, HEADS // 2, n2, n2), F32)],
        compiler_params=_cparams(("parallel", "arbitrary")),
        name="gdn_scan",
    )(*args)
    return o_f.reshape(n_seq * seq, hw), o_b.reshape(n_seq * seq, hw), s_f, s_b


def _state_to_pairs(s):
    b = s.shape[0]
    s = s.reshape(b, HEADS // 2, 2, HD, HD)
    z = jnp.zeros_like(s[:, :, 0])
    top = jnp.concatenate([z, s[:, :, 1]], axis=-1)
    bot = jnp.concatenate([s[:, :, 0], z], axis=-1)
    return jnp.concatenate([top, bot], axis=-2)


def _pairs_to_state(s):
    b = s.shape[0]
    even = s[:, :, HD:, :HD]
    odd = s[:, :, :HD, HD:]
    return jnp.stack([even, odd], axis=2).reshape(b, HEADS, HD, HD)


def _merge_kernel(*refs, n_x, ctx_blocks):
    x_refs, mod_ref, refs = refs[:n_x], refs[n_x], refs[n_x + 1:]
    pairs, (n1_ref, wgz_ref, gain_ref, ones_ref, wa_ref, wb_ref, wc_ref, wo_ref, o_ref) = refs[:8], refs[8:]
    is_ctx = pl.program_id(0) < ctx_blocks
    ya, yb, o_f, o_b = [jnp.where(is_ctx, pairs[2 * t][...], pairs[2 * t + 1][...]) for t in range(4)]
    m = mod_ref[0, 0]
    chunks = [slice(r, r + ROW_CHUNK) for r in range(0, o_ref.shape[0], ROW_CHUNK)]
    pa = [_dot(ya[c], wa_ref[0]) for c in chunks]
    pb = [_dot(yb[c], wb_ref[0]) for c in chunks]
    x = [_x_rows(x_refs, ctx_blocks, c) for c in chunks]
    h = [(_rms_rows(xc, n1_ref[...]) * (1.0 + m[1:2]) + m[0:1]).astype(BF16) for xc in x]
    gz = [_dot(hc, wgz_ref[0]) for hc in h]
    yc = []
    for c, gzc in zip(chunks, gz):
        o = o_f[c] + o_b[c]
        ms = _group_sums(o * o, ones_ref[...]) * (1.0 / HD)
        yc.append((o * lax.rsqrt(ms + EPS) * gain_ref[...] * _silu(gzc[:, 3 * D_MODEL:])).astype(BF16))
    pc = [_dot(ycc, wc_ref[0]) for ycc in yc]
    merged = [(jax.nn.sigmoid(g[:, :D_MODEL]) * a + jax.nn.sigmoid(g[:, D_MODEL:2 * D_MODEL]) * b
               + jax.nn.sigmoid(g[:, 2 * D_MODEL:3 * D_MODEL]) * cc).astype(BF16)
              for g, a, b, cc in zip(gz, pa, pb, pc)]
    mix = [_dot(mc, wo_ref[0]) for mc in merged]
    for c, xc, mixc in zip(chunks, x, mix):
        o_ref[c, :] = xc + m[2:3] * mixc


def _merge(x_parts, mod, layer, mixers, norm_gain, w_gz, gain, wa, wb, wc, wo, mod_row, ctx_rows):
    m_rows = sum(p.shape[0] for p in x_parts)
    tm = TM_DENSE
    nb_ctx = ctx_rows // tm
    nb_lat = (m_rows - ctx_rows) // tm
    ctx_half = pl.BlockSpec((tm, 512), lambda i: (jnp.minimum(i, nb_ctx - 1), 0))
    lat_half = pl.BlockSpec((tm, 512), lambda i: (jnp.clip(i - nb_ctx, 0, nb_lat - 1), 0))
    full = pl.BlockSpec((tm, D_MODEL), lambda i: (i, 0))

    def wspec(k):
        return pl.BlockSpec((1, k, D_MODEL), lambda i: (layer, 0, 0))

    return pl.pallas_call(
        functools.partial(_merge_kernel, n_x=len(x_parts), ctx_blocks=nb_ctx),
        grid=(m_rows // tm,),
        in_specs=_x_specs(x_parts, tm, ctx_rows)
        + [pl.BlockSpec((1, 1, N_MOD, D_MODEL), lambda i: (layer, mod_row(i, tm), 0, 0))]
        + [ctx_half, lat_half] * 4
        + [pl.BlockSpec((1, D_MODEL), lambda i: (0, 0)),
           pl.BlockSpec((1, D_MODEL, N_GZ), lambda i: (layer, 0, 0)),
           pl.BlockSpec((1, 512), lambda i: (0, 0)),
           pl.BlockSpec((256, 256), lambda i: (0, 0)),
           wspec(512), wspec(512), wspec(512), wspec(D_MODEL)],
        out_specs=full,
        out_shape=jax.ShapeDtypeStruct((m_rows, D_MODEL), F32),
        compiler_params=_cparams(("parallel",)),
        name="merge_out",
    )(*x_parts, mod, *mixers, norm_gain, w_gz, gain, _ones_blockdiag(256, HD), wa, wb, wc, wo)


def _ffn_kernel(x_ref, mod_ref, g_ref, wg_ref, wu_ref, wd_ref, fin_ref, o_ref, *, final):
    m = mod_ref[0, 0]
    chunks = [slice(r, r + ROW_CHUNK) for r in range(0, x_ref.shape[0], ROW_CHUNK)]
    x = [x_ref[c, :] for c in chunks]
    h = [(_rms_rows(xc, g_ref[...]) * (1.0 + m[4:5]) + m[3:4]).astype(BF16) for xc in x]
    gate = [_dot(hc, wg_ref[0]) for hc in h]
    up = [_dot(hc, wu_ref[0]) for hc in h]
    act = [(_silu(gc) * uc).astype(BF16) for gc, uc in zip(gate, up)]
    down = [_dot(ac, wd_ref[0]) for ac in act]
    for c, xc, dc in zip(chunks, x, down):
        out = xc + m[5:6] * dc
        if final:
            out = _rms_rows(out, fin_ref[...])
        o_ref[c, :] = out


def _ffn(x, mod, layer, gain, wg, wu, wd, fin, mod_row, final, row0=0, n_rows=None):
    m_rows = x.shape[0] if n_rows is None else n_rows
    tm = TM_DENSE
    b0 = row0 // tm
    full = pl.BlockSpec((tm, D_MODEL), lambda i: (i, 0))
    vec = pl.BlockSpec((1, D_MODEL), lambda i: (0, 0))
    once = pl.Buffered(1)
    return pl.pallas_call(
        functools.partial(_ffn_kernel, final=final),
        grid=(m_rows // tm,),
        in_specs=[pl.BlockSpec((tm, D_MODEL), lambda i: (b0 + i, 0)),
                  pl.BlockSpec((1, 1, N_MOD, D_MODEL), lambda i: (layer, mod_row(b0 + i, tm), 0, 0)),
                  vec,
                  pl.BlockSpec((1, D_MODEL, D_FF), lambda i: (layer, 0, 0), pipeline_mode=once),
                  pl.BlockSpec((1, D_MODEL, D_FF), lambda i: (layer, 0, 0), pipeline_mode=once),
                  pl.BlockSpec((1, D_FF, D_MODEL), lambda i: (layer, 0, 0), pipeline_mode=once),
                  vec],
        out_specs=full,
        out_shape=jax.ShapeDtypeStruct((m_rows, D_MODEL), F32),
        compiler_params=_cparams(("parallel",)),
        name="ffn",
    )(x, mod, gain, wg, wu, wd, fin)


def _layout_w_in(w_in):
    def cols(a, b):
        return w_in[:, :, a:b]

    pad = jnp.zeros(w_in.shape[:2] + (64,), w_in.dtype)
    parts = [cols(1440, 2976), cols(768, 1440), cols(3488, 3520), pad, cols(512, 768), cols(0, 512)]
    w_tok = jnp.concatenate(parts, axis=-1).astype(BF16)
    w_gz = jnp.concatenate([cols(3520, 6592), cols(2976, 3488)], axis=-1).astype(BF16)
    return w_tok, w_gz


def _layout_w_uq(w_uq):
    n_layers = w_uq.shape[0]
    w = w_uq.reshape(n_layers, Q_LORA, HEADS, NOPE_B + ROPE_B)
    w = jnp.pad(w, ((0, 0), (0, 0), (0, 0), (0, LANES - NOPE_B - ROPE_B)))
    return w.reshape(n_layers, Q_LORA, HEADS * LANES).astype(BF16)


def _layout_w_ukv(w_ukv):
    n_layers = w_ukv.shape[0]
    w = w_ukv.reshape(n_layers, KV_LORA, HEADS, 2, HD)
    zero = jnp.zeros((n_layers, KV_LORA, HEADS, HD), w.dtype)
    wk_top = jnp.concatenate([w[:, :, :, 0], zero], axis=-1).reshape(n_layers, KV_LORA, HEADS * LANES)
    place = np.zeros((LANES, HEADS, LANES), np.float32)
    for d in range(ROPE_B):
        place[d, :, NOPE_B + d] = 1.0
    place = jnp.broadcast_to(jnp.asarray(place.reshape(LANES, HEADS * LANES)), (n_layers, LANES, HEADS * LANES))
    wk = jnp.concatenate([wk_top, place], axis=1).astype(BF16)
    wv = w[:, :, :, 1].reshape(n_layers, KV_LORA, HEADS * HD).astype(BF16)
    return wk, wv


def kernel(x_prompt, x_sample, cache_ka, cache_va, cache_ckv, cache_kpe, state_fwd, state_bwd, c, c_ctx,
           w_mod, b_mod, norm1, norm2, w_in, a_qnorm, a_knorm, b_qnorm, b_kvnorm, w_uq, w_ukv, c_conv,
           c_alog, c_dt_bias, c_onorm, w_pa, w_pb, w_pc, w_out, w_gate, w_up, w_down, final_norm):
    n_ctx, s_ctx, _ = x_prompt.shape
    n_lat, s_lat, _ = x_sample.shape
    n_layers = w_mod.shape[0]
    past = cache_ka.shape[2]
    ctx_rows = n_ctx * s_ctx
    lat_rows = n_lat * s_lat
    assert s_ctx == TOK and s_lat % TOK == 0 and ctx_rows % 512 == 0 and s_lat % 512 == 0
    assert past % TOK == 0 and n_lat + 1 <= 8 and ctx_rows % s_lat == 0

    def mod_row(i, tm):
        return jnp.where(i < ctx_rows // tm, 0, 1 + (i - ctx_rows // tm) // (s_lat // tm))

    def tab_row(i):
        nb = ctx_rows // TOK_AB
        return jnp.where(i < nb, 0, 1 + (i - nb) % (s_lat // TOK_AB))

    x = (x_prompt.reshape(ctx_rows, D_MODEL), x_sample.reshape(lat_rows, D_MODEL))
    cvec = jnp.zeros((8, D_MODEL), F32).at[0].set(c_ctx).at[1:1 + n_lat].set(c)
    mod = _modulation(cvec, w_mod, b_mod).reshape(n_layers, 8, N_MOD, D_MODEL)
    cos, sin = _rope_tables(s_lat)

    w_in_p, w_gz = _layout_w_in(w_in)
    w_uq_p = _layout_w_uq(w_uq)
    w_k_p, w_v_p = _layout_w_ukv(w_ukv)
    bf = lambda w: w.astype(BF16)
    w_pa_b, w_pb_b, w_pc_b, w_out_b = bf(w_pa), bf(w_pb), bf(w_pc), bf(w_out)
    w_gate_b, w_up_b, w_down_b = bf(w_gate), bf(w_up), bf(w_down)

    cache_rows = n_lat * n_layers * past
    ck4, cv4 = _cache_a(cache_ka.reshape(cache_rows, LANES), cache_va.reshape(cache_rows, LANES))
    kpe_pad = jnp.pad(cache_kpe.reshape(cache_rows, ROPE_B), ((0, 0), (0, LANES - ROPE_B)))
    ckb, cvb = _cache_b(cache_ckv.reshape(cache_rows, KV_LORA), kpe_pad, w_k_p, w_v_p,
                        lambda i: (i // (past // TOK)) % n_layers)

    lane_vec = lambda v16: jnp.zeros((1, LANES), F32).at[0, 32:48].set(v16.reshape(16))
    new = []
    for l in range(n_layers):
        y, va_raw, qa, k4, v4, kn_a, qb, kb, vb, ckvn = _token_stage(
            x, mod, l, norm1[l][None], w_in_p, cos, sin,
            jnp.tile(a_qnorm[l], HEADS)[None], jnp.tile(a_knorm[l], KV_HEADS_A)[None],
            b_qnorm[l][None], b_kvnorm[l][None], w_uq_p, w_k_p, w_v_p, mod_row, tab_row, ctx_rows)
        conv_args = (c_conv[l, :, 0, :], lane_vec(c_alog[l]), lane_vec(c_dt_bias[l]))
        gdn_ctx = _prep_c(y, *conv_args, 0, ctx_rows, s_ctx)
        gdn_lat = _prep_c(y, *conv_args, ctx_rows, lat_rows, s_lat)

        cache_blk = (past, lambda b, l=l: b * n_layers + l)
        ya_c = _attention(qa, k4, v4, None, GROUPS_A, 0, n_ctx, s_ctx, s_ctx, None)
        ya_l = _attention(qa, k4, v4, (ck4, cv4), GROUPS_A, ctx_rows, n_lat, s_lat, 256, cache_blk)
        yb_c = _attention(qb, kb, vb, None, GROUPS_B, 0, n_ctx, s_ctx, s_ctx, None)
        yb_l = _attention(qb, kb, vb, (ckb, cvb), GROUPS_B, ctx_rows, n_lat, s_lat, 256, cache_blk)

        of_c, ob_c, sf_c, sb_c = _gdn_scan(*gdn_ctx, None, n_ctx, s_ctx)
        s0 = (_state_to_pairs(state_fwd[:, l]), _state_to_pairs(state_bwd[:, l]))
        of_l, ob_l, _, _ = _gdn_scan(*gdn_lat, s0, n_lat, s_lat)

        x_mid = _merge(x, mod, l, (ya_c, ya_l, yb_c, yb_l, of_c, of_l, ob_c, ob_l), norm1[l][None], w_gz,
                       jnp.tile(c_onorm[l], HEADS)[None], w_pa_b, w_pb_b, w_pc_b, w_out_b, mod_row, ctx_rows)
        ffn_args = (mod, l, norm2[l][None], w_gate_b, w_up_b, w_down_b, final_norm[None], mod_row)
        if l < n_layers - 1:
            x = (_ffn(x_mid, *ffn_args, final=False),)
        else:
            y_prompt = _ffn(x_mid, *ffn_args, final=True, row0=0, n_rows=ctx_rows)
            y_sample = _ffn(x_mid, *ffn_args, final=True, row0=ctx_rows, n_rows=lat_rows)

        new.append((kn_a[:ctx_rows].reshape(n_ctx, s_ctx, KV_HEADS_A, HD),
                    va_raw[:ctx_rows].reshape(n_ctx, s_ctx, KV_HEADS_A, HD),
                    ckvn[:ctx_rows].reshape(n_ctx, s_ctx, KV_LORA),
                    y[:ctx_rows, CONV_CH:CONV_CH + ROPE_B].reshape(n_ctx, s_ctx, ROPE_B),
                    _pairs_to_state(sf_c), _pairs_to_state(sb_c)))

    stacked = [jnp.stack([n[i] for n in new], axis=1) for i in range(6)]
    return (y_prompt.reshape(n_ctx, s_ctx, D_MODEL), y_sample.reshape(n_lat, s_lat, D_MODEL)) + tuple(stacked)
```

```python
import functools

import numpy as np
import jax
import jax.numpy as jnp
from jax import lax
from jax.experimental import pallas as pl
from jax.experimental.pallas import tpu as pltpu

F32 = jnp.float32
BF16 = jnp.bfloat16

D_MODEL = 1024
GRID_W = 64
GRID_W_LOG2 = 6
assert 1 << GRID_W_LOG2 == GRID_W
ROPE_THETA = 10000.0
EPS = 1e-6
N_MOD = 6
HEADS = 8
KV_HEADS_A = 2
HD = 64
Q_LORA = 384
KV_LORA = 256
NOPE_B = 64
ROPE_B = 32
CHUNK = 64
CONV_CH = 3 * HEADS * HD
D_FF = 2816
LANES = 128

N_AB = 2 * HEADS
O_QA = 0
O_KA = O_QA + HEADS * HD
O_VA = O_KA + KV_HEADS_A * HD
O_CQ = O_VA + KV_HEADS_A * HD
O_CKV = O_CQ + Q_LORA
O_KPE = O_CKV + KV_LORA
O_QKV = O_KPE + ROPE_B
O_Z = O_QKV + CONV_CH
O_A = O_Z + HEADS * HD
O_B = O_A + N_AB
O_GATES = O_B + N_AB
W_IN_COLS = O_GATES + 3 * D_MODEL

C_QKV = 0
C_B = C_QKV + CONV_CH
B_LOGITS = Q_LORA + KV_LORA
LOGIT_A = ROPE_B
LOGIT_B = LOGIT_A + N_AB
LOGIT_END = LOGIT_B + N_AB
C_KA = C_B + B_LOGITS + LANES
C_VA = C_KA + KV_HEADS_A * HD
C_QA = C_VA + KV_HEADS_A * HD
N_IN = C_QA + HEADS * HD
N_GZ = 3 * D_MODEL + HEADS * HD

LOG2E = 1.4426950408889634
TOK = 256
TOK_AB = 512
TM_DENSE = 512
ROW_CHUNK = 256
VMEM_LIMIT = 56 * 1024 * 1024


def _cparams(sem):
    return pltpu.CompilerParams(dimension_semantics=sem, vmem_limit_bytes=VMEM_LIMIT)


def _dot(a, b):
    return jnp.dot(a, b, preferred_element_type=F32)


def _dot_nt(a, b):
    return lax.dot_general(a, b, (((1,), (1,)), ((), ())), preferred_element_type=F32)


def _dot_tn(a, b):
    return lax.dot_general(a, b, (((0,), (0,)), ((), ())), preferred_element_type=F32)


def _split3(x):
    hi = x.astype(BF16)
    r = x - hi.astype(F32)
    mid = r.astype(BF16)
    lo = (r - mid.astype(F32)).astype(BF16)
    return hi, mid, lo


def _dot3(x, w):
    hi, mid, lo = _split3(x)
    return _dot(hi, w) + _dot(mid, w) + _dot(lo, w)


def _group_sums(x, ones):
    w = min(x.shape[1], ones.shape[0])
    blk = ones[:w, :w]
    slabs = []
    for s in range(x.shape[1] // w):
        xs = x[:, s * w:(s + 1) * w]
        hi = xs.astype(BF16)
        lo = (xs - hi.astype(F32)).astype(BF16)
        slabs.append(_dot(hi, blk) + _dot(lo, blk))
    return slabs[0] if len(slabs) == 1 else jnp.concatenate(slabs, axis=1)


def _silu(x):
    return x * jax.nn.sigmoid(x)


def _rms_rows(x, g):
    return x * lax.rsqrt(jnp.mean(x * x, axis=-1, keepdims=True) + EPS) * g


def _swap_pairs(x, half):
    lane = lax.broadcasted_iota(jnp.int32, x.shape, 1)
    n = x.shape[1]
    return jnp.where((lane & half) == 0, pltpu.roll(x, n - half, 1), pltpu.roll(x, half, 1))


def _rope_cols(x, cos, sin, half):
    outs = []
    for s in range(x.shape[1] // LANES):
        xs = x[:, s * LANES:(s + 1) * LANES]
        outs.append(xs * cos + _swap_pairs(xs, half) * sin)
    return outs[0] if len(outs) == 1 else jnp.concatenate(outs, axis=1)


def _mod_kernel(c_ref, w_ref, b_ref, o_ref):
    s = _silu(c_ref[...]).astype(BF16)
    o_ref[0] = _dot(s, w_ref[0].astype(BF16)) + b_ref[0]


def _modulation(cvec, w_mod, b_mod):
    n_layers = w_mod.shape[0]
    n_out = w_mod.shape[2]
    tn = 1536
    return pl.pallas_call(
        _mod_kernel,
        grid=(n_layers, n_out // tn),
        in_specs=[pl.BlockSpec((8, D_MODEL), lambda l, j: (0, 0)),
                  pl.BlockSpec((1, D_MODEL, tn), lambda l, j: (l, 0, j)),
                  pl.BlockSpec((1, 1, tn), lambda l, j: (l, 0, j))],
        out_specs=pl.BlockSpec((1, 8, tn), lambda l, j: (l, 0, j)),
        out_shape=jax.ShapeDtypeStruct((n_layers, 8, n_out), F32),
        compiler_params=_cparams(("parallel", "parallel")),
        name="modulation",
    )(cvec, w_mod, b_mod.reshape(n_layers, 1, n_out))


def _x_rows(x_refs, ctx_blocks, rows):
    if len(x_refs) == 1:
        return x_refs[0][rows, :]
    return jnp.where(pl.program_id(0) < ctx_blocks, x_refs[0][rows, :], x_refs[1][rows, :])


def _x_specs(x_parts, tm, ctx_rows):
    if len(x_parts) == 1:
        return [pl.BlockSpec((tm, D_MODEL), lambda i: (i, 0))]
    nb_ctx, nb_lat = ctx_rows // tm, x_parts[1].shape[0] // tm
    return [pl.BlockSpec((tm, D_MODEL), lambda i: (jnp.minimum(i, nb_ctx - 1), 0)),
            pl.BlockSpec((tm, D_MODEL), lambda i: (jnp.clip(i - nb_ctx, 0, nb_lat - 1), 0))]


def _token_kernel(*refs, n_x, ctx_blocks):
    x_refs = refs[:n_x]
    (mod_ref, g_ref, w_ref, cos_a, sin_a, cos_q, sin_q, cos_k, sin_k, gq_a, gk_a, ones_ref, place_ref,
     gq_b, gkv_b, wq_ref, wk_ref, wv_ref,
     yc_ref, va_ref, qa_ref, k4_ref, vta_ref, kna_ref, qb_ref, kb_ref, vtb_ref, ckv_ref, y_scr) = refs[n_x:]
    m = mod_ref[0, 0]
    for r in range(0, y_scr.shape[0], ROW_CHUNK):
        rows = slice(r, r + ROW_CHUNK)
        y = _rms_rows(_x_rows(x_refs, ctx_blocks, rows), g_ref[...])
        h = (y * (1.0 + m[1:2]) + m[0:1]).astype(BF16)
        y_scr[rows, :] = _dot(h, w_ref[0])
    yc_ref[:, :CONV_CH] = y_scr[:, C_QKV:C_QKV + CONV_CH]
    yc_ref[:, CONV_CH:] = y_scr[:, C_B + B_LOGITS:C_B + B_LOGITS + LANES]
    va_ref[...] = y_scr[:, C_VA:C_VA + LANES]
    _prep_a_kernel(y_scr.at[:, C_QA:C_QA + HEADS * HD], y_scr.at[:, C_KA:C_KA + 2 * LANES],
                   cos_a, sin_a, gq_a, gk_a, ones_ref, place_ref, qa_ref, k4_ref, vta_ref, kna_ref)
    _prep_b_kernel(y_scr.at[:, C_B:C_B + B_LOGITS + LANES], cos_q, sin_q, cos_k, sin_k, gq_b, gkv_b,
                   wq_ref, wk_ref, wv_ref, qb_ref, kb_ref, vtb_ref, ckv_ref)


def _token_stage(x_parts, mod, layer, norm_gain, w, cos, sin, gq_a, gk_a, gq_b, gkv_b, wq, wk, wv,
                 mod_row, tab_row, ctx_rows):
    m_rows = sum(p.shape[0] for p in x_parts)
    tm = TOK_AB
    rows = lambda w_: pl.BlockSpec((tm, w_), lambda i: (i, 0))
    cols = lambda h_: pl.BlockSpec((h_, tm), lambda i: (0, i))
    vec = lambda w_: pl.BlockSpec((1, w_), lambda i: (0, 0))
    tab = lambda a: pl.BlockSpec((1, tm, LANES), lambda i: (a, tab_row(i), 0))
    lw = lambda k, n: pl.BlockSpec((1, k, n), lambda i: (layer, 0, 0))
    f32 = lambda *shape: jax.ShapeDtypeStruct(shape, F32)
    bf16 = lambda *shape: jax.ShapeDtypeStruct(shape, BF16)
    return pl.pallas_call(
        functools.partial(_token_kernel, n_x=len(x_parts), ctx_blocks=ctx_rows // tm),
        grid=(m_rows // tm,),
        in_specs=_x_specs(x_parts, tm, ctx_rows)
        + [pl.BlockSpec((1, 1, N_MOD, D_MODEL), lambda i: (layer, mod_row(i, tm), 0, 0)),
           vec(D_MODEL), lw(D_MODEL, N_IN),
           tab(0), tab(0), tab(1), tab(1), tab(2), tab(2),
           vec(HEADS * HD), vec(LANES),
           pl.BlockSpec((256, 256), lambda i: (0, 0)), pl.BlockSpec((LANES, 4 * LANES), lambda i: (0, 0)),
           vec(Q_LORA), vec(KV_LORA),
           lw(Q_LORA, HEADS * LANES), lw(KV_LORA + LANES, HEADS * LANES), lw(KV_LORA, HEADS * HD)],
        out_specs=[rows(CONV_CH + LANES), rows(LANES),
                   rows(4 * LANES), rows(4 * LANES), cols(LANES), rows(LANES),
                   rows(HEADS * LANES), rows(HEADS * LANES), cols(HEADS * HD), rows(KV_LORA)],
        out_shape=[f32(m_rows, CONV_CH + LANES), f32(m_rows, LANES),
                   bf16(m_rows, 4 * LANES), bf16(m_rows, 4 * LANES), bf16(LANES, m_rows), f32(m_rows, LANES),
                   bf16(m_rows, HEADS * LANES), bf16(m_rows, HEADS * LANES), bf16(HEADS * HD, m_rows),
                   f32(m_rows, KV_LORA)],
        scratch_shapes=[pltpu.VMEM((tm, N_IN), F32)],
        compiler_params=_cparams(("parallel",)),
        name="token_stage",
    )(*x_parts, mod, norm_gain, w, cos, sin, cos, sin, cos, sin, gq_a, gk_a, _ones_blockdiag(256, HD),
      _place4(), gq_b, gkv_b, wq, wk, wv)


def _rope_table_kernel(freq_ref, col_ref, sign_ref, cos_ref, sin_ref):
    shape = cos_ref.shape
    pos = pl.program_id(1) * TOK_AB + lax.broadcasted_iota(jnp.int32, shape, 1) - TOK_AB
    grid_row = (pos >> GRID_W_LOG2).astype(F32)
    grid_col = (pos & (GRID_W - 1)).astype(F32)
    a = jnp.where(col_ref[...] > 0.5, grid_col, grid_row) * freq_ref[...]
    a = jnp.where(pos >= 0, a, 0.0)
    cos_ref[...] = jnp.cos(a)
    sin_ref[...] = jnp.sin(a) * sign_ref[...]


def _rope_tables(n_lat):
    lane = np.arange(LANES)

    def lane_pattern(nf, d, active):
        inv_freq = np.float32(ROPE_THETA) ** (-np.arange(nf, dtype=np.float32) / np.float32(nf))
        dd = np.where(active, d, 0)
        freq = np.where(active, inv_freq[dd % nf], 0.0).astype(np.float32)
        use_col = (dd >= 2 * nf).astype(np.float32)
        sign = np.where((dd % (2 * nf)) < nf, -1.0, 1.0).astype(np.float32)
        return freq, use_col, sign

    d_q = lane - NOPE_B
    patterns = [lane_pattern(HD // 4, lane % HD, np.ones(LANES, bool)),
                lane_pattern(ROPE_B // 4, d_q, (d_q >= 0) & (d_q < ROPE_B)),
                lane_pattern(ROPE_B // 4, lane, lane < ROPE_B)]
    freq, use_col, sign = [jnp.asarray(np.stack([p[t] for p in patterns]))[:, None, :] for t in range(3)]
    rows = TOK_AB + n_lat
    spec = pl.BlockSpec((1, TOK_AB, LANES), lambda a, i: (a, i, 0))
    vec = pl.BlockSpec((1, 1, LANES), lambda a, i: (a, 0, 0))
    return pl.pallas_call(
        _rope_table_kernel,
        grid=(3, rows // TOK_AB),
        in_specs=[vec, vec, vec],
        out_specs=[spec, spec],
        out_shape=[jax.ShapeDtypeStruct((3, rows, LANES), F32)] * 2,
        compiler_params=_cparams(("parallel", "parallel")),
        name="rope_tables",
    )(freq, use_col, sign)


def _prep_a_kernel(qa_ref, kv_ref, cos_ref, sin_ref, gq_ref, gk_ref, ones_ref, place_ref,
                   q_ref, k4_ref, vt_ref, kn_ref):
    cos = cos_ref[0]
    sin = sin_ref[0]
    ones = ones_ref[...]
    q = qa_ref[...]
    qn = q * lax.rsqrt(_group_sums(q * q, ones) * (1.0 / HD) + EPS) * gq_ref[...]
    q_ref[...] = (_rope_cols(qn, cos, sin, HD // 4) * (HD ** -0.5 * LOG2E)).astype(BF16)
    k = kv_ref[:, :LANES]
    kn = k * lax.rsqrt(_group_sums(k * k, ones) * (1.0 / HD) + EPS) * gk_ref[...]
    kn_ref[...] = kn
    k4_ref[...] = _dot(_rope_cols(kn, cos, sin, HD // 4).astype(BF16), place_ref[...]).astype(BF16)
    vt_ref[...] = kv_ref[:, LANES:].T.astype(BF16)


def _place4():
    p = np.zeros((LANES, 4 * LANES), np.float32)
    for kv in range(KV_HEADS_A):
        for side in range(2):
            for d in range(HD):
                p[kv * HD + d, (2 * kv + side) * LANES + side * HD + d] = 1.0
    return jnp.asarray(p, BF16)


def _ones_blockdiag(n, group):
    idx = np.arange(n) // group
    return jnp.asarray((idx[:, None] == idx[None, :]).astype(np.float32), BF16)


def _cache_a_kernel(k_ref, v_ref, place_ref, k4_ref, vt_ref):
    k4_ref[...] = _dot(k_ref[...].astype(BF16), place_ref[...]).astype(BF16)
    vt_ref[...] = v_ref[...].T.astype(BF16)


def _cache_a(ck, cv):
    rows = ck.shape[0]
    narrow = pl.BlockSpec((TOK, LANES), lambda i: (i, 0))
    return pl.pallas_call(
        _cache_a_kernel,
        grid=(rows // TOK,),
        in_specs=[narrow, narrow, pl.BlockSpec((LANES, 512), lambda i: (0, 0))],
        out_specs=[pl.BlockSpec((TOK, 512), lambda i: (i, 0)), pl.BlockSpec((LANES, TOK), lambda i: (0, i))],
        out_shape=[jax.ShapeDtypeStruct((rows, 512), BF16), jax.ShapeDtypeStruct((LANES, rows), BF16)],
        compiler_params=_cparams(("parallel",)),
        name="cache_a",
    )(ck, cv, _place4())


def _prep_b_kernel(g_ref, cq_ref, sq_ref, ck_ref, sk_ref, gq_ref, gkv_ref, wq_ref, wk_ref, wv_ref,
                   q_ref, k_ref, vt_ref, ckv_ref):
    g = g_ref[...]
    cqn = _rms_rows(g[:, :Q_LORA], gq_ref[...])
    q = _dot(cqn.astype(BF16), wq_ref[0])
    q = _rope_cols(q, cq_ref[0], sq_ref[0], ROPE_B // 4) * ((NOPE_B + ROPE_B) ** -0.5 * LOG2E)
    q_ref[...] = q.astype(BF16)
    ckvn = _rms_rows(g[:, Q_LORA:Q_LORA + KV_LORA], gkv_ref[...])
    ckv_ref[...] = ckvn
    kpe = _rope_cols(g[:, B_LOGITS:], ck_ref[0], sk_ref[0], ROPE_B // 4)
    ckv_b = ckvn.astype(BF16)
    kin = jnp.concatenate([ckv_b, kpe.astype(BF16)], axis=1)
    k_ref[...] = _dot(kin, wk_ref[0]).astype(BF16)
    vt_ref[...] = _dot(ckv_b, wv_ref[0]).T.astype(BF16)


def _cache_b_kernel(c_ref, p_ref, wk_ref, wv_ref, k_ref, vt_ref):
    ckv_b = c_ref[...].astype(BF16)
    kin = jnp.concatenate([ckv_b, p_ref[...].astype(BF16)], axis=1)
    k_ref[...] = _dot(kin, wk_ref[0]).astype(BF16)
    vt_ref[...] = _dot(ckv_b, wv_ref[0]).T.astype(BF16)


def _cache_b(ckv, kpe, wk, wv, layer_of_block):
    rows = ckv.shape[0]
    return pl.pallas_call(
        _cache_b_kernel,
        grid=(rows // TOK,),
        in_specs=[pl.BlockSpec((TOK, KV_LORA), lambda i: (i, 0)),
                  pl.BlockSpec((TOK, LANES), lambda i: (i, 0)),
                  pl.BlockSpec((1, KV_LORA + LANES, HEADS * LANES), lambda i: (layer_of_block(i), 0, 0)),
                  pl.BlockSpec((1, KV_LORA, HEADS * HD), lambda i: (layer_of_block(i), 0, 0))],
        out_specs=[pl.BlockSpec((TOK, HEADS * LANES), lambda i: (i, 0)),
                   pl.BlockSpec((HEADS * HD, TOK), lambda i: (0, i))],
        out_shape=[jax.ShapeDtypeStruct((rows, HEADS * LANES), BF16),
                   jax.ShapeDtypeStruct((HEADS * HD, rows), BF16)],
        compiler_params=_cparams(("parallel",)),
        name="cache_b",
    )(ckv, kpe, wk, wv)


ATT_KEY_TILE = 256
ATT_LOOKAHEAD = 5


def _attn_kernel(*refs, groups, has_cache):
    if has_cache:
        q_ref, kc_ref, vc_ref, kn_ref, vn_ref, o_ref = refs
        segments = [(kc_ref, vc_ref), (kn_ref, vn_ref)]
    else:
        q_ref, kn_ref, vn_ref, o_ref = refs
        segments = [(kn_ref, vn_ref)]
    tiles = [(k_ref, v_ref, t * ATT_KEY_TILE) for k_ref, v_ref in segments
             for t in range(k_ref.shape[0] // ATT_KEY_TILE)]
    q = [q_ref[:, qg * LANES:(qg + 1) * LANES] for qg, _, _ in groups]

    def scores(h, i):
        k_ref, _, t0 = tiles[i]
        kg = groups[h][1]
        return _dot_nt(k_ref[t0:t0 + ATT_KEY_TILE, kg * LANES:(kg + 1) * LANES], q[h])

    heads = range(len(groups))
    jobs = [(i, h) for i in range(len(tiles)) for h in heads]
    pending = {j: scores(h, i) for j, (i, h) in enumerate(jobs[:ATT_LOOKAHEAD])}
    ones_rows = (lax.broadcasted_iota(jnp.int32, (16, ATT_KEY_TILE), 0) == 0).astype(BF16)
    m, acc = {}, {}
    for j, (i, h) in enumerate(jobs):
        if j + ATT_LOOKAHEAD < len(jobs):
            i_n, h_n = jobs[j + ATT_LOOKAHEAD]
            pending[j + ATT_LOOKAHEAD] = scores(h_n, i_n)
        s = pending.pop(j)
        _, v_ref, t0 = tiles[i]
        vrow = groups[h][2]
        v_t = jnp.concatenate([v_ref[vrow:vrow + HD, t0:t0 + ATT_KEY_TILE], ones_rows], axis=0)
        m_tile = jnp.max(s, axis=0, keepdims=True)
        if i == 0:
            m[h] = m_tile
            acc[h] = _dot(v_t, jnp.exp2(s - m_tile).astype(BF16))
        else:
            m_new = jnp.maximum(m[h], m_tile)
            alpha = jnp.exp2(m[h] - m_new)
            acc[h] = acc[h] * alpha + _dot(v_t, jnp.exp2(s - m_new).astype(BF16))
            m[h] = m_new
    outs = [acc[h][:HD] / acc[h][HD:HD + 1] for h in heads]
    o_ref[...] = jnp.concatenate(outs, axis=0).T.astype(BF16)


def _attention(q, kn, vn, cache, groups, row0, n_seq, seq, tq, cache_block):
    qw, kw, vr = q.shape[1], kn.shape[1], vn.shape[0]
    nq = seq // tq
    qb0, sb0 = row0 // tq, row0 // seq
    in_specs = [pl.BlockSpec((tq, qw), lambda b, i: (qb0 + b * nq + i, 0))]
    args = [q]
    if cache is not None:
        past = cache_block[0]
        in_specs += [pl.BlockSpec((past, kw), lambda b, i: (cache_block[1](b), 0)),
                     pl.BlockSpec((vr, past), lambda b, i: (0, cache_block[1](b)))]
        args += list(cache)
    in_specs += [pl.BlockSpec((seq, kw), lambda b, i: (sb0 + b, 0)),
                 pl.BlockSpec((vr, seq), lambda b, i: (0, sb0 + b))]
    args += [kn, vn]
    return pl.pallas_call(
        functools.partial(_attn_kernel, groups=groups, has_cache=cache is not None),
        grid=(n_seq, nq),
        in_specs=in_specs,
        out_specs=pl.BlockSpec((tq, 512), lambda b, i: (b * nq + i, 0)),
        out_shape=jax.ShapeDtypeStruct((n_seq * seq, 512), BF16),
        compiler_params=_cparams(("parallel", "parallel")),
        name="attention",
    )(*args)


GROUPS_A = tuple((h // 2, 2 * (h // 4) + (h % 2), (h // 4) * HD) for h in range(HEADS))
GROUPS_B = tuple((h, h, h * HD) for h in range(HEADS))


def _prep_c_kernel(x_ref, prev_ref, next_ref, ab_ref, w_ref, alog_ref, dtb_ref, ones_ref, exp_ref, tri_ref,
                   q_ref, k_ref, v_ref, gf_ref, gb_ref, bf_ref, bb_ref, *, blocks_per_seq):
    j = pl.program_id(0) % blocks_per_seq
    is_first, is_last = j == 0, j == blocks_per_seq - 1
    x = x_ref[...]
    rows = x.shape[0]
    rid = lax.broadcasted_iota(jnp.int32, x.shape, 0)
    halo_p = jnp.where(is_first, 0.0, prev_ref[7:8, :])
    halo_n = jnp.where(is_last, 0.0, next_ref[0:1, :])
    x_prev = jnp.where(rid == 0, halo_p, pltpu.roll(x, 1, 0))
    x_next = jnp.where(rid == rows - 1, halo_n, pltpu.roll(x, rows - 1, 0))
    w = w_ref[...]
    s = _silu(x_prev * w[0:1] + x * w[1:2] + x_next * w[2:3])
    ones = ones_ref[...]
    hw = HEADS * HD
    q, k = s[:, :hw], s[:, hw:2 * hw]
    q_ref[...] = q * lax.rsqrt(_group_sums(q * q, ones) + EPS) * (HD ** -0.5)
    k_ref[...] = k * lax.rsqrt(_group_sums(k * k, ones) + EPS)
    v_ref[...] = s[:, 2 * hw:]
    ab = ab_ref[...]
    t = ab + dtb_ref[...]
    softplus = jnp.maximum(t, 0.0) + jnp.log1p(jnp.exp(-jnp.abs(t)))
    g = -jnp.exp(alog_ref[...]) * softplus
    lane = lax.broadcasted_iota(jnp.int32, ab.shape, 1)
    parts = _split3(g)
    cum_f = sum(_dot(tri_ref[0], part) for part in parts)
    cum_b = sum(_dot(tri_ref[1], part) for part in parts)
    both = jnp.where(lane < LOGIT_A + HEADS, cum_f, jnp.where(lane < LOGIT_B, cum_b, jax.nn.sigmoid(ab)))
    e = _dot3(both, exp_ref[...])
    gf_ref[...] = e[:, 0 * hw:1 * hw]
    gb_ref[...] = e[:, 1 * hw:2 * hw]
    bf_ref[...] = e[:, 2 * hw:3 * hw]
    bb_ref[...] = e[:, 3 * hw:4 * hw]


def _expand_matrix():
    e = np.zeros((LANES, 4 * HEADS * HD), np.float32)
    for kind in range(2):
        for d in range(2):
            for h in range(HEADS):
                src = LOGIT_A + N_AB * kind + HEADS * d + h
                dst = (2 * kind + d) * HEADS * HD + h * HD
                e[src, dst:dst + HD] = 1.0
    return jnp.asarray(e, BF16)


def _chunk_tri():
    r = np.arange(TOK)
    same = (r[:, None] // CHUNK) == (r[None, :] // CHUNK)
    low = same & (r[:, None] >= r[None, :])
    up = same & (r[:, None] <= r[None, :])
    return jnp.asarray(np.stack([low, up]).astype(np.float32), BF16)


def _prep_c(y, w_conv, alog, dtb, row0, n_rows, seq):
    nb8 = TOK // 8
    b0 = row0 // TOK
    last8 = (row0 + n_rows) // 8 - 1
    qkv_blk = 0
    out = pl.BlockSpec((TOK, HEADS * HD), lambda i: (i, 0))
    vec = pl.BlockSpec((1, LANES), lambda i: (0, 0))
    return pl.pallas_call(
        functools.partial(_prep_c_kernel, blocks_per_seq=seq // TOK),
        grid=(n_rows // TOK,),
        in_specs=[pl.BlockSpec((TOK, CONV_CH), lambda i: (b0 + i, qkv_blk)),
                  pl.BlockSpec((8, CONV_CH), lambda i: (jnp.maximum((b0 + i) * nb8 - 1, 0), qkv_blk)),
                  pl.BlockSpec((8, CONV_CH), lambda i: (jnp.minimum((b0 + i + 1) * nb8, last8), qkv_blk)),
                  pl.BlockSpec((TOK, LANES), lambda i: (b0 + i, CONV_CH // LANES)),
                  pl.BlockSpec((3, CONV_CH), lambda i: (0, 0)),
                  vec, vec,
                  pl.BlockSpec((256, 256), lambda i: (0, 0)),
                  pl.BlockSpec((LANES, 4 * HEADS * HD), lambda i: (0, 0)),
                  pl.BlockSpec((2, TOK, TOK), lambda i: (0, 0, 0))],
        out_specs=[out] * 7,
        out_shape=[jax.ShapeDtypeStruct((n_rows, HEADS * HD), F32)] * 7,
        compiler_params=_cparams(("parallel",)),
        name="prep_c",
    )(y, y, y, y, w_conv, alog, dtb, _ones_blockdiag(256, HD), _expand_matrix(), _chunk_tri())


def _gdn_kernel(*refs, has_s0, n_chunks, n_par):
    ins = refs[:10]
    pos = 10
    if has_s0:
        s0_refs = refs[pos:pos + 2]
        pos += 2
    o_refs = refs[pos:pos + 2]
    s_out_refs = refs[pos + 2:pos + 4]
    s_scr = refs[pos + 4]
    c = pl.program_id(1)

    @pl.when(c == 0)
    def _():
        for d in range(2):
            s_scr[d] = s0_refs[d][...] if has_s0 else jnp.zeros(s_scr.shape[1:], F32)

    n2 = 2 * CHUNK
    ri = lax.broadcasted_iota(jnp.int32, (n2, n2), 0)
    ci = lax.broadcasted_iota(jnp.int32, (n2, n2), 1)
    same = (ri // CHUNK) == (ci // CHUNK)
    left = lax.broadcasted_iota(jnp.int32, (CHUNK, n2), 1) < CHUNK
    eye = (ri == ci).astype(F32)

    def level_mask(lvl, lower):
        bit = 1 << lvl
        hi, lo = (ri, ci) if lower else (ci, ri)
        return ((ri >> (lvl + 1)) == (ci >> (lvl + 1))) & ((hi & bit) != 0) & ((lo & bit) == 0)

    levels = [[level_mask(lvl, d == 0) for lvl in range(6)] for d in range(2)]

    def block_diag(x):
        return jnp.concatenate([jnp.where(left, x, 0.0), jnp.where(left, 0.0, x)], axis=0)

    def anti_diag(x):
        xr = pltpu.roll(x, CHUNK, 1)
        return jnp.concatenate([jnp.where(left, 0.0, xr), jnp.where(left, xr, 0.0)], axis=0)

    chains = [(d, n, p) for n in range(n_par) for d in range(2) for p in range(HEADS // 2)]
    incl = [same & ((ri >= ci) if d == 0 else (ri <= ci)) for d in range(2)]
    strict = [same & ((ri > ci) if d == 0 else (ri < ci)) for d in range(2)]

    def load(which):
        return [ins[5 * d + which][n, 0, :, p * n2:(p + 1) * n2] for d, n, p in chains]

    q, k, v, gc, beta = load(0), load(1), load(2), load(3), load(4)
    gtot = [gi[CHUNK - 1:CHUNK, :] if d == 0 else gi[0:1, :] for (d, _, _), gi in zip(chains, gc)]
    eg = [jnp.exp(gi) for gi in gc]
    kb = [ki * bi for ki, bi in zip(k, beta)]
    decay = []
    for (d, _, _), gi in zip(chains, gc):
        gcr = pltpu.roll(gi, CHUNK, 1)
        gcol = jnp.concatenate([jnp.where(left, gi, gcr), jnp.where(left, gcr, gi)], axis=0)
        diff = gcol - gcol.T
        decay.append(jnp.where(incl[d], jnp.exp(jnp.where(incl[d], diff, 0.0)), 0.0))
    k_bd = [block_diag(ki).astype(BF16) for ki in k]
    kk = [_dot_nt(block_diag(kbi).astype(BF16), kbd) for kbi, kbd in zip(kb, k_bd)]
    qk = [_dot_nt(block_diag(qi).astype(BF16), kbd) for qi, kbd in zip(q, k_bd)]
    a_mat = [jnp.where(strict[d], kki * di, 0.0) for (d, _, _), kki, di in zip(chains, kk, decay)]
    qk = [(qki * di).astype(BF16) for qki, di in zip(qk, decay)]
    x = []
    for vi, bi, kbi, egi in zip(v, beta, kb, eg):
        vb = vi * bi
        kbe_r = pltpu.roll(kbi * egi, CHUNK, 1)
        x.append(jnp.concatenate([jnp.where(left, vb, kbe_r), jnp.where(left, kbe_r, vb)], axis=0))
    inv = [eye - jnp.where(levels[d][0], ai, 0.0) for (d, _, _), ai in zip(chains, a_mat)]
    a_b = [ai.astype(BF16) for ai in a_mat]
    for lvl in range(1, 6):
        d_b = [di.astype(BF16) for di in inv]
        t = [_dot(ai, di).astype(BF16) for ai, di in zip(a_b, d_b)]
        inv = [di - jnp.where(levels[d][lvl], _dot(dbi, ti), 0.0)
               for (d, _, _), di, dbi, ti in zip(chains, inv, d_b, t)]
    x = [xi + _dot((di - eye).astype(BF16), xi.astype(BF16)) for xi, di in zip(x, inv)]
    s = [s_scr[d, n, p] for d, n, p in chains]
    s_b = [si.astype(BF16) for si in s]
    ws = [_dot(jnp.where(same, 0.0, xi).astype(BF16), sbi) for xi, sbi in zip(x, s_b)]
    v_new = [(jnp.where(same, xi, 0.0) - wi).astype(BF16) for xi, wi in zip(x, ws)]
    o_s = [_dot(anti_diag(qi * egi).astype(BF16), sbi) for qi, egi, sbi in zip(q, eg, s_b)]
    o_v = [_dot(qki, vni) for qki, vni in zip(qk, v_new)]
    kd = [anti_diag(ki * jnp.exp(gt - gi)).astype(BF16) for ki, gt, gi in zip(k, gtot, gc)]
    s_upd = [_dot_tn(kdi, vni) for kdi, vni in zip(kd, v_new)]
    for i, (d, n, p) in enumerate(chains):
        o = o_s[i] + o_v[i]
        o_refs[d][n, 0, :, p * n2:(p + 1) * n2] = o[:CHUNK] + o[CHUNK:]
        s_scr[d, n, p] = s[i] * jnp.exp(gtot[i]) + s_upd[i]

    @pl.when(c == n_chunks - 1)
    def _():
        for d in range(2):
            s_out_refs[d][...] = s_scr[d]


GDN_SEQS_PER_STEP = 2


def _gdn_scan(qn, kn, v, gcf, gcb, bf, bb, s0, n_seq, seq):
    nc = seq // CHUNK
    hw = HEADS * HD
    n2 = 2 * CHUNK
    n_par = max(p for p in range(1, GDN_SEQS_PER_STEP + 1) if n_seq % p == 0)
    view = lambda a: a.reshape(n_seq, nc, CHUNK, hw)
    fwd = pl.BlockSpec((n_par, 1, CHUNK, hw), lambda b, c: (b, c, 0, 0))
    bwd = pl.BlockSpec((n_par, 1, CHUNK, hw), lambda b, c: (b, nc - 1 - c, 0, 0))
    state = pl.BlockSpec((n_par, HEADS // 2, n2, n2), lambda b, c: (b, 0, 0, 0))
    in_specs = [fwd] * 5 + [bwd] * 5
    args = [view(a) for a in (qn, kn, v, gcf, bf, qn, kn, v, gcb, bb)]
    if s0 is not None:
        in_specs += [state, state]
        args += list(s0)
    st_shape = jax.ShapeDtypeStruct((n_seq, HEADS // 2, n2, n2), F32)
    o_shape = jax.ShapeDtypeStruct((n_seq, nc, CHUNK, hw), F32)
    o_f, o_b, s_f, s_b = pl.pallas_call(
        functools.partial(_gdn_kernel, has_s0=s0 is not None, n_chunks=nc, n_par=n_par),
        grid=(n_seq // n_par, nc),
        in_specs=in_specs,
        out_specs=[fwd, bwd, state, state],
        out_shape=[o_shape, o_shape, st_shape, st_shape],
        scratch_shapes=[pltpu.VMEM((2, n_par, HEADS // 2, n2, n2), F32)],
        compiler_params=_cparams(("parallel", "arbitrary")),
        name="gdn_scan",
    )(*args)
    return o_f.reshape(n_seq * seq, hw), o_b.reshape(n_seq * seq, hw), s_f, s_b


def _state_to_pairs(s):
    b = s.shape[0]
    s = s.reshape(b, HEADS // 2, 2, HD, HD)
    z = jnp.zeros_like(s[:, :, 0])
    top = jnp.concatenate([z, s[:, :, 1]], axis=-1)
    bot = jnp.concatenate([s[:, :, 0], z], axis=-1)
    return jnp.concatenate([top, bot], axis=-2)


def _pairs_to_state(s):
    b = s.shape[0]
    even = s[:, :, HD:, :HD]
    odd = s[:, :, :HD, HD:]
    return jnp.stack([even, odd], axis=2).reshape(b, HEADS, HD, HD)


def _merge_kernel(*refs, n_x, ctx_blocks):
    x_refs, mod_ref, refs = refs[:n_x], refs[n_x], refs[n_x + 1:]
    pairs, (n1_ref, wgz_ref, gain_ref, ones_ref, wa_ref, wb_ref, wc_ref, wo_ref, o_ref) = refs[:8], refs[8:]
    is_ctx = pl.program_id(0) < ctx_blocks
    ya, yb, o_f, o_b = [jnp.where(is_ctx, pairs[2 * t][...], pairs[2 * t + 1][...]) for t in range(4)]
    m = mod_ref[0, 0]
    chunks = [slice(r, r + ROW_CHUNK) for r in range(0, o_ref.shape[0], ROW_CHUNK)]
    pa = [_dot(ya[c], wa_ref[0]) for c in chunks]
    pb = [_dot(yb[c], wb_ref[0]) for c in chunks]
    x = [_x_rows(x_refs, ctx_blocks, c) for c in chunks]
    h = [(_rms_rows(xc, n1_ref[...]) * (1.0 + m[1:2]) + m[0:1]).astype(BF16) for xc in x]
    gz = [_dot(hc, wgz_ref[0]) for hc in h]
    yc = []
    for c, gzc in zip(chunks, gz):
        o = o_f[c] + o_b[c]
        ms = _group_sums(o * o, ones_ref[...]) * (1.0 / HD)
        yc.append((o * lax.rsqrt(ms + EPS) * gain_ref[...] * _silu(gzc[:, 3 * D_MODEL:])).astype(BF16))
    pc = [_dot(ycc, wc_ref[0]) for ycc in yc]
    merged = [(jax.nn.sigmoid(g[:, :D_MODEL]) * a + jax.nn.sigmoid(g[:, D_MODEL:2 * D_MODEL]) * b
               + jax.nn.sigmoid(g[:, 2 * D_MODEL:3 * D_MODEL]) * cc).astype(BF16)
              for g, a, b, cc in zip(gz, pa, pb, pc)]
    mix = [_dot(mc, wo_ref[0]) for mc in merged]
    for c, xc, mixc in zip(chunks, x, mix):
        o_ref[c, :] = xc + m[2:3] * mixc


def _merge(x_parts, mod, layer, mixers, norm_gain, w_gz, gain, wa, wb, wc, wo, mod_row, ctx_rows):
    m_rows = sum(p.shape[0] for p in x_parts)
    tm = TM_DENSE
    nb_ctx = ctx_rows // tm
    nb_lat = (m_rows - ctx_rows) // tm
    ctx_half = pl.BlockSpec((tm, 512), lambda i: (jnp.minimum(i, nb_ctx - 1), 0))
    lat_half = pl.BlockSpec((tm, 512), lambda i: (jnp.clip(i - nb_ctx, 0, nb_lat - 1), 0))
    full = pl.BlockSpec((tm, D_MODEL), lambda i: (i, 0))

    def wspec(k):
        return pl.BlockSpec((1, k, D_MODEL), lambda i: (layer, 0, 0))

    return pl.pallas_call(
        functools.partial(_merge_kernel, n_x=len(x_parts), ctx_blocks=nb_ctx),
        grid=(m_rows // tm,),
        in_specs=_x_specs(x_parts, tm, ctx_rows)
        + [pl.BlockSpec((1, 1, N_MOD, D_MODEL), lambda i: (layer, mod_row(i, tm), 0, 0))]
        + [ctx_half, lat_half] * 4
        + [pl.BlockSpec((1, D_MODEL), lambda i: (0, 0)),
           pl.BlockSpec((1, D_MODEL, N_GZ), lambda i: (layer, 0, 0)),
           pl.BlockSpec((1, 512), lambda i: (0, 0)),
           pl.BlockSpec((256, 256), lambda i: (0, 0)),
           wspec(512), wspec(512), wspec(512), wspec(D_MODEL)],
        out_specs=full,
        out_shape=jax.ShapeDtypeStruct((m_rows, D_MODEL), F32),
        compiler_params=_cparams(("parallel",)),
        name="merge_out",
    )(*x_parts, mod, *mixers, norm_gain, w_gz, gain, _ones_blockdiag(256, HD), wa, wb, wc, wo)


def _ffn_kernel(x_ref, mod_ref, g_ref, wg_ref, wu_ref, wd_ref, fin_ref, o_ref, *, final):
    m = mod_ref[0, 0]
    chunks = [slice(r, r + ROW_CHUNK) for r in range(0, x_ref.shape[0], ROW_CHUNK)]
    x = [x_ref[c, :] for c in chunks]
    h = [(_rms_rows(xc, g_ref[...]) * (1.0 + m[4:5]) + m[3:4]).astype(BF16) for xc in x]
    gate = [_dot(hc, wg_ref[0]) for hc in h]
    up = [_dot(hc, wu_ref[0]) for hc in h]
    act = [(_silu(gc) * uc).astype(BF16) for gc, uc in zip(gate, up)]
    down = [_dot(ac, wd_ref[0]) for ac in act]
    for c, xc, dc in zip(chunks, x, down):
        out = xc + m[5:6] * dc
        if final:
            out = _rms_rows(out, fin_ref[...])
        o_ref[c, :] = out


def _ffn(x, mod, layer, gain, wg, wu, wd, fin, mod_row, final, row0=0, n_rows=None):
    m_rows = x.shape[0] if n_rows is None else n_rows
    tm = TM_DENSE
    b0 = row0 // tm
    full = pl.BlockSpec((tm, D_MODEL), lambda i: (i, 0))
    vec = pl.BlockSpec((1, D_MODEL), lambda i: (0, 0))
    once = pl.Buffered(1)
    return pl.pallas_call(
        functools.partial(_ffn_kernel, final=final),
        grid=(m_rows // tm,),
        in_specs=[pl.BlockSpec((tm, D_MODEL), lambda i: (b0 + i, 0)),
                  pl.BlockSpec((1, 1, N_MOD, D_MODEL), lambda i: (layer, mod_row(b0 + i, tm), 0, 0)),
                  vec,
                  pl.BlockSpec((1, D_MODEL, D_FF), lambda i: (layer, 0, 0), pipeline_mode=once),
                  pl.BlockSpec((1, D_MODEL, D_FF), lambda i: (layer, 0, 0), pipeline_mode=once),
                  pl.BlockSpec((1, D_FF, D_MODEL), lambda i: (layer, 0, 0), pipeline_mode=once),
                  vec],
        out_specs=full,
        out_shape=jax.ShapeDtypeStruct((m_rows, D_MODEL), F32),
        compiler_params=_cparams(("parallel",)),
        name="ffn",
    )(x, mod, gain, wg, wu, wd, fin)


W_TOK_PARTS = (((O_QKV, O_Z), C_QKV), ((O_CQ, O_QKV), C_B), ((O_A, O_GATES), C_B + B_LOGITS + LOGIT_A),
               ((O_KA, O_CQ), C_KA), ((O_QA, O_KA), C_QA))
W_GZ_PARTS = (((O_GATES, W_IN_COLS), 0), ((O_Z, O_A), 3 * D_MODEL))
W_ROWS = 256


def _layout_w_in_kernel(w_ref, tok_ref, gz_ref):
    w = w_ref[0]
    for (a, b), dst in W_TOK_PARTS:
        tok_ref[0, :, dst:dst + b - a] = w[:, a:b].astype(BF16)
    pad0, pad1 = C_B + B_LOGITS + LOGIT_END, C_B + B_LOGITS + LANES
    tok_ref[0, :, pad0:pad1] = jnp.zeros((w.shape[0], pad1 - pad0), BF16)
    for (a, b), dst in W_GZ_PARTS:
        gz_ref[0, :, dst:dst + b - a] = w[:, a:b].astype(BF16)


def _layout_w_in(w_in):
    n_layers, d, n = w_in.shape
    return pl.pallas_call(
        _layout_w_in_kernel,
        grid=(n_layers, d // W_ROWS),
        in_specs=[pl.BlockSpec((1, W_ROWS, n), lambda l, i: (l, i, 0))],
        out_specs=[pl.BlockSpec((1, W_ROWS, N_IN), lambda l, i: (l, i, 0)),
                   pl.BlockSpec((1, W_ROWS, N_GZ), lambda l, i: (l, i, 0))],
        out_shape=[jax.ShapeDtypeStruct((n_layers, d, N_IN), BF16),
                   jax.ShapeDtypeStruct((n_layers, d, N_GZ), BF16)],
        compiler_params=_cparams(("parallel", "parallel")),
        name="layout_w_in",
    )(w_in)


def _layout_w_uq(w_uq):
    n_layers = w_uq.shape[0]
    w = w_uq.reshape(n_layers, Q_LORA, HEADS, NOPE_B + ROPE_B)
    w = jnp.pad(w, ((0, 0), (0, 0), (0, 0), (0, LANES - NOPE_B - ROPE_B)))
    return w.reshape(n_layers, Q_LORA, HEADS * LANES).astype(BF16)


def _layout_w_ukv(w_ukv):
    n_layers = w_ukv.shape[0]
    w = w_ukv.reshape(n_layers, KV_LORA, HEADS, 2, HD)
    zero = jnp.zeros((n_layers, KV_LORA, HEADS, HD), w.dtype)
    wk_top = jnp.concatenate([w[:, :, :, 0], zero], axis=-1).reshape(n_layers, KV_LORA, HEADS * LANES)
    place = np.zeros((LANES, HEADS, LANES), np.float32)
    for d in range(ROPE_B):
        place[d, :, NOPE_B + d] = 1.0
    place = jnp.broadcast_to(jnp.asarray(place.reshape(LANES, HEADS * LANES)), (n_layers, LANES, HEADS * LANES))
    wk = jnp.concatenate([wk_top, place], axis=1).astype(BF16)
    wv = w[:, :, :, 1].reshape(n_layers, KV_LORA, HEADS * HD).astype(BF16)
    return wk, wv


def kernel(x_prompt, x_sample, cache_ka, cache_va, cache_ckv, cache_kpe, state_fwd, state_bwd, c, c_ctx,
           w_mod, b_mod, norm1, norm2, w_in, a_qnorm, a_knorm, b_qnorm, b_kvnorm, w_uq, w_ukv, c_conv,
           c_alog, c_dt_bias, c_onorm, w_pa, w_pb, w_pc, w_out, w_gate, w_up, w_down, final_norm):
    n_ctx, s_ctx, _ = x_prompt.shape
    n_lat, s_lat, _ = x_sample.shape
    n_layers = w_mod.shape[0]
    past = cache_ka.shape[2]
    ctx_rows = n_ctx * s_ctx
    lat_rows = n_lat * s_lat
    assert s_ctx == TOK and s_lat % TOK == 0 and ctx_rows % 512 == 0 and s_lat % 512 == 0
    assert past % TOK == 0 and n_lat + 1 <= 8 and ctx_rows % s_lat == 0
    assert w_in.shape[2] == W_IN_COLS and N_IN % LANES == 0 and C_B % (B_LOGITS + LANES) == 0

    def mod_row(i, tm):
        return jnp.where(i < ctx_rows // tm, 0, 1 + (i - ctx_rows // tm) // (s_lat // tm))

    def tab_row(i):
        nb = ctx_rows // TOK_AB
        return jnp.where(i < nb, 0, 1 + (i - nb) % (s_lat // TOK_AB))

    x = (x_prompt.reshape(ctx_rows, D_MODEL), x_sample.reshape(lat_rows, D_MODEL))
    cvec = jnp.zeros((8, D_MODEL), F32).at[0].set(c_ctx).at[1:1 + n_lat].set(c)
    mod = _modulation(cvec, w_mod, b_mod).reshape(n_layers, 8, N_MOD, D_MODEL)
    cos, sin = _rope_tables(s_lat)

    w_in_p, w_gz = _layout_w_in(w_in)
    w_uq_p = _layout_w_uq(w_uq)
    w_k_p, w_v_p = _layout_w_ukv(w_ukv)
    bf = lambda w: w.astype(BF16)
    w_pa_b, w_pb_b, w_pc_b, w_out_b = bf(w_pa), bf(w_pb), bf(w_pc), bf(w_out)
    w_gate_b, w_up_b, w_down_b = bf(w_gate), bf(w_up), bf(w_down)

    cache_rows = n_lat * n_layers * past
    ck4, cv4 = _cache_a(cache_ka.reshape(cache_rows, LANES), cache_va.reshape(cache_rows, LANES))
    kpe_pad = jnp.pad(cache_kpe.reshape(cache_rows, ROPE_B), ((0, 0), (0, LANES - ROPE_B)))
    ckb, cvb = _cache_b(cache_ckv.reshape(cache_rows, KV_LORA), kpe_pad, w_k_p, w_v_p,
                        lambda i: (i // (past // TOK)) % n_layers)

    lane_vec = lambda v: jnp.zeros((1, LANES), F32).at[0, LOGIT_A:LOGIT_B].set(v.reshape(N_AB))
    new = []
    for l in range(n_layers):
        y, va_raw, qa, k4, v4, kn_a, qb, kb, vb, ckvn = _token_stage(
            x, mod, l, norm1[l][None], w_in_p, cos, sin,
            jnp.tile(a_qnorm[l], HEADS)[None], jnp.tile(a_knorm[l], KV_HEADS_A)[None],
            b_qnorm[l][None], b_kvnorm[l][None], w_uq_p, w_k_p, w_v_p, mod_row, tab_row, ctx_rows)
        conv_args = (c_conv[l, :, 0, :], lane_vec(c_alog[l]), lane_vec(c_dt_bias[l]))
        gdn_ctx = _prep_c(y, *conv_args, 0, ctx_rows, s_ctx)
        gdn_lat = _prep_c(y, *conv_args, ctx_rows, lat_rows, s_lat)

        cache_blk = (past, lambda b, l=l: b * n_layers + l)
        ya_c = _attention(qa, k4, v4, None, GROUPS_A, 0, n_ctx, s_ctx, s_ctx, None)
        ya_l = _attention(qa, k4, v4, (ck4, cv4), GROUPS_A, ctx_rows, n_lat, s_lat, 256, cache_blk)
        yb_c = _attention(qb, kb, vb, None, GROUPS_B, 0, n_ctx, s_ctx, s_ctx, None)
        yb_l = _attention(qb, kb, vb, (ckb, cvb), GROUPS_B, ctx_rows, n_lat, s_lat, 256, cache_blk)

        of_c, ob_c, sf_c, sb_c = _gdn_scan(*gdn_ctx, None, n_ctx, s_ctx)
        s0 = (_state_to_pairs(state_fwd[:, l]), _state_to_pairs(state_bwd[:, l]))
        of_l, ob_l, _, _ = _gdn_scan(*gdn_lat, s0, n_lat, s_lat)

        x_mid = _merge(x, mod, l, (ya_c, ya_l, yb_c, yb_l, of_c, of_l, ob_c, ob_l), norm1[l][None], w_gz,
                       jnp.tile(c_onorm[l], HEADS)[None], w_pa_b, w_pb_b, w_pc_b, w_out_b, mod_row, ctx_rows)
        ffn_args = (mod, l, norm2[l][None], w_gate_b, w_up_b, w_down_b, final_norm[None], mod_row)
        if l < n_layers - 1:
            x = (_ffn(x_mid, *ffn_args, final=False),)
        else:
            y_prompt = _ffn(x_mid, *ffn_args, final=True, row0=0, n_rows=ctx_rows)
            y_sample = _ffn(x_mid, *ffn_args, final=True, row0=ctx_rows, n_rows=lat_rows)

        new.append((kn_a[:ctx_rows].reshape(n_ctx, s_ctx, KV_HEADS_A, HD),
                    va_raw[:ctx_rows].reshape(n_ctx, s_ctx, KV_HEADS_A, HD),
                    ckvn[:ctx_rows].reshape(n_ctx, s_ctx, KV_LORA),
                    y[:ctx_rows, CONV_CH:CONV_CH + ROPE_B].reshape(n_ctx, s_ctx, ROPE_B),
                    _pairs_to_state(sf_c), _pairs_to_state(sb_c)))

    stacked = [jnp.stack([n[i] for n in new], axis=1) for i in range(6)]
    return (y_prompt.reshape(n_ctx, s_ctx, D_MODEL), y_sample.reshape(n_lat, s_lat, D_MODEL)) + tuple(stacked)
```

```python
import functools

import numpy as np
import jax
import jax.numpy as jnp
from jax import lax
from jax.experimental import pallas as pl
from jax.experimental.pallas import tpu as pltpu

F32 = jnp.float32
BF16 = jnp.bfloat16

D_MODEL = 1024
GRID_W = 64
GRID_W_LOG2 = 6
assert 1 << GRID_W_LOG2 == GRID_W
ROPE_THETA = 10000.0
EPS = 1e-6
N_MOD = 6
HEADS = 8
KV_HEADS_A = 2
HD = 64
Q_LORA = 384
KV_LORA = 256
NOPE_B = 64
ROPE_B = 32
CHUNK = 64
CONV_CH = 3 * HEADS * HD
D_FF = 2816
LANES = 128

N_AB = 2 * HEADS
O_QA = 0
O_KA = O_QA + HEADS * HD
O_VA = O_KA + KV_HEADS_A * HD
O_CQ = O_VA + KV_HEADS_A * HD
O_CKV = O_CQ + Q_LORA
O_KPE = O_CKV + KV_LORA
O_QKV = O_KPE + ROPE_B
O_Z = O_QKV + CONV_CH
O_A = O_Z + HEADS * HD
O_B = O_A + N_AB
O_GATES = O_B + N_AB
W_IN_COLS = O_GATES + 3 * D_MODEL

C_QKV = 0
C_B = C_QKV + CONV_CH
B_LOGITS = Q_LORA + KV_LORA
LOGIT_A = ROPE_B
LOGIT_B = LOGIT_A + N_AB
LOGIT_END = LOGIT_B + N_AB
C_KA = C_B + B_LOGITS + LANES
C_VA = C_KA + KV_HEADS_A * HD
C_QA = C_VA + KV_HEADS_A * HD
N_IN = C_QA + HEADS * HD
N_GZ = 3 * D_MODEL + HEADS * HD

LOG2E = 1.4426950408889634
TOK = 256
TOK_AB = 512
TM_DENSE = 512
ROW_CHUNK = 512
VMEM_LIMIT = 56 * 1024 * 1024


def _cparams(sem):
    return pltpu.CompilerParams(dimension_semantics=sem, vmem_limit_bytes=VMEM_LIMIT)


def _dot(a, b):
    return jnp.dot(a, b, preferred_element_type=F32)


def _dot_nt(a, b):
    return lax.dot_general(a, b, (((1,), (1,)), ((), ())), preferred_element_type=F32)


def _dot_tn(a, b):
    return lax.dot_general(a, b, (((0,), (0,)), ((), ())), preferred_element_type=F32)


def _split3(x):
    hi = x.astype(BF16)
    r = x - hi.astype(F32)
    mid = r.astype(BF16)
    lo = (r - mid.astype(F32)).astype(BF16)
    return hi, mid, lo


def _dot3(x, w):
    hi, mid, lo = _split3(x)
    return _dot(hi, w) + _dot(mid, w) + _dot(lo, w)


def _group_sums(x, ones):
    w = min(x.shape[1], ones.shape[0])
    blk = ones[:w, :w]
    slabs = []
    for s in range(x.shape[1] // w):
        xs = x[:, s * w:(s + 1) * w]
        hi = xs.astype(BF16)
        lo = (xs - hi.astype(F32)).astype(BF16)
        slabs.append(_dot(hi, blk) + _dot(lo, blk))
    return slabs[0] if len(slabs) == 1 else jnp.concatenate(slabs, axis=1)


def _silu(x):
    return x * jax.nn.sigmoid(x)


def _rms_rows(x, g):
    return x * lax.rsqrt(jnp.mean(x * x, axis=-1, keepdims=True) + EPS) * g


def _swap_pairs(x, half):
    lane = lax.broadcasted_iota(jnp.int32, x.shape, 1)
    n = x.shape[1]
    return jnp.where((lane & half) == 0, pltpu.roll(x, n - half, 1), pltpu.roll(x, half, 1))


def _rope_cols(x, cos, sin, half):
    outs = []
    for s in range(x.shape[1] // LANES):
        xs = x[:, s * LANES:(s + 1) * LANES]
        outs.append(xs * cos + _swap_pairs(xs, half) * sin)
    return outs[0] if len(outs) == 1 else jnp.concatenate(outs, axis=1)


def _mod_kernel(c_ref, w_ref, b_ref, o_ref):
    s = _silu(c_ref[...]).astype(BF16)
    o_ref[0] = _dot(s, w_ref[0].astype(BF16)) + b_ref[0]


def _modulation(cvec, w_mod, b_mod):
    n_layers = w_mod.shape[0]
    n_out = w_mod.shape[2]
    tn = 1536
    return pl.pallas_call(
        _mod_kernel,
        grid=(n_layers, n_out // tn),
        in_specs=[pl.BlockSpec((8, D_MODEL), lambda l, j: (0, 0)),
                  pl.BlockSpec((1, D_MODEL, tn), lambda l, j: (l, 0, j)),
                  pl.BlockSpec((1, 1, tn), lambda l, j: (l, 0, j))],
        out_specs=pl.BlockSpec((1, 8, tn), lambda l, j: (l, 0, j)),
        out_shape=jax.ShapeDtypeStruct((n_layers, 8, n_out), F32),
        compiler_params=_cparams(("parallel", "parallel")),
        name="modulation",
    )(cvec, w_mod, b_mod.reshape(n_layers, 1, n_out))


def _x_rows(x_refs, ctx_blocks, rows):
    if len(x_refs) == 1:
        return x_refs[0][rows, :]
    return jnp.where(pl.program_id(0) < ctx_blocks, x_refs[0][rows, :], x_refs[1][rows, :])


def _x_specs(x_parts, tm, ctx_rows):
    if len(x_parts) == 1:
        return [pl.BlockSpec((tm, D_MODEL), lambda i: (i, 0))]
    nb_ctx, nb_lat = ctx_rows // tm, x_parts[1].shape[0] // tm
    return [pl.BlockSpec((tm, D_MODEL), lambda i: (jnp.minimum(i, nb_ctx - 1), 0)),
            pl.BlockSpec((tm, D_MODEL), lambda i: (jnp.clip(i - nb_ctx, 0, nb_lat - 1), 0))]


def _token_kernel(*refs, n_x, ctx_blocks):
    x_refs = refs[:n_x]
    (mod_ref, g_ref, w_ref, cos_a, sin_a, cos_q, sin_q, cos_k, sin_k, gq_a, gk_a, ones_ref, place_ref,
     gq_b, gkv_b, wq_ref, wk_ref, wv_ref,
     yc_ref, va_ref, qa_ref, k4_ref, vta_ref, kna_ref, qb_ref, kb_ref, vtb_ref, ckv_ref, y_scr) = refs[n_x:]
    m = mod_ref[0, 0]
    for r in range(0, y_scr.shape[0], ROW_CHUNK):
        rows = slice(r, r + ROW_CHUNK)
        y = _rms_rows(_x_rows(x_refs, ctx_blocks, rows), g_ref[...])
        h = (y * (1.0 + m[1:2]) + m[0:1]).astype(BF16)
        y_scr[rows, :] = _dot(h, w_ref[0])
    yc_ref[:, :CONV_CH] = y_scr[:, C_QKV:C_QKV + CONV_CH]
    yc_ref[:, CONV_CH:] = y_scr[:, C_B + B_LOGITS:C_B + B_LOGITS + LANES]
    va_ref[...] = y_scr[:, C_VA:C_VA + LANES]
    _prep_a_kernel(y_scr.at[:, C_QA:C_QA + HEADS * HD], y_scr.at[:, C_KA:C_KA + 2 * LANES],
                   cos_a, sin_a, gq_a, gk_a, ones_ref, place_ref, qa_ref, k4_ref, vta_ref, kna_ref)
    _prep_b_kernel(y_scr.at[:, C_B:C_B + B_LOGITS + LANES], cos_q, sin_q, cos_k, sin_k, gq_b, gkv_b,
                   wq_ref, wk_ref, wv_ref, qb_ref, kb_ref, vtb_ref, ckv_ref)


def _token_stage(x_parts, mod, layer, norm_gain, w, cos, sin, gq_a, gk_a, gq_b, gkv_b, wq, wk, wv,
                 mod_row, tab_row, ctx_rows):
    m_rows = sum(p.shape[0] for p in x_parts)
    tm = TOK_AB
    rows = lambda w_: pl.BlockSpec((tm, w_), lambda i: (i, 0))
    cols = lambda h_: pl.BlockSpec((h_, tm), lambda i: (0, i))
    vec = lambda w_: pl.BlockSpec((1, w_), lambda i: (0, 0))
    tab = lambda a: pl.BlockSpec((1, tm, LANES), lambda i: (a, tab_row(i), 0))
    lw = lambda k, n: pl.BlockSpec((1, k, n), lambda i: (layer, 0, 0))
    f32 = lambda *shape: jax.ShapeDtypeStruct(shape, F32)
    bf16 = lambda *shape: jax.ShapeDtypeStruct(shape, BF16)
    return pl.pallas_call(
        functools.partial(_token_kernel, n_x=len(x_parts), ctx_blocks=ctx_rows // tm),
        grid=(m_rows // tm,),
        in_specs=_x_specs(x_parts, tm, ctx_rows)
        + [pl.BlockSpec((1, 1, N_MOD, D_MODEL), lambda i: (layer, mod_row(i, tm), 0, 0)),
           vec(D_MODEL), lw(D_MODEL, N_IN),
           tab(0), tab(0), tab(1), tab(1), tab(2), tab(2),
           vec(HEADS * HD), vec(LANES),
           pl.BlockSpec((256, 256), lambda i: (0, 0)), pl.BlockSpec((LANES, 4 * LANES), lambda i: (0, 0)),
           vec(Q_LORA), vec(KV_LORA),
           lw(Q_LORA, HEADS * LANES), lw(KV_LORA + LANES, HEADS * LANES), lw(KV_LORA, HEADS * HD)],
        out_specs=[rows(CONV_CH + LANES), rows(LANES),
                   rows(4 * LANES), rows(4 * LANES), cols(LANES), rows(LANES),
                   rows(HEADS * LANES), rows(HEADS * LANES), cols(HEADS * HD), rows(KV_LORA)],
        out_shape=[f32(m_rows, CONV_CH + LANES), f32(m_rows, LANES),
                   bf16(m_rows, 4 * LANES), bf16(m_rows, 4 * LANES), bf16(LANES, m_rows), f32(m_rows, LANES),
                   bf16(m_rows, HEADS * LANES), bf16(m_rows, HEADS * LANES), bf16(HEADS * HD, m_rows),
                   f32(m_rows, KV_LORA)],
        scratch_shapes=[pltpu.VMEM((tm, N_IN), F32)],
        compiler_params=_cparams(("parallel",)),
        name="token_stage",
    )(*x_parts, mod, norm_gain, w, cos, sin, cos, sin, cos, sin, gq_a, gk_a, _ones_blockdiag(256, HD),
      _place4(), gq_b, gkv_b, wq, wk, wv)


def _rope_table_kernel(freq_ref, col_ref, sign_ref, cos_ref, sin_ref):
    shape = cos_ref.shape
    pos = pl.program_id(1) * TOK_AB + lax.broadcasted_iota(jnp.int32, shape, 1) - TOK_AB
    grid_row = (pos >> GRID_W_LOG2).astype(F32)
    grid_col = (pos & (GRID_W - 1)).astype(F32)
    a = jnp.where(col_ref[...] > 0.5, grid_col, grid_row) * freq_ref[...]
    a = jnp.where(pos >= 0, a, 0.0)
    cos_ref[...] = jnp.cos(a)
    sin_ref[...] = jnp.sin(a) * sign_ref[...]


def _rope_tables(n_lat):
    lane = np.arange(LANES)

    def lane_pattern(nf, d, active):
        inv_freq = np.float32(ROPE_THETA) ** (-np.arange(nf, dtype=np.float32) / np.float32(nf))
        dd = np.where(active, d, 0)
        freq = np.where(active, inv_freq[dd % nf], 0.0).astype(np.float32)
        use_col = (dd >= 2 * nf).astype(np.float32)
        sign = np.where((dd % (2 * nf)) < nf, -1.0, 1.0).astype(np.float32)
        return freq, use_col, sign

    d_q = lane - NOPE_B
    patterns = [lane_pattern(HD // 4, lane % HD, np.ones(LANES, bool)),
                lane_pattern(ROPE_B // 4, d_q, (d_q >= 0) & (d_q < ROPE_B)),
                lane_pattern(ROPE_B // 4, lane, lane < ROPE_B)]
    freq, use_col, sign = [jnp.asarray(np.stack([p[t] for p in patterns]))[:, None, :] for t in range(3)]
    rows = TOK_AB + n_lat
    spec = pl.BlockSpec((1, TOK_AB, LANES), lambda a, i: (a, i, 0))
    vec = pl.BlockSpec((1, 1, LANES), lambda a, i: (a, 0, 0))
    return pl.pallas_call(
        _rope_table_kernel,
        grid=(3, rows // TOK_AB),
        in_specs=[vec, vec, vec],
        out_specs=[spec, spec],
        out_shape=[jax.ShapeDtypeStruct((3, rows, LANES), F32)] * 2,
        compiler_params=_cparams(("parallel", "parallel")),
        name="rope_tables",
    )(freq, use_col, sign)


def _prep_a_kernel(qa_ref, kv_ref, cos_ref, sin_ref, gq_ref, gk_ref, ones_ref, place_ref,
                   q_ref, k4_ref, vt_ref, kn_ref):
    cos = cos_ref[0]
    sin = sin_ref[0]
    ones = ones_ref[...]
    q = qa_ref[...]
    qn = q * lax.rsqrt(_group_sums(q * q, ones) * (1.0 / HD) + EPS) * gq_ref[...]
    q_ref[...] = (_rope_cols(qn, cos, sin, HD // 4) * (HD ** -0.5 * LOG2E)).astype(BF16)
    k = kv_ref[:, :LANES]
    kn = k * lax.rsqrt(_group_sums(k * k, ones) * (1.0 / HD) + EPS) * gk_ref[...]
    kn_ref[...] = kn
    k4_ref[...] = _dot(_rope_cols(kn, cos, sin, HD // 4).astype(BF16), place_ref[...]).astype(BF16)
    vt_ref[...] = kv_ref[:, LANES:].T.astype(BF16)


def _place4():
    p = np.zeros((LANES, 4 * LANES), np.float32)
    for kv in range(KV_HEADS_A):
        for side in range(2):
            for d in range(HD):
                p[kv * HD + d, (2 * kv + side) * LANES + side * HD + d] = 1.0
    return jnp.asarray(p, BF16)


def _ones_blockdiag(n, group):
    idx = np.arange(n) // group
    return jnp.asarray((idx[:, None] == idx[None, :]).astype(np.float32), BF16)


def _cache_a_kernel(k_ref, v_ref, place_ref, k4_ref, vt_ref):
    k4_ref[...] = _dot(k_ref[...].astype(BF16), place_ref[...]).astype(BF16)
    vt_ref[...] = v_ref[...].T.astype(BF16)


def _cache_a(ck, cv):
    rows = ck.shape[0]
    narrow = pl.BlockSpec((TOK, LANES), lambda i: (i, 0))
    return pl.pallas_call(
        _cache_a_kernel,
        grid=(rows // TOK,),
        in_specs=[narrow, narrow, pl.BlockSpec((LANES, 512), lambda i: (0, 0))],
        out_specs=[pl.BlockSpec((TOK, 512), lambda i: (i, 0)), pl.BlockSpec((LANES, TOK), lambda i: (0, i))],
        out_shape=[jax.ShapeDtypeStruct((rows, 512), BF16), jax.ShapeDtypeStruct((LANES, rows), BF16)],
        compiler_params=_cparams(("parallel",)),
        name="cache_a",
    )(ck, cv, _place4())


def _prep_b_kernel(g_ref, cq_ref, sq_ref, ck_ref, sk_ref, gq_ref, gkv_ref, wq_ref, wk_ref, wv_ref,
                   q_ref, k_ref, vt_ref, ckv_ref):
    g = g_ref[...]
    cqn = _rms_rows(g[:, :Q_LORA], gq_ref[...])
    q = _dot(cqn.astype(BF16), wq_ref[0])
    q = _rope_cols(q, cq_ref[0], sq_ref[0], ROPE_B // 4) * ((NOPE_B + ROPE_B) ** -0.5 * LOG2E)
    q_ref[...] = q.astype(BF16)
    ckvn = _rms_rows(g[:, Q_LORA:Q_LORA + KV_LORA], gkv_ref[...])
    ckv_ref[...] = ckvn
    kpe = _rope_cols(g[:, B_LOGITS:], ck_ref[0], sk_ref[0], ROPE_B // 4)
    ckv_b = ckvn.astype(BF16)
    kin = jnp.concatenate([ckv_b, kpe.astype(BF16)], axis=1)
    k_ref[...] = _dot(kin, wk_ref[0]).astype(BF16)
    vt_ref[...] = _dot(ckv_b, wv_ref[0]).T.astype(BF16)


def _cache_b_kernel(c_ref, p_ref, wk_ref, wv_ref, k_ref, vt_ref):
    ckv_b = c_ref[...].astype(BF16)
    kin = jnp.concatenate([ckv_b, p_ref[...].astype(BF16)], axis=1)
    k_ref[...] = _dot(kin, wk_ref[0]).astype(BF16)
    vt_ref[...] = _dot(ckv_b, wv_ref[0]).T.astype(BF16)


def _cache_b(ckv, kpe, wk, wv, layer_of_block):
    rows = ckv.shape[0]
    return pl.pallas_call(
        _cache_b_kernel,
        grid=(rows // TOK,),
        in_specs=[pl.BlockSpec((TOK, KV_LORA), lambda i: (i, 0)),
                  pl.BlockSpec((TOK, LANES), lambda i: (i, 0)),
                  pl.BlockSpec((1, KV_LORA + LANES, HEADS * LANES), lambda i: (layer_of_block(i), 0, 0)),
                  pl.BlockSpec((1, KV_LORA, HEADS * HD), lambda i: (layer_of_block(i), 0, 0))],
        out_specs=[pl.BlockSpec((TOK, HEADS * LANES), lambda i: (i, 0)),
                   pl.BlockSpec((HEADS * HD, TOK), lambda i: (0, i))],
        out_shape=[jax.ShapeDtypeStruct((rows, HEADS * LANES), BF16),
                   jax.ShapeDtypeStruct((HEADS * HD, rows), BF16)],
        compiler_params=_cparams(("parallel",)),
        name="cache_b",
    )(ckv, kpe, wk, wv)


ATT_KEY_TILE = 256
ATT_LOOKAHEAD = 5


def _attn_kernel(*refs, groups, has_cache):
    if has_cache:
        q_ref, kc_ref, vc_ref, kn_ref, vn_ref, o_ref = refs
        segments = [(kc_ref, vc_ref), (kn_ref, vn_ref)]
    else:
        q_ref, kn_ref, vn_ref, o_ref = refs
        segments = [(kn_ref, vn_ref)]
    tiles = [(k_ref, v_ref, t * ATT_KEY_TILE) for k_ref, v_ref in segments
             for t in range(k_ref.shape[0] // ATT_KEY_TILE)]
    q = [q_ref[:, qg * LANES:(qg + 1) * LANES] for qg, _, _ in groups]

    def scores(h, i):
        k_ref, _, t0 = tiles[i]
        kg = groups[h][1]
        return _dot_nt(k_ref[t0:t0 + ATT_KEY_TILE, kg * LANES:(kg + 1) * LANES], q[h])

    heads = range(len(groups))
    jobs = [(i, h) for i in range(len(tiles)) for h in heads]
    pending = {j: scores(h, i) for j, (i, h) in enumerate(jobs[:ATT_LOOKAHEAD])}
    ones_rows = (lax.broadcasted_iota(jnp.int32, (16, ATT_KEY_TILE), 0) == 0).astype(BF16)
    m, acc = {}, {}
    for j, (i, h) in enumerate(jobs):
        if j + ATT_LOOKAHEAD < len(jobs):
            i_n, h_n = jobs[j + ATT_LOOKAHEAD]
            pending[j + ATT_LOOKAHEAD] = scores(h_n, i_n)
        s = pending.pop(j)
        _, v_ref, t0 = tiles[i]
        vrow = groups[h][2]
        v_t = jnp.concatenate([v_ref[vrow:vrow + HD, t0:t0 + ATT_KEY_TILE], ones_rows], axis=0)
        m_tile = jnp.max(s, axis=0, keepdims=True)
        if i == 0:
            m[h] = m_tile
            acc[h] = _dot(v_t, jnp.exp2(s - m_tile).astype(BF16))
        else:
            m_new = jnp.maximum(m[h], m_tile)
            alpha = jnp.exp2(m[h] - m_new)
            acc[h] = acc[h] * alpha + _dot(v_t, jnp.exp2(s - m_new).astype(BF16))
            m[h] = m_new
    outs = [acc[h][:HD] / acc[h][HD:HD + 1] for h in heads]
    o_ref[...] = jnp.concatenate(outs, axis=0).T.astype(BF16)


def _attention(q, kn, vn, cache, groups, row0, n_seq, seq, tq, cache_block):
    qw, kw, vr = q.shape[1], kn.shape[1], vn.shape[0]
    nq = seq // tq
    qb0, sb0 = row0 // tq, row0 // seq
    in_specs = [pl.BlockSpec((tq, qw), lambda b, i: (qb0 + b * nq + i, 0))]
    args = [q]
    if cache is not None:
        past = cache_block[0]
        in_specs += [pl.BlockSpec((past, kw), lambda b, i: (cache_block[1](b), 0)),
                     pl.BlockSpec((vr, past), lambda b, i: (0, cache_block[1](b)))]
        args += list(cache)
    in_specs += [pl.BlockSpec((seq, kw), lambda b, i: (sb0 + b, 0)),
                 pl.BlockSpec((vr, seq), lambda b, i: (0, sb0 + b))]
    args += [kn, vn]
    return pl.pallas_call(
        functools.partial(_attn_kernel, groups=groups, has_cache=cache is not None),
        grid=(n_seq, nq),
        in_specs=in_specs,
        out_specs=pl.BlockSpec((tq, 512), lambda b, i: (b * nq + i, 0)),
        out_shape=jax.ShapeDtypeStruct((n_seq * seq, 512), BF16),
        compiler_params=_cparams(("parallel", "parallel")),
        name="attention",
    )(*args)


GROUPS_A = tuple((h // 2, 2 * (h // 4) + (h % 2), (h // 4) * HD) for h in range(HEADS))
GROUPS_B = tuple((h, h, h * HD) for h in range(HEADS))


def _prep_c_kernel(x_ref, prev_ref, next_ref, ab_ref, w_ref, alog_ref, dtb_ref, ones_ref, exp_ref, tri_ref,
                   q_ref, k_ref, v_ref, gf_ref, gb_ref, bf_ref, bb_ref, *, blocks_per_seq):
    j = pl.program_id(0) % blocks_per_seq
    is_first, is_last = j == 0, j == blocks_per_seq - 1
    x = x_ref[...]
    rows = x.shape[0]
    rid = lax.broadcasted_iota(jnp.int32, x.shape, 0)
    halo_p = jnp.where(is_first, 0.0, prev_ref[7:8, :])
    halo_n = jnp.where(is_last, 0.0, next_ref[0:1, :])
    x_prev = jnp.where(rid == 0, halo_p, pltpu.roll(x, 1, 0))
    x_next = jnp.where(rid == rows - 1, halo_n, pltpu.roll(x, rows - 1, 0))
    w = w_ref[...]
    s = _silu(x_prev * w[0:1] + x * w[1:2] + x_next * w[2:3])
    ones = ones_ref[...]
    hw = HEADS * HD
    q, k = s[:, :hw], s[:, hw:2 * hw]
    q_ref[...] = q * lax.rsqrt(_group_sums(q * q, ones) + EPS) * (HD ** -0.5)
    k_ref[...] = k * lax.rsqrt(_group_sums(k * k, ones) + EPS)
    v_ref[...] = s[:, 2 * hw:]
    ab = ab_ref[...]
    t = ab + dtb_ref[...]
    softplus = jnp.maximum(t, 0.0) + jnp.log1p(jnp.exp(-jnp.abs(t)))
    g = -jnp.exp(alog_ref[...]) * softplus
    lane = lax.broadcasted_iota(jnp.int32, ab.shape, 1)
    parts = _split3(g)
    cum_f = sum(_dot(tri_ref[0], part) for part in parts)
    cum_b = sum(_dot(tri_ref[1], part) for part in parts)
    both = jnp.where(lane < LOGIT_A + HEADS, cum_f, jnp.where(lane < LOGIT_B, cum_b, jax.nn.sigmoid(ab)))
    e = _dot3(both, exp_ref[...])
    gf_ref[...] = e[:, 0 * hw:1 * hw]
    gb_ref[...] = e[:, 1 * hw:2 * hw]
    bf_ref[...] = e[:, 2 * hw:3 * hw]
    bb_ref[...] = e[:, 3 * hw:4 * hw]


def _expand_matrix():
    e = np.zeros((LANES, 4 * HEADS * HD), np.float32)
    for kind in range(2):
        for d in range(2):
            for h in range(HEADS):
                src = LOGIT_A + N_AB * kind + HEADS * d + h
                dst = (2 * kind + d) * HEADS * HD + h * HD
                e[src, dst:dst + HD] = 1.0
    return jnp.asarray(e, BF16)


def _chunk_tri():
    r = np.arange(TOK)
    same = (r[:, None] // CHUNK) == (r[None, :] // CHUNK)
    low = same & (r[:, None] >= r[None, :])
    up = same & (r[:, None] <= r[None, :])
    return jnp.asarray(np.stack([low, up]).astype(np.float32), BF16)


def _prep_c(y, w_conv, alog, dtb, row0, n_rows, seq):
    nb8 = TOK // 8
    b0 = row0 // TOK
    last8 = (row0 + n_rows) // 8 - 1
    qkv_blk = 0
    out = pl.BlockSpec((TOK, HEADS * HD), lambda i: (i, 0))
    vec = pl.BlockSpec((1, LANES), lambda i: (0, 0))
    return pl.pallas_call(
        functools.partial(_prep_c_kernel, blocks_per_seq=seq // TOK),
        grid=(n_rows // TOK,),
        in_specs=[pl.BlockSpec((TOK, CONV_CH), lambda i: (b0 + i, qkv_blk)),
                  pl.BlockSpec((8, CONV_CH), lambda i: (jnp.maximum((b0 + i) * nb8 - 1, 0), qkv_blk)),
                  pl.BlockSpec((8, CONV_CH), lambda i: (jnp.minimum((b0 + i + 1) * nb8, last8), qkv_blk)),
                  pl.BlockSpec((TOK, LANES), lambda i: (b0 + i, CONV_CH // LANES)),
                  pl.BlockSpec((3, CONV_CH), lambda i: (0, 0)),
                  vec, vec,
                  pl.BlockSpec((256, 256), lambda i: (0, 0)),
                  pl.BlockSpec((LANES, 4 * HEADS * HD), lambda i: (0, 0)),
                  pl.BlockSpec((2, TOK, TOK), lambda i: (0, 0, 0))],
        out_specs=[out] * 7,
        out_shape=[jax.ShapeDtypeStruct((n_rows, HEADS * HD), F32)] * 7,
        compiler_params=_cparams(("parallel",)),
        name="prep_c",
    )(y, y, y, y, w_conv, alog, dtb, _ones_blockdiag(256, HD), _expand_matrix(), _chunk_tri())


def _gdn_kernel(*refs, has_s0, n_chunks, n_par):
    ins = refs[:10]
    pos = 10
    if has_s0:
        s0_refs = refs[pos:pos + 2]
        pos += 2
    o_refs = refs[pos:pos + 2]
    s_out_refs = refs[pos + 2:pos + 4]
    s_scr = refs[pos + 4]
    c = pl.program_id(1)

    @pl.when(c == 0)
    def _():
        for d in range(2):
            s_scr[d] = s0_refs[d][...] if has_s0 else jnp.zeros(s_scr.shape[1:], F32)

    n2 = 2 * CHUNK
    ri = lax.broadcasted_iota(jnp.int32, (n2, n2), 0)
    ci = lax.broadcasted_iota(jnp.int32, (n2, n2), 1)
    same = (ri // CHUNK) == (ci // CHUNK)
    left = lax.broadcasted_iota(jnp.int32, (CHUNK, n2), 1) < CHUNK
    eye = (ri == ci).astype(F32)

    def level_mask(lvl, lower):
        bit = 1 << lvl
        hi, lo = (ri, ci) if lower else (ci, ri)
        return ((ri >> (lvl + 1)) == (ci >> (lvl + 1))) & ((hi & bit) != 0) & ((lo & bit) == 0)

    levels = [[level_mask(lvl, d == 0) for lvl in range(6)] for d in range(2)]

    def block_diag(x):
        return jnp.concatenate([jnp.where(left, x, 0.0), jnp.where(left, 0.0, x)], axis=0)

    def anti_diag(x):
        xr = pltpu.roll(x, CHUNK, 1)
        return jnp.concatenate([jnp.where(left, 0.0, xr), jnp.where(left, xr, 0.0)], axis=0)

    chains = [(d, n, p) for n in range(n_par) for d in range(2) for p in range(HEADS // 2)]
    incl = [same & ((ri >= ci) if d == 0 else (ri <= ci)) for d in range(2)]
    strict = [same & ((ri > ci) if d == 0 else (ri < ci)) for d in range(2)]

    def load(which):
        return [ins[5 * d + which][n, 0, :, p * n2:(p + 1) * n2] for d, n, p in chains]

    q, k, v, gc, beta = load(0), load(1), load(2), load(3), load(4)
    gtot = [gi[CHUNK - 1:CHUNK, :] if d == 0 else gi[0:1, :] for (d, _, _), gi in zip(chains, gc)]
    eg = [jnp.exp(gi) for gi in gc]
    kb = [ki * bi for ki, bi in zip(k, beta)]
    decay = []
    for (d, _, _), gi in zip(chains, gc):
        gcr = pltpu.roll(gi, CHUNK, 1)
        gcol = jnp.concatenate([jnp.where(left, gi, gcr), jnp.where(left, gcr, gi)], axis=0)
        diff = gcol - gcol.T
        decay.append(jnp.where(incl[d], jnp.exp(jnp.where(incl[d], diff, 0.0)), 0.0))
    k_bd = [block_diag(ki).astype(BF16) for ki in k]
    kk = [_dot_nt(block_diag(kbi).astype(BF16), kbd) for kbi, kbd in zip(kb, k_bd)]
    qk = [_dot_nt(block_diag(qi).astype(BF16), kbd) for qi, kbd in zip(q, k_bd)]
    a_mat = [jnp.where(strict[d], kki * di, 0.0) for (d, _, _), kki, di in zip(chains, kk, decay)]
    qk = [(qki * di).astype(BF16) for qki, di in zip(qk, decay)]
    x = []
    for vi, bi, kbi, egi in zip(v, beta, kb, eg):
        vb = vi * bi
        kbe_r = pltpu.roll(kbi * egi, CHUNK, 1)
        x.append(jnp.concatenate([jnp.where(left, vb, kbe_r), jnp.where(left, kbe_r, vb)], axis=0))
    inv = [eye - jnp.where(levels[d][0], ai, 0.0) for (d, _, _), ai in zip(chains, a_mat)]
    a_b = [ai.astype(BF16) for ai in a_mat]
    for lvl in range(1, 6):
        d_b = [di.astype(BF16) for di in inv]
        t = [_dot(ai, di).astype(BF16) for ai, di in zip(a_b, d_b)]
        inv = [di - jnp.where(levels[d][lvl], _dot(dbi, ti), 0.0)
               for (d, _, _), di, dbi, ti in zip(chains, inv, d_b, t)]
    x = [xi + _dot((di - eye).astype(BF16), xi.astype(BF16)) for xi, di in zip(x, inv)]
    s = [s_scr[d, n, p] for d, n, p in chains]
    s_b = [si.astype(BF16) for si in s]
    ws = [_dot(jnp.where(same, 0.0, xi).astype(BF16), sbi) for xi, sbi in zip(x, s_b)]
    v_new = [(jnp.where(same, xi, 0.0) - wi).astype(BF16) for xi, wi in zip(x, ws)]
    o_s = [_dot(anti_diag(qi * egi).astype(BF16), sbi) for qi, egi, sbi in zip(q, eg, s_b)]
    o_v = [_dot(qki, vni) for qki, vni in zip(qk, v_new)]
    kd = [anti_diag(ki * jnp.exp(gt - gi)).astype(BF16) for ki, gt, gi in zip(k, gtot, gc)]
    s_upd = [_dot_tn(kdi, vni) for kdi, vni in zip(kd, v_new)]
    for i, (d, n, p) in enumerate(chains):
        o = o_s[i] + o_v[i]
        o_refs[d][n, 0, :, p * n2:(p + 1) * n2] = o[:CHUNK] + o[CHUNK:]
        s_scr[d, n, p] = s[i] * jnp.exp(gtot[i]) + s_upd[i]

    @pl.when(c == n_chunks - 1)
    def _():
        for d in range(2):
            s_out_refs[d][...] = s_scr[d]


GDN_SEQS_PER_STEP = 2


def _gdn_scan(qn, kn, v, gcf, gcb, bf, bb, s0, n_seq, seq):
    nc = seq // CHUNK
    hw = HEADS * HD
    n2 = 2 * CHUNK
    n_par = max(p for p in range(1, GDN_SEQS_PER_STEP + 1) if n_seq % p == 0)
    view = lambda a: a.reshape(n_seq, nc, CHUNK, hw)
    fwd = pl.BlockSpec((n_par, 1, CHUNK, hw), lambda b, c: (b, c, 0, 0))
    bwd = pl.BlockSpec((n_par, 1, CHUNK, hw), lambda b, c: (b, nc - 1 - c, 0, 0))
    state = pl.BlockSpec((n_par, HEADS // 2, n2, n2), lambda b, c: (b, 0, 0, 0))
    in_specs = [fwd] * 5 + [bwd] * 5
    args = [view(a) for a in (qn, kn, v, gcf, bf, qn, kn, v, gcb, bb)]
    if s0 is not None:
        in_specs += [state, state]
        args += list(s0)
    st_shape = jax.ShapeDtypeStruct((n_seq, HEADS // 2, n2, n2), F32)
    o_shape = jax.ShapeDtypeStruct((n_seq, nc, CHUNK, hw), F32)
    o_f, o_b, s_f, s_b = pl.pallas_call(
        functools.partial(_gdn_kernel, has_s0=s0 is not None, n_chunks=nc, n_par=n_par),
        grid=(n_seq // n_par, nc),
        in_specs=in_specs,
        out_specs=[fwd, bwd, state, state],
        out_shape=[o_shape, o_shape, st_shape, st_shape],
        scratch_shapes=[pltpu.VMEM((2, n_par, HEADS // 2, n2, n2), F32)],
        compiler_params=_cparams(("parallel", "arbitrary")),
        name="gdn_scan",
    )(*args)
    return o_f.reshape(n_seq * seq, hw), o_b.reshape(n_seq * seq, hw), s_f, s_b


def _state_to_pairs(s):
    b = s.shape[0]
    s = s.reshape(b, HEADS // 2, 2, HD, HD)
    z = jnp.zeros_like(s[:, :, 0])
    top = jnp.concatenate([z, s[:, :, 1]], axis=-1)
    bot = jnp.concatenate([s[:, :, 0], z], axis=-1)
    return jnp.concatenate([top, bot], axis=-2)


def _pairs_to_state(s):
    b = s.shape[0]
    even = s[:, :, HD:, :HD]
    odd = s[:, :, :HD, HD:]
    return jnp.stack([even, odd], axis=2).reshape(b, HEADS, HD, HD)


def _merge_kernel(*refs, n_x, ctx_blocks):
    x_refs, mod_ref, refs = refs[:n_x], refs[n_x], refs[n_x + 1:]
    pairs, (n1_ref, wgz_ref, gain_ref, ones_ref, wa_ref, wb_ref, wc_ref, wo_ref, o_ref) = refs[:8], refs[8:]
    is_ctx = pl.program_id(0) < ctx_blocks
    ya, yb, o_f, o_b = [jnp.where(is_ctx, pairs[2 * t][...], pairs[2 * t + 1][...]) for t in range(4)]
    m = mod_ref[0, 0]
    chunks = [slice(r, r + ROW_CHUNK) for r in range(0, o_ref.shape[0], ROW_CHUNK)]
    pa = [_dot(ya[c], wa_ref[0]) for c in chunks]
    pb = [_dot(yb[c], wb_ref[0]) for c in chunks]
    x = [_x_rows(x_refs, ctx_blocks, c) for c in chunks]
    h = [(_rms_rows(xc, n1_ref[...]) * (1.0 + m[1:2]) + m[0:1]).astype(BF16) for xc in x]
    gz = [_dot(hc, wgz_ref[0]) for hc in h]
    yc = []
    for c, gzc in zip(chunks, gz):
        o = o_f[c] + o_b[c]
        ms = _group_sums(o * o, ones_ref[...]) * (1.0 / HD)
        yc.append((o * lax.rsqrt(ms + EPS) * gain_ref[...] * _silu(gzc[:, 3 * D_MODEL:])).astype(BF16))
    pc = [_dot(ycc, wc_ref[0]) for ycc in yc]
    merged = [(jax.nn.sigmoid(g[:, :D_MODEL]) * a + jax.nn.sigmoid(g[:, D_MODEL:2 * D_MODEL]) * b
               + jax.nn.sigmoid(g[:, 2 * D_MODEL:3 * D_MODEL]) * cc).astype(BF16)
              for g, a, b, cc in zip(gz, pa, pb, pc)]
    mix = [_dot(mc, wo_ref[0]) for mc in merged]
    for c, xc, mixc in zip(chunks, x, mix):
        o_ref[c, :] = xc + m[2:3] * mixc


def _merge(x_parts, mod, layer, mixers, norm_gain, w_gz, gain, wa, wb, wc, wo, mod_row, ctx_rows):
    m_rows = sum(p.shape[0] for p in x_parts)
    tm = TM_DENSE
    nb_ctx = ctx_rows // tm
    nb_lat = (m_rows - ctx_rows) // tm
    ctx_half = pl.BlockSpec((tm, 512), lambda i: (jnp.minimum(i, nb_ctx - 1), 0))
    lat_half = pl.BlockSpec((tm, 512), lambda i: (jnp.clip(i - nb_ctx, 0, nb_lat - 1), 0))
    full = pl.BlockSpec((tm, D_MODEL), lambda i: (i, 0))

    def wspec(k):
        return pl.BlockSpec((1, k, D_MODEL), lambda i: (layer, 0, 0))

    return pl.pallas_call(
        functools.partial(_merge_kernel, n_x=len(x_parts), ctx_blocks=nb_ctx),
        grid=(m_rows // tm,),
        in_specs=_x_specs(x_parts, tm, ctx_rows)
        + [pl.BlockSpec((1, 1, N_MOD, D_MODEL), lambda i: (layer, mod_row(i, tm), 0, 0))]
        + [ctx_half, lat_half] * 4
        + [pl.BlockSpec((1, D_MODEL), lambda i: (0, 0)),
           pl.BlockSpec((1, D_MODEL, N_GZ), lambda i: (layer, 0, 0)),
           pl.BlockSpec((1, 512), lambda i: (0, 0)),
           pl.BlockSpec((256, 256), lambda i: (0, 0)),
           wspec(512), wspec(512), wspec(512), wspec(D_MODEL)],
        out_specs=full,
        out_shape=jax.ShapeDtypeStruct((m_rows, D_MODEL), F32),
        compiler_params=_cparams(("parallel",)),
        name="merge_out",
    )(*x_parts, mod, *mixers, norm_gain, w_gz, gain, _ones_blockdiag(256, HD), wa, wb, wc, wo)


def _ffn_kernel(x_ref, mod_ref, g_ref, wg_ref, wu_ref, wd_ref, fin_ref, o_ref, *, final):
    m = mod_ref[0, 0]
    chunks = [slice(r, r + ROW_CHUNK) for r in range(0, x_ref.shape[0], ROW_CHUNK)]
    x = [x_ref[c, :] for c in chunks]
    h = [(_rms_rows(xc, g_ref[...]) * (1.0 + m[4:5]) + m[3:4]).astype(BF16) for xc in x]
    gate = [_dot(hc, wg_ref[0]) for hc in h]
    up = [_dot(hc, wu_ref[0]) for hc in h]
    act = [(_silu(gc) * uc).astype(BF16) for gc, uc in zip(gate, up)]
    down = [_dot(ac, wd_ref[0]) for ac in act]
    for c, xc, dc in zip(chunks, x, down):
        out = xc + m[5:6] * dc
        if final:
            out = _rms_rows(out, fin_ref[...])
        o_ref[c, :] = out


def _ffn(x, mod, layer, gain, wg, wu, wd, fin, mod_row, final, row0=0, n_rows=None):
    m_rows = x.shape[0] if n_rows is None else n_rows
    tm = TM_DENSE
    b0 = row0 // tm
    full = pl.BlockSpec((tm, D_MODEL), lambda i: (i, 0))
    vec = pl.BlockSpec((1, D_MODEL), lambda i: (0, 0))
    once = pl.Buffered(1)
    return pl.pallas_call(
        functools.partial(_ffn_kernel, final=final),
        grid=(m_rows // tm,),
        in_specs=[pl.BlockSpec((tm, D_MODEL), lambda i: (b0 + i, 0)),
                  pl.BlockSpec((1, 1, N_MOD, D_MODEL), lambda i: (layer, mod_row(b0 + i, tm), 0, 0)),
                  vec,
                  pl.BlockSpec((1, D_MODEL, D_FF), lambda i: (layer, 0, 0), pipeline_mode=once),
                  pl.BlockSpec((1, D_MODEL, D_FF), lambda i: (layer, 0, 0), pipeline_mode=once),
                  pl.BlockSpec((1, D_FF, D_MODEL), lambda i: (layer, 0, 0), pipeline_mode=once),
                  vec],
        out_specs=full,
        out_shape=jax.ShapeDtypeStruct((m_rows, D_MODEL), F32),
        compiler_params=_cparams(("parallel",)),
        name="ffn",
    )(x, mod, gain, wg, wu, wd, fin)


W_TOK_PARTS = (((O_QKV, O_Z), C_QKV), ((O_CQ, O_QKV), C_B), ((O_A, O_GATES), C_B + B_LOGITS + LOGIT_A),
               ((O_KA, O_CQ), C_KA), ((O_QA, O_KA), C_QA))
W_GZ_PARTS = (((O_GATES, W_IN_COLS), 0), ((O_Z, O_A), 3 * D_MODEL))
W_ROWS = 256


def _layout_w_in_kernel(w_ref, tok_ref, gz_ref):
    w = w_ref[0]
    for (a, b), dst in W_TOK_PARTS:
        tok_ref[0, :, dst:dst + b - a] = w[:, a:b].astype(BF16)
    pad0, pad1 = C_B + B_LOGITS + LOGIT_END, C_B + B_LOGITS + LANES
    tok_ref[0, :, pad0:pad1] = jnp.zeros((w.shape[0], pad1 - pad0), BF16)
    for (a, b), dst in W_GZ_PARTS:
        gz_ref[0, :, dst:dst + b - a] = w[:, a:b].astype(BF16)


def _layout_w_in(w_in):
    n_layers, d, n = w_in.shape
    return pl.pallas_call(
        _layout_w_in_kernel,
        grid=(n_layers, d // W_ROWS),
        in_specs=[pl.BlockSpec((1, W_ROWS, n), lambda l, i: (l, i, 0))],
        out_specs=[pl.BlockSpec((1, W_ROWS, N_IN), lambda l, i: (l, i, 0)),
                   pl.BlockSpec((1, W_ROWS, N_GZ), lambda l, i: (l, i, 0))],
        out_shape=[jax.ShapeDtypeStruct((n_layers, d, N_IN), BF16),
                   jax.ShapeDtypeStruct((n_layers, d, N_GZ), BF16)],
        compiler_params=_cparams(("parallel", "parallel")),
        name="layout_w_in",
    )(w_in)


def _layout_w_uq(w_uq):
    n_layers = w_uq.shape[0]
    w = w_uq.reshape(n_layers, Q_LORA, HEADS, NOPE_B + ROPE_B)
    w = jnp.pad(w, ((0, 0), (0, 0), (0, 0), (0, LANES - NOPE_B - ROPE_B)))
    return w.reshape(n_layers, Q_LORA, HEADS * LANES).astype(BF16)


def _layout_w_ukv(w_ukv):
    n_layers = w_ukv.shape[0]
    w = w_ukv.reshape(n_layers, KV_LORA, HEADS, 2, HD)
    zero = jnp.zeros((n_layers, KV_LORA, HEADS, HD), w.dtype)
    wk_top = jnp.concatenate([w[:, :, :, 0], zero], axis=-1).reshape(n_layers, KV_LORA, HEADS * LANES)
    place = np.zeros((LANES, HEADS, LANES), np.float32)
    for d in range(ROPE_B):
        place[d, :, NOPE_B + d] = 1.0
    place = jnp.broadcast_to(jnp.asarray(place.reshape(LANES, HEADS * LANES)), (n_layers, LANES, HEADS * LANES))
    wk = jnp.concatenate([wk_top, place], axis=1).astype(BF16)
    wv = w[:, :, :, 1].reshape(n_layers, KV_LORA, HEADS * HD).astype(BF16)
    return wk, wv


def kernel(x_prompt, x_sample, cache_ka, cache_va, cache_ckv, cache_kpe, state_fwd, state_bwd, c, c_ctx,
           w_mod, b_mod, norm1, norm2, w_in, a_qnorm, a_knorm, b_qnorm, b_kvnorm, w_uq, w_ukv, c_conv,
           c_alog, c_dt_bias, c_onorm, w_pa, w_pb, w_pc, w_out, w_gate, w_up, w_down, final_norm):
    n_ctx, s_ctx, _ = x_prompt.shape
    n_lat, s_lat, _ = x_sample.shape
    n_layers = w_mod.shape[0]
    past = cache_ka.shape[2]
    ctx_rows = n_ctx * s_ctx
    lat_rows = n_lat * s_lat
    assert s_ctx == TOK and s_lat % TOK == 0 and ctx_rows % 512 == 0 and s_lat % 512 == 0
    assert past % TOK == 0 and n_lat + 1 <= 8 and ctx_rows % s_lat == 0
    assert w_in.shape[2] == W_IN_COLS and N_IN % LANES == 0 and C_B % (B_LOGITS + LANES) == 0

    def mod_row(i, tm):
        return jnp.where(i < ctx_rows // tm, 0, 1 + (i - ctx_rows // tm) // (s_lat // tm))

    def tab_row(i):
        nb = ctx_rows // TOK_AB
        return jnp.where(i < nb, 0, 1 + (i - nb) % (s_lat // TOK_AB))

    x = (x_prompt.reshape(ctx_rows, D_MODEL), x_sample.reshape(lat_rows, D_MODEL))
    cvec = jnp.zeros((8, D_MODEL), F32).at[0].set(c_ctx).at[1:1 + n_lat].set(c)
    mod = _modulation(cvec, w_mod, b_mod).reshape(n_layers, 8, N_MOD, D_MODEL)
    cos, sin = _rope_tables(s_lat)

    w_in_p, w_gz = _layout_w_in(w_in)
    w_uq_p = _layout_w_uq(w_uq)
    w_k_p, w_v_p = _layout_w_ukv(w_ukv)
    bf = lambda w: w.astype(BF16)
    w_pa_b, w_pb_b, w_pc_b, w_out_b = bf(w_pa), bf(w_pb), bf(w_pc), bf(w_out)
    w_gate_b, w_up_b, w_down_b = bf(w_gate), bf(w_up), bf(w_down)

    cache_rows = n_lat * n_layers * past
    ck4, cv4 = _cache_a(cache_ka.reshape(cache_rows, LANES), cache_va.reshape(cache_rows, LANES))
    kpe_pad = jnp.pad(cache_kpe.reshape(cache_rows, ROPE_B), ((0, 0), (0, LANES - ROPE_B)))
    ckb, cvb = _cache_b(cache_ckv.reshape(cache_rows, KV_LORA), kpe_pad, w_k_p, w_v_p,
                        lambda i: (i // (past // TOK)) % n_layers)

    lane_vec = lambda v: jnp.zeros((1, LANES), F32).at[0, LOGIT_A:LOGIT_B].set(v.reshape(N_AB))
    new = []
    for l in range(n_layers):
        y, va_raw, qa, k4, v4, kn_a, qb, kb, vb, ckvn = _token_stage(
            x, mod, l, norm1[l][None], w_in_p, cos, sin,
            jnp.tile(a_qnorm[l], HEADS)[None], jnp.tile(a_knorm[l], KV_HEADS_A)[None],
            b_qnorm[l][None], b_kvnorm[l][None], w_uq_p, w_k_p, w_v_p, mod_row, tab_row, ctx_rows)
        conv_args = (c_conv[l, :, 0, :], lane_vec(c_alog[l]), lane_vec(c_dt_bias[l]))
        gdn_ctx = _prep_c(y, *conv_args, 0, ctx_rows, s_ctx)
        gdn_lat = _prep_c(y, *conv_args, ctx_rows, lat_rows, s_lat)

        cache_blk = (past, lambda b, l=l: b * n_layers + l)
        ya_c = _attention(qa, k4, v4, None, GROUPS_A, 0, n_ctx, s_ctx, s_ctx, None)
        ya_l = _attention(qa, k4, v4, (ck4, cv4), GROUPS_A, ctx_rows, n_lat, s_lat, 256, cache_blk)
        yb_c = _attention(qb, kb, vb, None, GROUPS_B, 0, n_ctx, s_ctx, s_ctx, None)
        yb_l = _attention(qb, kb, vb, (ckb, cvb), GROUPS_B, ctx_rows, n_lat, s_lat, 256, cache_blk)

        of_c, ob_c, sf_c, sb_c = _gdn_scan(*gdn_ctx, None, n_ctx, s_ctx)
        s0 = (_state_to_pairs(state_fwd[:, l]), _state_to_pairs(state_bwd[:, l]))
        of_l, ob_l, _, _ = _gdn_scan(*gdn_lat, s0, n_lat, s_lat)

        x_mid = _merge(x, mod, l, (ya_c, ya_l, yb_c, yb_l, of_c, of_l, ob_c, ob_l), norm1[l][None], w_gz,
                       jnp.tile(c_onorm[l], HEADS)[None], w_pa_b, w_pb_b, w_pc_b, w_out_b, mod_row, ctx_rows)
        ffn_args = (mod, l, norm2[l][None], w_gate_b, w_up_b, w_down_b, final_norm[None], mod_row)
        if l < n_layers - 1:
            x = (_ffn(x_mid, *ffn_args, final=False),)
        else:
            y_prompt = _ffn(x_mid, *ffn_args, final=True, row0=0, n_rows=ctx_rows)
            y_sample = _ffn(x_mid, *ffn_args, final=True, row0=ctx_rows, n_rows=lat_rows)

        new.append((kn_a[:ctx_rows].reshape(n_ctx, s_ctx, KV_HEADS_A, HD),
                    va_raw[:ctx_rows].reshape(n_ctx, s_ctx, KV_HEADS_A, HD),
                    ckvn[:ctx_rows].reshape(n_ctx, s_ctx, KV_LORA),
                    y[:ctx_rows, CONV_CH:CONV_CH + ROPE_B].reshape(n_ctx, s_ctx, ROPE_B),
                    _pairs_to_state(sf_c), _pairs_to_state(sb_c)))

    stacked = [jnp.stack([n[i] for n in new], axis=1) for i in range(6)]
    return (y_prompt.reshape(n_ctx, s_ctx, D_MODEL), y_sample.reshape(n_lat, s_lat, D_MODEL)) + tuple(stacked)
```

```python
import functools

import numpy as np
import jax
import jax.numpy as jnp
from jax import lax
from jax.experimental import pallas as pl
from jax.experimental.pallas import tpu as pltpu

F32 = jnp.float32
BF16 = jnp.bfloat16

D_MODEL = 1024
GRID_W = 64
GRID_W_LOG2 = 6
assert 1 << GRID_W_LOG2 == GRID_W
ROPE_THETA = 10000.0
EPS = 1e-6
N_MOD = 6
HEADS = 8
KV_HEADS_A = 2
HD = 64
Q_LORA = 384
KV_LORA = 256
NOPE_B = 64
ROPE_B = 32
CHUNK = 64
CONV_CH = 3 * HEADS * HD
D_FF = 2816
LANES = 128

N_AB = 2 * HEADS
O_QA = 0
O_KA = O_QA + HEADS * HD
O_VA = O_KA + KV_HEADS_A * HD
O_CQ = O_VA + KV_HEADS_A * HD
O_CKV = O_CQ + Q_LORA
O_KPE = O_CKV + KV_LORA
O_QKV = O_KPE + ROPE_B
O_Z = O_QKV + CONV_CH
O_A = O_Z + HEADS * HD
O_B = O_A + N_AB
O_GATES = O_B + N_AB
W_IN_COLS = O_GATES + 3 * D_MODEL

C_QKV = 0
C_B = C_QKV + CONV_CH
B_LOGITS = Q_LORA + KV_LORA
LOGIT_A = ROPE_B
LOGIT_B = LOGIT_A + N_AB
LOGIT_END = LOGIT_B + N_AB
C_KA = C_B + B_LOGITS + LANES
C_VA = C_KA + KV_HEADS_A * HD
C_QA = C_VA + KV_HEADS_A * HD
N_IN = C_QA + HEADS * HD
N_GZ = 3 * D_MODEL + HEADS * HD

LOG2E = 1.4426950408889634
TOK = 256
TOK_AB = 512
TM_DENSE = 512
ROW_CHUNK = 256
VMEM_LIMIT = 56 * 1024 * 1024


def _cparams(sem):
    return pltpu.CompilerParams(dimension_semantics=sem, vmem_limit_bytes=VMEM_LIMIT)


def _dot(a, b):
    return jnp.dot(a, b, preferred_element_type=F32)


def _dot_nt(a, b):
    return lax.dot_general(a, b, (((1,), (1,)), ((), ())), preferred_element_type=F32)


def _dot_tn(a, b):
    return lax.dot_general(a, b, (((0,), (0,)), ((), ())), preferred_element_type=F32)


def _split3(x):
    hi = x.astype(BF16)
    r = x - hi.astype(F32)
    mid = r.astype(BF16)
    lo = (r - mid.astype(F32)).astype(BF16)
    return hi, mid, lo


def _dot3(x, w):
    hi, mid, lo = _split3(x)
    return _dot(hi, w) + _dot(mid, w) + _dot(lo, w)


def _group_sums(x, ones):
    w = min(x.shape[1], ones.shape[0])
    blk = ones[:w, :w]
    slabs = []
    for s in range(x.shape[1] // w):
        xs = x[:, s * w:(s + 1) * w]
        hi = xs.astype(BF16)
        lo = (xs - hi.astype(F32)).astype(BF16)
        slabs.append(_dot(hi, blk) + _dot(lo, blk))
    return slabs[0] if len(slabs) == 1 else jnp.concatenate(slabs, axis=1)


def _silu(x):
    return x * jax.nn.sigmoid(x)


def _rms_rows(x, g):
    return x * lax.rsqrt(jnp.mean(x * x, axis=-1, keepdims=True) + EPS) * g


def _swap_pairs(x, half):
    lane = lax.broadcasted_iota(jnp.int32, x.shape, 1)
    n = x.shape[1]
    return jnp.where((lane & half) == 0, pltpu.roll(x, n - half, 1), pltpu.roll(x, half, 1))


def _rope_cols(x, cos, sin, half):
    outs = []
    for s in range(x.shape[1] // LANES):
        xs = x[:, s * LANES:(s + 1) * LANES]
        outs.append(xs * cos + _swap_pairs(xs, half) * sin)
    return outs[0] if len(outs) == 1 else jnp.concatenate(outs, axis=1)


def _mod_kernel(c_ref, w_ref, b_ref, o_ref):
    s = _silu(c_ref[...]).astype(BF16)
    o_ref[0] = _dot(s, w_ref[0].astype(BF16)) + b_ref[0]


def _modulation(cvec, w_mod, b_mod):
    n_layers = w_mod.shape[0]
    n_out = w_mod.shape[2]
    tn = 1536
    return pl.pallas_call(
        _mod_kernel,
        grid=(n_layers, n_out // tn),
        in_specs=[pl.BlockSpec((8, D_MODEL), lambda l, j: (0, 0)),
                  pl.BlockSpec((1, D_MODEL, tn), lambda l, j: (l, 0, j)),
                  pl.BlockSpec((1, 1, tn), lambda l, j: (l, 0, j))],
        out_specs=pl.BlockSpec((1, 8, tn), lambda l, j: (l, 0, j)),
        out_shape=jax.ShapeDtypeStruct((n_layers, 8, n_out), F32),
        compiler_params=_cparams(("parallel", "parallel")),
        name="modulation",
    )(cvec, w_mod, b_mod.reshape(n_layers, 1, n_out))


def _x_rows(x_refs, ctx_blocks, rows):
    if len(x_refs) == 1:
        return x_refs[0][rows, :]
    return jnp.where(pl.program_id(0) < ctx_blocks, x_refs[0][rows, :], x_refs[1][rows, :])


def _x_specs(x_parts, tm, ctx_rows):
    if len(x_parts) == 1:
        return [pl.BlockSpec((tm, D_MODEL), lambda i: (i, 0))]
    nb_ctx, nb_lat = ctx_rows // tm, x_parts[1].shape[0] // tm
    return [pl.BlockSpec((tm, D_MODEL), lambda i: (jnp.minimum(i, nb_ctx - 1), 0)),
            pl.BlockSpec((tm, D_MODEL), lambda i: (jnp.clip(i - nb_ctx, 0, nb_lat - 1), 0))]


def _token_kernel(*refs, n_x, ctx_blocks):
    x_refs = refs[:n_x]
    (mod_ref, g_ref, w_ref, cos_a, sin_a, cos_q, sin_q, cos_k, sin_k, gq_a, gk_a, ones_ref, place_ref,
     gq_b, gkv_b, wq_ref, wk_ref, wv_ref,
     yc_ref, va_ref, qa_ref, k4_ref, vta_ref, kna_ref, qb_ref, kb_ref, vtb_ref, ckv_ref, y_scr) = refs[n_x:]
    m = mod_ref[0, 0]
    for r in range(0, y_scr.shape[0], ROW_CHUNK):
        rows = slice(r, r + ROW_CHUNK)
        y = _rms_rows(_x_rows(x_refs, ctx_blocks, rows), g_ref[...])
        h = (y * (1.0 + m[1:2]) + m[0:1]).astype(BF16)
        y_scr[rows, :] = _dot(h, w_ref[0])
    yc_ref[:, :CONV_CH] = y_scr[:, C_QKV:C_QKV + CONV_CH]
    yc_ref[:, CONV_CH:] = y_scr[:, C_B + B_LOGITS:C_B + B_LOGITS + LANES]
    va_ref[...] = y_scr[:, C_VA:C_VA + LANES]
    _prep_a_kernel(y_scr.at[:, C_QA:C_QA + HEADS * HD], y_scr.at[:, C_KA:C_KA + 2 * LANES],
                   cos_a, sin_a, gq_a, gk_a, ones_ref, place_ref, qa_ref, k4_ref, vta_ref, kna_ref)
    _prep_b_kernel(y_scr.at[:, C_B:C_B + B_LOGITS + LANES], cos_q, sin_q, cos_k, sin_k, gq_b, gkv_b,
                   wq_ref, wk_ref, wv_ref, qb_ref, kb_ref, vtb_ref, ckv_ref)


def _token_stage(x_parts, mod, layer, norm_gain, w, cos, sin, gq_a, gk_a, gq_b, gkv_b, wq, wk, wv,
                 mod_row, tab_row, ctx_rows):
    m_rows = sum(p.shape[0] for p in x_parts)
    tm = TOK_AB
    rows = lambda w_: pl.BlockSpec((tm, w_), lambda i: (i, 0))
    cols = lambda h_: pl.BlockSpec((h_, tm), lambda i: (0, i))
    vec = lambda w_: pl.BlockSpec((1, w_), lambda i: (0, 0))
    tab = lambda a: pl.BlockSpec((1, tm, LANES), lambda i: (a, tab_row(i), 0))
    lw = lambda k, n: pl.BlockSpec((1, k, n), lambda i: (layer, 0, 0))
    f32 = lambda *shape: jax.ShapeDtypeStruct(shape, F32)
    bf16 = lambda *shape: jax.ShapeDtypeStruct(shape, BF16)
    return pl.pallas_call(
        functools.partial(_token_kernel, n_x=len(x_parts), ctx_blocks=ctx_rows // tm),
        grid=(m_rows // tm,),
        in_specs=_x_specs(x_parts, tm, ctx_rows)
        + [pl.BlockSpec((1, 1, N_MOD, D_MODEL), lambda i: (layer, mod_row(i, tm), 0, 0)),
           vec(D_MODEL), lw(D_MODEL, N_IN),
           tab(0), tab(0), tab(1), tab(1), tab(2), tab(2),
           vec(HEADS * HD), vec(LANES),
           pl.BlockSpec((256, 256), lambda i: (0, 0)), pl.BlockSpec((LANES, 4 * LANES), lambda i: (0, 0)),
           vec(Q_LORA), vec(KV_LORA),
           lw(Q_LORA, HEADS * LANES), lw(KV_LORA + LANES, HEADS * LANES), lw(KV_LORA, HEADS * HD)],
        out_specs=[rows(CONV_CH + LANES), rows(LANES),
                   rows(4 * LANES), rows(4 * LANES), cols(LANES), rows(LANES),
                   rows(HEADS * LANES), rows(HEADS * LANES), cols(HEADS * HD), rows(KV_LORA)],
        out_shape=[f32(m_rows, CONV_CH + LANES), f32(m_rows, LANES),
                   bf16(m_rows, 4 * LANES), bf16(m_rows, 4 * LANES), bf16(LANES, m_rows), f32(m_rows, LANES),
                   bf16(m_rows, HEADS * LANES), bf16(m_rows, HEADS * LANES), bf16(HEADS * HD, m_rows),
                   f32(m_rows, KV_LORA)],
        scratch_shapes=[pltpu.VMEM((tm, N_IN), F32)],
        compiler_params=_cparams(("parallel",)),
        name="token_stage",
    )(*x_parts, mod, norm_gain, w, cos, sin, cos, sin, cos, sin, gq_a, gk_a, _ones_blockdiag(256, HD),
      _place4(), gq_b, gkv_b, wq, wk, wv)


def _rope_table_kernel(freq_ref, col_ref, sign_ref, cos_ref, sin_ref):
    shape = cos_ref.shape
    pos = pl.program_id(1) * TOK_AB + lax.broadcasted_iota(jnp.int32, shape, 1) - TOK_AB
    grid_row = (pos >> GRID_W_LOG2).astype(F32)
    grid_col = (pos & (GRID_W - 1)).astype(F32)
    a = jnp.where(col_ref[...] > 0.5, grid_col, grid_row) * freq_ref[...]
    a = jnp.where(pos >= 0, a, 0.0)
    cos_ref[...] = jnp.cos(a)
    sin_ref[...] = jnp.sin(a) * sign_ref[...]


def _rope_tables(n_lat):
    lane = np.arange(LANES)

    def lane_pattern(nf, d, active):
        inv_freq = np.float32(ROPE_THETA) ** (-np.arange(nf, dtype=np.float32) / np.float32(nf))
        dd = np.where(active, d, 0)
        freq = np.where(active, inv_freq[dd % nf], 0.0).astype(np.float32)
        use_col = (dd >= 2 * nf).astype(np.float32)
        sign = np.where((dd % (2 * nf)) < nf, -1.0, 1.0).astype(np.float32)
        return freq, use_col, sign

    d_q = lane - NOPE_B
    patterns = [lane_pattern(HD // 4, lane % HD, np.ones(LANES, bool)),
                lane_pattern(ROPE_B // 4, d_q, (d_q >= 0) & (d_q < ROPE_B)),
                lane_pattern(ROPE_B // 4, lane, lane < ROPE_B)]
    freq, use_col, sign = [jnp.asarray(np.stack([p[t] for p in patterns]))[:, None, :] for t in range(3)]
    rows = TOK_AB + n_lat
    spec = pl.BlockSpec((1, TOK_AB, LANES), lambda a, i: (a, i, 0))
    vec = pl.BlockSpec((1, 1, LANES), lambda a, i: (a, 0, 0))
    return pl.pallas_call(
        _rope_table_kernel,
        grid=(3, rows // TOK_AB),
        in_specs=[vec, vec, vec],
        out_specs=[spec, spec],
        out_shape=[jax.ShapeDtypeStruct((3, rows, LANES), F32)] * 2,
        compiler_params=_cparams(("parallel", "parallel")),
        name="rope_tables",
    )(freq, use_col, sign)


def _prep_a_kernel(qa_ref, kv_ref, cos_ref, sin_ref, gq_ref, gk_ref, ones_ref, place_ref,
                   q_ref, k4_ref, vt_ref, kn_ref):
    cos = cos_ref[0]
    sin = sin_ref[0]
    ones = ones_ref[...]
    q = qa_ref[...]
    qn = q * lax.rsqrt(_group_sums(q * q, ones) * (1.0 / HD) + EPS) * gq_ref[...]
    q_ref[...] = (_rope_cols(qn, cos, sin, HD // 4) * (HD ** -0.5 * LOG2E)).astype(BF16)
    k = kv_ref[:, :LANES]
    kn = k * lax.rsqrt(_group_sums(k * k, ones) * (1.0 / HD) + EPS) * gk_ref[...]
    kn_ref[...] = kn
    k4_ref[...] = _dot(_rope_cols(kn, cos, sin, HD // 4).astype(BF16), place_ref[...]).astype(BF16)
    vt_ref[...] = kv_ref[:, LANES:].T.astype(BF16)


def _place4():
    p = np.zeros((LANES, 4 * LANES), np.float32)
    for kv in range(KV_HEADS_A):
        for side in range(2):
            for d in range(HD):
                p[kv * HD + d, (2 * kv + side) * LANES + side * HD + d] = 1.0
    return jnp.asarray(p, BF16)


def _ones_blockdiag(n, group):
    idx = np.arange(n) // group
    return jnp.asarray((idx[:, None] == idx[None, :]).astype(np.float32), BF16)


def _cache_a_kernel(k_ref, v_ref, place_ref, k4_ref, vt_ref):
    k4_ref[...] = _dot(k_ref[...].astype(BF16), place_ref[...]).astype(BF16)
    vt_ref[...] = v_ref[...].T.astype(BF16)


def _cache_a(ck, cv):
    rows = ck.shape[0]
    narrow = pl.BlockSpec((TOK, LANES), lambda i: (i, 0))
    return pl.pallas_call(
        _cache_a_kernel,
        grid=(rows // TOK,),
        in_specs=[narrow, narrow, pl.BlockSpec((LANES, 512), lambda i: (0, 0))],
        out_specs=[pl.BlockSpec((TOK, 512), lambda i: (i, 0)), pl.BlockSpec((LANES, TOK), lambda i: (0, i))],
        out_shape=[jax.ShapeDtypeStruct((rows, 512), BF16), jax.ShapeDtypeStruct((LANES, rows), BF16)],
        compiler_params=_cparams(("parallel",)),
        name="cache_a",
    )(ck, cv, _place4())


def _prep_b_kernel(g_ref, cq_ref, sq_ref, ck_ref, sk_ref, gq_ref, gkv_ref, wq_ref, wk_ref, wv_ref,
                   q_ref, k_ref, vt_ref, ckv_ref):
    g = g_ref[...]
    cqn = _rms_rows(g[:, :Q_LORA], gq_ref[...])
    q = _dot(cqn.astype(BF16), wq_ref[0])
    q = _rope_cols(q, cq_ref[0], sq_ref[0], ROPE_B // 4) * ((NOPE_B + ROPE_B) ** -0.5 * LOG2E)
    q_ref[...] = q.astype(BF16)
    ckvn = _rms_rows(g[:, Q_LORA:Q_LORA + KV_LORA], gkv_ref[...])
    ckv_ref[...] = ckvn
    kpe = _rope_cols(g[:, B_LOGITS:], ck_ref[0], sk_ref[0], ROPE_B // 4)
    ckv_b = ckvn.astype(BF16)
    kin = jnp.concatenate([ckv_b, kpe.astype(BF16)], axis=1)
    k_ref[...] = _dot(kin, wk_ref[0]).astype(BF16)
    vt_ref[...] = _dot(ckv_b, wv_ref[0]).T.astype(BF16)


def _cache_b_kernel(c_ref, p_ref, wk_ref, wv_ref, k_ref, vt_ref):
    ckv_b = c_ref[...].astype(BF16)
    kin = jnp.concatenate([ckv_b, p_ref[...].astype(BF16)], axis=1)
    k_ref[...] = _dot(kin, wk_ref[0]).astype(BF16)
    vt_ref[...] = _dot(ckv_b, wv_ref[0]).T.astype(BF16)


def _cache_b(ckv, kpe, wk, wv, layer_of_block):
    rows = ckv.shape[0]
    return pl.pallas_call(
        _cache_b_kernel,
        grid=(rows // TOK,),
        in_specs=[pl.BlockSpec((TOK, KV_LORA), lambda i: (i, 0)),
                  pl.BlockSpec((TOK, LANES), lambda i: (i, 0)),
                  pl.BlockSpec((1, KV_LORA + LANES, HEADS * LANES), lambda i: (layer_of_block(i), 0, 0)),
                  pl.BlockSpec((1, KV_LORA, HEADS * HD), lambda i: (layer_of_block(i), 0, 0))],
        out_specs=[pl.BlockSpec((TOK, HEADS * LANES), lambda i: (i, 0)),
                   pl.BlockSpec((HEADS * HD, TOK), lambda i: (0, i))],
        out_shape=[jax.ShapeDtypeStruct((rows, HEADS * LANES), BF16),
                   jax.ShapeDtypeStruct((HEADS * HD, rows), BF16)],
        compiler_params=_cparams(("parallel",)),
        name="cache_b",
    )(ckv, kpe, wk, wv)


ATT_KEY_TILE = 256
ATT_LOOKAHEAD = 5


def _attn_kernel(*refs, groups, has_cache):
    if has_cache:
        q_ref, kc_ref, vc_ref, kn_ref, vn_ref, o_ref = refs
        segments = [(kc_ref, vc_ref), (kn_ref, vn_ref)]
    else:
        q_ref, kn_ref, vn_ref, o_ref = refs
        segments = [(kn_ref, vn_ref)]
    tiles = [(k_ref, v_ref, t * ATT_KEY_TILE) for k_ref, v_ref in segments
             for t in range(k_ref.shape[0] // ATT_KEY_TILE)]
    q = [q_ref[:, qg * LANES:(qg + 1) * LANES] for qg, _, _ in groups]

    def scores(h, i):
        k_ref, _, t0 = tiles[i]
        kg = groups[h][1]
        return _dot_nt(k_ref[t0:t0 + ATT_KEY_TILE, kg * LANES:(kg + 1) * LANES], q[h])

    heads = range(len(groups))
    jobs = [(i, h) for i in range(len(tiles)) for h in heads]
    pending = {j: scores(h, i) for j, (i, h) in enumerate(jobs[:ATT_LOOKAHEAD])}
    ones_rows = (lax.broadcasted_iota(jnp.int32, (16, ATT_KEY_TILE), 0) == 0).astype(BF16)
    m, acc = {}, {}
    for j, (i, h) in enumerate(jobs):
        if j + ATT_LOOKAHEAD < len(jobs):
            i_n, h_n = jobs[j + ATT_LOOKAHEAD]
            pending[j + ATT_LOOKAHEAD] = scores(h_n, i_n)
        s = pending.pop(j)
        _, v_ref, t0 = tiles[i]
        vrow = groups[h][2]
        v_t = jnp.concatenate([v_ref[vrow:vrow + HD, t0:t0 + ATT_KEY_TILE], ones_rows], axis=0)
        m_tile = jnp.max(s, axis=0, keepdims=True)
        if i == 0:
            m[h] = m_tile
            acc[h] = _dot(v_t, jnp.exp2(s - m_tile).astype(BF16))
        else:
            m_new = jnp.maximum(m[h], m_tile)
            alpha = jnp.exp2(m[h] - m_new)
            acc[h] = acc[h] * alpha + _dot(v_t, jnp.exp2(s - m_new).astype(BF16))
            m[h] = m_new
    outs = [acc[h][:HD] / acc[h][HD:HD + 1] for h in heads]
    o_ref[...] = jnp.concatenate(outs, axis=0).T.astype(BF16)


def _attention(q, kn, vn, cache, groups, row0, n_seq, seq, tq, cache_block):
    qw, kw, vr = q.shape[1], kn.shape[1], vn.shape[0]
    nq = seq // tq
    qb0, sb0 = row0 // tq, row0 // seq
    in_specs = [pl.BlockSpec((tq, qw), lambda b, i: (qb0 + b * nq + i, 0))]
    args = [q]
    if cache is not None:
        past = cache_block[0]
        in_specs += [pl.BlockSpec((past, kw), lambda b, i: (cache_block[1](b), 0)),
                     pl.BlockSpec((vr, past), lambda b, i: (0, cache_block[1](b)))]
        args += list(cache)
    in_specs += [pl.BlockSpec((seq, kw), lambda b, i: (sb0 + b, 0)),
                 pl.BlockSpec((vr, seq), lambda b, i: (0, sb0 + b))]
    args += [kn, vn]
    return pl.pallas_call(
        functools.partial(_attn_kernel, groups=groups, has_cache=cache is not None),
        grid=(n_seq, nq),
        in_specs=in_specs,
        out_specs=pl.BlockSpec((tq, 512), lambda b, i: (b * nq + i, 0)),
        out_shape=jax.ShapeDtypeStruct((n_seq * seq, 512), BF16),
        compiler_params=_cparams(("parallel", "parallel")),
        name="attention",
    )(*args)


GROUPS_A = tuple((h // 2, 2 * (h // 4) + (h % 2), (h // 4) * HD) for h in range(HEADS))
GROUPS_B = tuple((h, h, h * HD) for h in range(HEADS))


def _prep_c_kernel(x_ref, prev_ref, next_ref, ab_ref, w_ref, alog_ref, dtb_ref, ones_ref, exp_ref, tri_ref,
                   q_ref, k_ref, v_ref, gf_ref, gb_ref, bf_ref, bb_ref, *, blocks_per_seq):
    j = pl.program_id(0) % blocks_per_seq
    is_first, is_last = j == 0, j == blocks_per_seq - 1
    x = x_ref[...]
    rows = x.shape[0]
    rid = lax.broadcasted_iota(jnp.int32, x.shape, 0)
    halo_p = jnp.where(is_first, 0.0, prev_ref[7:8, :])
    halo_n = jnp.where(is_last, 0.0, next_ref[0:1, :])
    x_prev = jnp.where(rid == 0, halo_p, pltpu.roll(x, 1, 0))
    x_next = jnp.where(rid == rows - 1, halo_n, pltpu.roll(x, rows - 1, 0))
    w = w_ref[...]
    s = _silu(x_prev * w[0:1] + x * w[1:2] + x_next * w[2:3])
    ones = ones_ref[...]
    hw = HEADS * HD
    q, k = s[:, :hw], s[:, hw:2 * hw]
    q_ref[...] = q * lax.rsqrt(_group_sums(q * q, ones) + EPS) * (HD ** -0.5)
    k_ref[...] = k * lax.rsqrt(_group_sums(k * k, ones) + EPS)
    v_ref[...] = s[:, 2 * hw:]
    ab = ab_ref[...]
    t = ab + dtb_ref[...]
    softplus = jnp.maximum(t, 0.0) + jnp.log1p(jnp.exp(-jnp.abs(t)))
    g = -jnp.exp(alog_ref[...]) * softplus
    lane = lax.broadcasted_iota(jnp.int32, ab.shape, 1)
    parts = _split3(g)
    cum_f = sum(_dot(tri_ref[0], part) for part in parts)
    cum_b = sum(_dot(tri_ref[1], part) for part in parts)
    both = jnp.where(lane < LOGIT_A + HEADS, cum_f, jnp.where(lane < LOGIT_B, cum_b, jax.nn.sigmoid(ab)))
    e = _dot3(both, exp_ref[...])
    gf_ref[...] = e[:, 0 * hw:1 * hw]
    gb_ref[...] = e[:, 1 * hw:2 * hw]
    bf_ref[...] = e[:, 2 * hw:3 * hw]
    bb_ref[...] = e[:, 3 * hw:4 * hw]


def _expand_matrix():
    e = np.zeros((LANES, 4 * HEADS * HD), np.float32)
    for kind in range(2):
        for d in range(2):
            for h in range(HEADS):
                src = LOGIT_A + N_AB * kind + HEADS * d + h
                dst = (2 * kind + d) * HEADS * HD + h * HD
                e[src, dst:dst + HD] = 1.0
    return jnp.asarray(e, BF16)


def _chunk_tri():
    r = np.arange(TOK)
    same = (r[:, None] // CHUNK) == (r[None, :] // CHUNK)
    low = same & (r[:, None] >= r[None, :])
    up = same & (r[:, None] <= r[None, :])
    return jnp.asarray(np.stack([low, up]).astype(np.float32), BF16)


def _prep_c(y, w_conv, alog, dtb, row0, n_rows, seq):
    nb8 = TOK // 8
    b0 = row0 // TOK
    last8 = (row0 + n_rows) // 8 - 1
    qkv_blk = 0
    out = pl.BlockSpec((TOK, HEADS * HD), lambda i: (i, 0))
    vec = pl.BlockSpec((1, LANES), lambda i: (0, 0))
    return pl.pallas_call(
        functools.partial(_prep_c_kernel, blocks_per_seq=seq // TOK),
        grid=(n_rows // TOK,),
        in_specs=[pl.BlockSpec((TOK, CONV_CH), lambda i: (b0 + i, qkv_blk)),
                  pl.BlockSpec((8, CONV_CH), lambda i: (jnp.maximum((b0 + i) * nb8 - 1, 0), qkv_blk)),
                  pl.BlockSpec((8, CONV_CH), lambda i: (jnp.minimum((b0 + i + 1) * nb8, last8), qkv_blk)),
                  pl.BlockSpec((TOK, LANES), lambda i: (b0 + i, CONV_CH // LANES)),
                  pl.BlockSpec((3, CONV_CH), lambda i: (0, 0)),
                  vec, vec,
                  pl.BlockSpec((256, 256), lambda i: (0, 0)),
                  pl.BlockSpec((LANES, 4 * HEADS * HD), lambda i: (0, 0)),
                  pl.BlockSpec((2, TOK, TOK), lambda i: (0, 0, 0))],
        out_specs=[out] * 7,
        out_shape=[jax.ShapeDtypeStruct((n_rows, HEADS * HD), F32)] * 7,
        compiler_params=_cparams(("parallel",)),
        name="prep_c",
    )(y, y, y, y, w_conv, alog, dtb, _ones_blockdiag(256, HD), _expand_matrix(), _chunk_tri())


def _gdn_kernel(*refs, has_s0, n_chunks, n_par):
    ins = refs[:10]
    pos = 10
    if has_s0:
        s0_refs = refs[pos:pos + 2]
        pos += 2
    o_refs = refs[pos:pos + 2]
    s_out_refs = refs[pos + 2:pos + 4]
    s_scr = refs[pos + 4]
    c = pl.program_id(1)

    @pl.when(c == 0)
    def _():
        for d in range(2):
            s_scr[d] = s0_refs[d][...] if has_s0 else jnp.zeros(s_scr.shape[1:], F32)

    n2 = 2 * CHUNK
    ri = lax.broadcasted_iota(jnp.int32, (n2, n2), 0)
    ci = lax.broadcasted_iota(jnp.int32, (n2, n2), 1)
    same = (ri // CHUNK) == (ci // CHUNK)
    left = lax.broadcasted_iota(jnp.int32, (CHUNK, n2), 1) < CHUNK
    eye = (ri == ci).astype(F32)

    def level_mask(lvl, lower):
        bit = 1 << lvl
        hi, lo = (ri, ci) if lower else (ci, ri)
        return ((ri >> (lvl + 1)) == (ci >> (lvl + 1))) & ((hi & bit) != 0) & ((lo & bit) == 0)

    levels = [[level_mask(lvl, d == 0) for lvl in range(6)] for d in range(2)]

    def block_diag(x):
        return jnp.concatenate([jnp.where(left, x, 0.0), jnp.where(left, 0.0, x)], axis=0)

    def anti_diag(x):
        xr = pltpu.roll(x, CHUNK, 1)
        return jnp.concatenate([jnp.where(left, 0.0, xr), jnp.where(left, xr, 0.0)], axis=0)

    chains = [(d, n, p) for n in range(n_par) for d in range(2) for p in range(HEADS // 2)]
    incl = [same & ((ri >= ci) if d == 0 else (ri <= ci)) for d in range(2)]
    strict = [same & ((ri > ci) if d == 0 else (ri < ci)) for d in range(2)]

    def load(which):
        return [ins[5 * d + which][n, 0, :, p * n2:(p + 1) * n2] for d, n, p in chains]

    q, k, v, gc, beta = load(0), load(1), load(2), load(3), load(4)
    gtot = [gi[CHUNK - 1:CHUNK, :] if d == 0 else gi[0:1, :] for (d, _, _), gi in zip(chains, gc)]
    eg = [jnp.exp(gi) for gi in gc]
    kb = [ki * bi for ki, bi in zip(k, beta)]
    decay = []
    for (d, _, _), gi in zip(chains, gc):
        gcr = pltpu.roll(gi, CHUNK, 1)
        gcol = jnp.concatenate([jnp.where(left, gi, gcr), jnp.where(left, gcr, gi)], axis=0)
        diff = gcol - gcol.T
        decay.append(jnp.where(incl[d], jnp.exp(jnp.where(incl[d], diff, 0.0)), 0.0))
    k_bd = [block_diag(ki).astype(BF16) for ki in k]
    kk = [_dot_nt(block_diag(kbi).astype(BF16), kbd) for kbi, kbd in zip(kb, k_bd)]
    qk = [_dot_nt(block_diag(qi).astype(BF16), kbd) for qi, kbd in zip(q, k_bd)]
    a_mat = [jnp.where(strict[d], kki * di, 0.0) for (d, _, _), kki, di in zip(chains, kk, decay)]
    qk = [(qki * di).astype(BF16) for qki, di in zip(qk, decay)]
    x = []
    for vi, bi, kbi, egi in zip(v, beta, kb, eg):
        vb = vi * bi
        kbe_r = pltpu.roll(kbi * egi, CHUNK, 1)
        x.append(jnp.concatenate([jnp.where(left, vb, kbe_r), jnp.where(left, kbe_r, vb)], axis=0))
    inv = [eye - jnp.where(levels[d][0], ai, 0.0) for (d, _, _), ai in zip(chains, a_mat)]
    a_b = [ai.astype(BF16) for ai in a_mat]
    for lvl in range(1, 6):
        d_b = [di.astype(BF16) for di in inv]
        t = [_dot(ai, di).astype(BF16) for ai, di in zip(a_b, d_b)]
        inv = [di - jnp.where(levels[d][lvl], _dot(dbi, ti), 0.0)
               for (d, _, _), di, dbi, ti in zip(chains, inv, d_b, t)]
    x = [xi + _dot((di - eye).astype(BF16), xi.astype(BF16)) for xi, di in zip(x, inv)]
    s = [s_scr[d, n, p] for d, n, p in chains]
    s_b = [si.astype(BF16) for si in s]
    ws = [_dot(jnp.where(same, 0.0, xi).astype(BF16), sbi) for xi, sbi in zip(x, s_b)]
    v_new = [(jnp.where(same, xi, 0.0) - wi).astype(BF16) for xi, wi in zip(x, ws)]
    o_s = [_dot(anti_diag(qi * egi).astype(BF16), sbi) for qi, egi, sbi in zip(q, eg, s_b)]
    o_v = [_dot(qki, vni) for qki, vni in zip(qk, v_new)]
    kd = [anti_diag(ki * jnp.exp(gt - gi)).astype(BF16) for ki, gt, gi in zip(k, gtot, gc)]
    s_upd = [_dot_tn(kdi, vni) for kdi, vni in zip(kd, v_new)]
    for i, (d, n, p) in enumerate(chains):
        o = o_s[i] + o_v[i]
        o_refs[d][n, 0, :, p * n2:(p + 1) * n2] = o[:CHUNK] + o[CHUNK:]
        s_scr[d, n, p] = s[i] * jnp.exp(gtot[i]) + s_upd[i]

    @pl.when(c == n_chunks - 1)
    def _():
        for d in range(2):
            s_out_refs[d][...] = s_scr[d]


GDN_SEQS_PER_STEP = 2


def _gdn_scan(qn, kn, v, gcf, gcb, bf, bb, s0, n_seq, seq):
    nc = seq // CHUNK
    hw = HEADS * HD
    n2 = 2 * CHUNK
    n_par = max(p for p in range(1, GDN_SEQS_PER_STEP + 1) if n_seq % p == 0)
    view = lambda a: a.reshape(n_seq, nc, CHUNK, hw)
    fwd = pl.BlockSpec((n_par, 1, CHUNK, hw), lambda b, c: (b, c, 0, 0))
    bwd = pl.BlockSpec((n_par, 1, CHUNK, hw), lambda b, c: (b, nc - 1 - c, 0, 0))
    state = pl.BlockSpec((n_par, HEADS // 2, n2, n2), lambda b, c: (b, 0, 0, 0))
    in_specs = [fwd] * 5 + [bwd] * 5
    args = [view(a) for a in (qn, kn, v, gcf, bf, qn, kn, v, gcb, bb)]
    if s0 is not None:
        in_specs += [state, state]
        args += list(s0)
    st_shape = jax.ShapeDtypeStruct((n_seq, HEADS // 2, n2, n2), F32)
    o_shape = jax.ShapeDtypeStruct((n_seq, nc, CHUNK, hw), F32)
    o_f, o_b, s_f, s_b = pl.pallas_call(
        functools.partial(_gdn_kernel, has_s0=s0 is not None, n_chunks=nc, n_par=n_par),
        grid=(n_seq // n_par, nc),
        in_specs=in_specs,
        out_specs=[fwd, bwd, state, state],
        out_shape=[o_shape, o_shape, st_shape, st_shape],
        scratch_shapes=[pltpu.VMEM((2, n_par, HEADS // 2, n2, n2), F32)],
        compiler_params=_cparams(("parallel", "arbitrary")),
        name="gdn_scan",
    )(*args)
    return o_f.reshape(n_seq * seq, hw), o_b.reshape(n_seq * seq, hw), s_f, s_b


def _state_to_pairs(s):
    b = s.shape[0]
    s = s.reshape(b, HEADS // 2, 2, HD, HD)
    z = jnp.zeros_like(s[:, :, 0])
    top = jnp.concatenate([z, s[:, :, 1]], axis=-1)
    bot = jnp.concatenate([s[:, :, 0], z], axis=-1)
    return jnp.concatenate([top, bot], axis=-2)


def _pairs_to_state(s):
    b = s.shape[0]
    even = s[:, :, HD:, :HD]
    odd = s[:, :, :HD, HD:]
    return jnp.stack([even, odd], axis=2).reshape(b, HEADS, HD, HD)


def _merge_kernel(*refs, n_x, ctx_blocks):
    x_refs, mod_ref, refs = refs[:n_x], refs[n_x], refs[n_x + 1:]
    pairs, (n1_ref, wgz_ref, gain_ref, ones_ref, wa_ref, wb_ref, wc_ref, wo_ref, o_ref) = refs[:8], refs[8:]
    is_ctx = pl.program_id(0) < ctx_blocks
    ya, yb, o_f, o_b = [jnp.where(is_ctx, pairs[2 * t][...], pairs[2 * t + 1][...]) for t in range(4)]
    m = mod_ref[0, 0]
    chunks = [slice(r, r + ROW_CHUNK) for r in range(0, o_ref.shape[0], ROW_CHUNK)]
    pa = [_dot(ya[c], wa_ref[0]) for c in chunks]
    pb = [_dot(yb[c], wb_ref[0]) for c in chunks]
    x = [_x_rows(x_refs, ctx_blocks, c) for c in chunks]
    h = [(_rms_rows(xc, n1_ref[...]) * (1.0 + m[1:2]) + m[0:1]).astype(BF16) for xc in x]
    gz = [_dot(hc, wgz_ref[0]) for hc in h]
    yc = []
    for c, gzc in zip(chunks, gz):
        o = o_f[c] + o_b[c]
        ms = _group_sums(o * o, ones_ref[...]) * (1.0 / HD)
        yc.append((o * lax.rsqrt(ms + EPS) * gain_ref[...] * _silu(gzc[:, 3 * D_MODEL:])).astype(BF16))
    pc = [_dot(ycc, wc_ref[0]) for ycc in yc]
    merged = [(jax.nn.sigmoid(g[:, :D_MODEL]) * a + jax.nn.sigmoid(g[:, D_MODEL:2 * D_MODEL]) * b
               + jax.nn.sigmoid(g[:, 2 * D_MODEL:3 * D_MODEL]) * cc).astype(BF16)
              for g, a, b, cc in zip(gz, pa, pb, pc)]
    mix = [_dot(mc, wo_ref[0]) for mc in merged]
    for c, xc, mixc in zip(chunks, x, mix):
        o_ref[c, :] = xc + m[2:3] * mixc


def _merge(x_parts, mod, layer, mixers, norm_gain, w_gz, gain, wa, wb, wc, wo, mod_row, ctx_rows):
    m_rows = sum(p.shape[0] for p in x_parts)
    tm = TM_DENSE
    nb_ctx = ctx_rows // tm
    nb_lat = (m_rows - ctx_rows) // tm
    ctx_half = pl.BlockSpec((tm, 512), lambda i: (jnp.minimum(i, nb_ctx - 1), 0))
    lat_half = pl.BlockSpec((tm, 512), lambda i: (jnp.clip(i - nb_ctx, 0, nb_lat - 1), 0))
    full = pl.BlockSpec((tm, D_MODEL), lambda i: (i, 0))

    def wspec(k):
        return pl.BlockSpec((1, k, D_MODEL), lambda i: (layer, 0, 0))

    return pl.pallas_call(
        functools.partial(_merge_kernel, n_x=len(x_parts), ctx_blocks=nb_ctx),
        grid=(m_rows // tm,),
        in_specs=_x_specs(x_parts, tm, ctx_rows)
        + [pl.BlockSpec((1, 1, N_MOD, D_MODEL), lambda i: (layer, mod_row(i, tm), 0, 0))]
        + [ctx_half, lat_half] * 4
        + [pl.BlockSpec((1, D_MODEL), lambda i: (0, 0)),
           pl.BlockSpec((1, D_MODEL, N_GZ), lambda i: (layer, 0, 0)),
           pl.BlockSpec((1, 512), lambda i: (0, 0)),
           pl.BlockSpec((256, 256), lambda i: (0, 0)),
           wspec(512), wspec(512), wspec(512), wspec(D_MODEL)],
        out_specs=full,
        out_shape=jax.ShapeDtypeStruct((m_rows, D_MODEL), F32),
        compiler_params=_cparams(("parallel",)),
        name="merge_out",
    )(*x_parts, mod, *mixers, norm_gain, w_gz, gain, _ones_blockdiag(256, HD), wa, wb, wc, wo)


def _ffn_kernel(x_ref, mod_ref, g_ref, wg_ref, wu_ref, wd_ref, fin_ref, o_ref, *, final):
    m = mod_ref[0, 0]
    chunks = [slice(r, r + ROW_CHUNK) for r in range(0, x_ref.shape[0], ROW_CHUNK)]
    x = [x_ref[c, :] for c in chunks]
    h = [(_rms_rows(xc, g_ref[...]) * (1.0 + m[4:5]) + m[3:4]).astype(BF16) for xc in x]
    gate = [_dot(hc, wg_ref[0]) for hc in h]
    up = [_dot(hc, wu_ref[0]) for hc in h]
    act = [(_silu(gc) * uc).astype(BF16) for gc, uc in zip(gate, up)]
    down = [_dot(ac, wd_ref[0]) for ac in act]
    for c, xc, dc in zip(chunks, x, down):
        out = xc + m[5:6] * dc
        if final:
            out = _rms_rows(out, fin_ref[...])
        o_ref[c, :] = out


def _ffn(x, mod, layer, gain, wg, wu, wd, fin, mod_row, final, row0=0, n_rows=None):
    m_rows = x.shape[0] if n_rows is None else n_rows
    tm = TM_DENSE
    b0 = row0 // tm
    full = pl.BlockSpec((tm, D_MODEL), lambda i: (i, 0))
    vec = pl.BlockSpec((1, D_MODEL), lambda i: (0, 0))
    once = pl.Buffered(1)
    return pl.pallas_call(
        functools.partial(_ffn_kernel, final=final),
        grid=(m_rows // tm,),
        in_specs=[pl.BlockSpec((tm, D_MODEL), lambda i: (b0 + i, 0)),
                  pl.BlockSpec((1, 1, N_MOD, D_MODEL), lambda i: (layer, mod_row(b0 + i, tm), 0, 0)),
                  vec,
                  pl.BlockSpec((1, D_MODEL, D_FF), lambda i: (layer, 0, 0), pipeline_mode=once),
                  pl.BlockSpec((1, D_MODEL, D_FF), lambda i: (layer, 0, 0), pipeline_mode=once),
                  pl.BlockSpec((1, D_FF, D_MODEL), lambda i: (layer, 0, 0), pipeline_mode=once),
                  vec],
        out_specs=full,
        out_shape=jax.ShapeDtypeStruct((m_rows, D_MODEL), F32),
        compiler_params=_cparams(("parallel",)),
        name="ffn",
    )(x, mod, gain, wg, wu, wd, fin)


W_TOK_PARTS = (((O_QKV, O_Z), C_QKV), ((O_CQ, O_QKV), C_B), ((O_A, O_GATES), C_B + B_LOGITS + LOGIT_A),
               ((O_KA, O_CQ), C_KA), ((O_QA, O_KA), C_QA))
W_GZ_PARTS = (((O_GATES, W_IN_COLS), 0), ((O_Z, O_A), 3 * D_MODEL))
W_ROWS = 256


def _layout_w_in_kernel(w_ref, tok_ref, gz_ref):
    w = w_ref[0]
    for (a, b), dst in W_TOK_PARTS:
        tok_ref[0, :, dst:dst + b - a] = w[:, a:b].astype(BF16)
    pad0, pad1 = C_B + B_LOGITS + LOGIT_END, C_B + B_LOGITS + LANES
    tok_ref[0, :, pad0:pad1] = jnp.zeros((w.shape[0], pad1 - pad0), BF16)
    for (a, b), dst in W_GZ_PARTS:
        gz_ref[0, :, dst:dst + b - a] = w[:, a:b].astype(BF16)


def _layout_w_in(w_in):
    n_layers, d, n = w_in.shape
    return pl.pallas_call(
        _layout_w_in_kernel,
        grid=(n_layers, d // W_ROWS),
        in_specs=[pl.BlockSpec((1, W_ROWS, n), lambda l, i: (l, i, 0))],
        out_specs=[pl.BlockSpec((1, W_ROWS, N_IN), lambda l, i: (l, i, 0)),
                   pl.BlockSpec((1, W_ROWS, N_GZ), lambda l, i: (l, i, 0))],
        out_shape=[jax.ShapeDtypeStruct((n_layers, d, N_IN), BF16),
                   jax.ShapeDtypeStruct((n_layers, d, N_GZ), BF16)],
        compiler_params=_cparams(("parallel", "parallel")),
        name="layout_w_in",
    )(w_in)


def _layout_w_uq(w_uq):
    n_layers = w_uq.shape[0]
    w = w_uq.reshape(n_layers, Q_LORA, HEADS, NOPE_B + ROPE_B)
    w = jnp.pad(w, ((0, 0), (0, 0), (0, 0), (0, LANES - NOPE_B - ROPE_B)))
    return w.reshape(n_layers, Q_LORA, HEADS * LANES).astype(BF16)


def _layout_w_ukv(w_ukv):
    n_layers = w_ukv.shape[0]
    w = w_ukv.reshape(n_layers, KV_LORA, HEADS, 2, HD)
    zero = jnp.zeros((n_layers, KV_LORA, HEADS, HD), w.dtype)
    wk_top = jnp.concatenate([w[:, :, :, 0], zero], axis=-1).reshape(n_layers, KV_LORA, HEADS * LANES)
    place = np.zeros((LANES, HEADS, LANES), np.float32)
    for d in range(ROPE_B):
        place[d, :, NOPE_B + d] = 1.0
    place = jnp.broadcast_to(jnp.asarray(place.reshape(LANES, HEADS * LANES)), (n_layers, LANES, HEADS * LANES))
    wk = jnp.concatenate([wk_top, place], axis=1).astype(BF16)
    wv = w[:, :, :, 1].reshape(n_layers, KV_LORA, HEADS * HD).astype(BF16)
    return wk, wv


def kernel(x_prompt, x_sample, cache_ka, cache_va, cache_ckv, cache_kpe, state_fwd, state_bwd, c, c_ctx,
           w_mod, b_mod, norm1, norm2, w_in, a_qnorm, a_knorm, b_qnorm, b_kvnorm, w_uq, w_ukv, c_conv,
           c_alog, c_dt_bias, c_onorm, w_pa, w_pb, w_pc, w_out, w_gate, w_up, w_down, final_norm):
    n_ctx, s_ctx, _ = x_prompt.shape
    n_lat, s_lat, _ = x_sample.shape
    n_layers = w_mod.shape[0]
    past = cache_ka.shape[2]
    ctx_rows = n_ctx * s_ctx
    lat_rows = n_lat * s_lat
    assert s_ctx == TOK and s_lat % TOK == 0 and ctx_rows % 512 == 0 and s_lat % 512 == 0
    assert past % TOK == 0 and n_lat + 1 <= 8 and ctx_rows % s_lat == 0
    assert w_in.shape[2] == W_IN_COLS and N_IN % LANES == 0 and C_B % (B_LOGITS + LANES) == 0

    def mod_row(i, tm):
        return jnp.where(i < ctx_rows // tm, 0, 1 + (i - ctx_rows // tm) // (s_lat // tm))

    def tab_row(i):
        nb = ctx_rows // TOK_AB
        return jnp.where(i < nb, 0, 1 + (i - nb) % (s_lat // TOK_AB))

    x = (x_prompt.reshape(ctx_rows, D_MODEL), x_sample.reshape(lat_rows, D_MODEL))
    cvec = jnp.zeros((8, D_MODEL), F32).at[0].set(c_ctx).at[1:1 + n_lat].set(c)
    mod = _modulation(cvec, w_mod, b_mod).reshape(n_layers, 8, N_MOD, D_MODEL)
    cos, sin = _rope_tables(s_lat)

    w_in_p, w_gz = _layout_w_in(w_in)
    w_uq_p = _layout_w_uq(w_uq)
    w_k_p, w_v_p = _layout_w_ukv(w_ukv)
    bf = lambda w: w.astype(BF16)
    w_pa_b, w_pb_b, w_pc_b, w_out_b = bf(w_pa), bf(w_pb), bf(w_pc), bf(w_out)
    w_gate_b, w_up_b, w_down_b = bf(w_gate), bf(w_up), bf(w_down)

    cache_rows = n_lat * n_layers * past
    ck4, cv4 = _cache_a(cache_ka.reshape(cache_rows, LANES), cache_va.reshape(cache_rows, LANES))
    kpe_pad = jnp.pad(cache_kpe.reshape(cache_rows, ROPE_B), ((0, 0), (0, LANES - ROPE_B)))
    ckb, cvb = _cache_b(cache_ckv.reshape(cache_rows, KV_LORA), kpe_pad, w_k_p, w_v_p,
                        lambda i: (i // (past // TOK)) % n_layers)

    lane_vec = lambda v: jnp.zeros((1, LANES), F32).at[0, LOGIT_A:LOGIT_B].set(v.reshape(N_AB))
    new = []
    for l in range(n_layers):
        y, va_raw, qa, k4, v4, kn_a, qb, kb, vb, ckvn = _token_stage(
            x, mod, l, norm1[l][None], w_in_p, cos, sin,
            jnp.tile(a_qnorm[l], HEADS)[None], jnp.tile(a_knorm[l], KV_HEADS_A)[None],
            b_qnorm[l][None], b_kvnorm[l][None], w_uq_p, w_k_p, w_v_p, mod_row, tab_row, ctx_rows)
        conv_args = (c_conv[l, :, 0, :], lane_vec(c_alog[l]), lane_vec(c_dt_bias[l]))
        gdn_ctx = _prep_c(y, *conv_args, 0, ctx_rows, s_ctx)
        gdn_lat = _prep_c(y, *conv_args, ctx_rows, lat_rows, s_lat)

        cache_blk = (past, lambda b, l=l: b * n_layers + l)
        ya_c = _attention(qa, k4, v4, None, GROUPS_A, 0, n_ctx, s_ctx, s_ctx, None)
        ya_l = _attention(qa, k4, v4, (ck4, cv4), GROUPS_A, ctx_rows, n_lat, s_lat, 512, cache_blk)
        yb_c = _attention(qb, kb, vb, None, GROUPS_B, 0, n_ctx, s_ctx, s_ctx, None)
        yb_l = _attention(qb, kb, vb, (ckb, cvb), GROUPS_B, ctx_rows, n_lat, s_lat, 512, cache_blk)

        of_c, ob_c, sf_c, sb_c = _gdn_scan(*gdn_ctx, None, n_ctx, s_ctx)
        s0 = (_state_to_pairs(state_fwd[:, l]), _state_to_pairs(state_bwd[:, l]))
        of_l, ob_l, _, _ = _gdn_scan(*gdn_lat, s0, n_lat, s_lat)

        x_mid = _merge(x, mod, l, (ya_c, ya_l, yb_c, yb_l, of_c, of_l, ob_c, ob_l), norm1[l][None], w_gz,
                       jnp.tile(c_onorm[l], HEADS)[None], w_pa_b, w_pb_b, w_pc_b, w_out_b, mod_row, ctx_rows)
        ffn_args = (mod, l, norm2[l][None], w_gate_b, w_up_b, w_down_b, final_norm[None], mod_row)
        if l < n_layers - 1:
            x = (_ffn(x_mid, *ffn_args, final=False),)
        else:
            y_prompt = _ffn(x_mid, *ffn_args, final=True, row0=0, n_rows=ctx_rows)
            y_sample = _ffn(x_mid, *ffn_args, final=True, row0=ctx_rows, n_rows=lat_rows)

        new.append((kn_a[:ctx_rows].reshape(n_ctx, s_ctx, KV_HEADS_A, HD),
                    va_raw[:ctx_rows].reshape(n_ctx, s_ctx, KV_HEADS_A, HD),
                    ckvn[:ctx_rows].reshape(n_ctx, s_ctx, KV_LORA),
                    y[:ctx_rows, CONV_CH:CONV_CH + ROPE_B].reshape(n_ctx, s_ctx, ROPE_B),
                    _pairs_to_state(sf_c), _pairs_to_state(sb_c)))

    stacked = [jnp.stack([n[i] for n in new], axis=1) for i in range(6)]
    return (y_prompt.reshape(n_ctx, s_ctx, D_MODEL), y_sample.reshape(n_lat, s_lat, D_MODEL)) + tuple(stacked)
```
